```python
import jax, jax.numpy as jnp
from jax import lax
import numpy as np

D_MODEL = 1024
BATCH = 8
SEQ = 2048
DEPTH = 4

PLE_DIM = 256
RMS_EPS = 1e-6
LB_FLOOR = 1e-30
HG_HEADS = 4
HG_HEAD_DIM = 128
HG_WIDTH = HG_HEADS * HG_HEAD_DIM
HG_CHUNK = 64
M2_HEADS = 8
M2_HEAD_DIM = 64
M2_INNER = M2_HEADS * M2_HEAD_DIM
M2_GROUPS = 2
M2_STATE = 128
M2_CONV = 5
M2_CHUNK = 128
M2_XBC = M2_INNER + 2 * M2_GROUPS * M2_STATE
EVEN_IN_WIDTH = 5 * HG_WIDTH + M2_INNER + M2_XBC + 2 * M2_HEADS
EVEN_MIX_WIDTH = HG_WIDTH + M2_INNER
S5_GROUP_SIZE = 16
S5_GROUPS = D_MODEL // S5_GROUP_SIZE
S5_STATE = 64
S5_DT_MIN = 1e-3
S5_DT_MAX = 1e-1
D_FF = 3584
N_EXPERTS = 8
TOP_K = 2
MOE_BLOCK = 128

kernel_name = 'hybrid_hgrn2_ssd_s5_moe_encoder'

F32 = jnp.float32


def rmsnorm(x, w):
    xf = x.astype(F32)
    y = xf * lax.rsqrt(jnp.mean(xf * xf, axis=-1, keepdims=True) + RMS_EPS)
    return (y * w.astype(F32)).astype(x.dtype)


def _flip(t):
    return jnp.flip(t, axis=1)


def _masked_exp(mask, t):
    return jnp.where(mask, jnp.exp(jnp.where(mask, t, 0.0)), 0.0)


def _hgrn2_one_direction(q, k, v, logf):
    b, s, h, dk = q.shape
    dv = v.shape[-1]
    nc = s // HG_CHUNK

    def chunks(t):
        return t.reshape(b, nc, HG_CHUNK, h, t.shape[-1]).transpose(1, 0, 3, 2, 4)

    qc, kc, vc = chunks(q), chunks(k), chunks(v)
    bc = jnp.cumsum(chunks(logf), axis=3)
    mask = jnp.tril(jnp.ones((HG_CHUNK, HG_CHUNK), bool))[:, :, None]

    def step(state, inp):
        qi, ki, vi, bi = inp
        diff = bi[:, :, :, None, :] - bi[:, :, None, :, :]
        decay = _masked_exp(mask, diff)
        scores = jnp.einsum('bhtsd,bhsd->bhts', decay * qi[:, :, :, None, :], ki)
        o = (jnp.einsum('bhts,bhsv->bhtv', scores, vi)
             + jnp.einsum('bhtd,bhdv->bhtv', qi * jnp.exp(bi), state))
        b_end = bi[:, :, -1, :]
        k_dec = ki * jnp.exp(b_end[:, :, None, :] - bi)
        state = jnp.exp(b_end)[..., None] * state + jnp.einsum('bhsd,bhsv->bhdv', k_dec, vi)
        return state, o

    state0 = jnp.zeros((b, h, dk, dv), q.dtype)
    _, out = lax.scan(step, state0, (qc, kc, vc, bc))
    return out.transpose(1, 0, 3, 2, 4).reshape(b, s, h, dv)


def hgrn2_mixer(q_pre, ff_pre, fb_pre, i_pre, g_pre, lb, norm_w):
    b, s, _ = q_pre.shape
    shp = (b, s, HG_HEADS, HG_HEAD_DIM)
    q = jax.nn.silu(q_pre.astype(F32)).reshape(shp)
    v = i_pre.astype(F32).reshape(shp)
    lb = lb.astype(F32).reshape(HG_HEADS, HG_HEAD_DIM)
    log_lb, log_1m_lb = jnp.log(jnp.maximum(lb, LB_FLOOR)), jnp.log1p(-lb)

    def forget(f_pre):
        logf = jnp.logaddexp(log_lb, log_1m_lb + jax.nn.log_sigmoid(f_pre.astype(F32).reshape(shp)))
        return logf, -jnp.expm1(logf)

    logf_f, k_f = forget(ff_pre)
    logf_b, k_b = forget(fb_pre)
    o = (_hgrn2_one_direction(q, k_f, v, logf_f)
         + _flip(_hgrn2_one_direction(_flip(q), _flip(k_b), _flip(v), _flip(logf_b))))
    o = o * lax.rsqrt(jnp.mean(o * o, axis=-1, keepdims=True) + RMS_EPS)
    return o.reshape(b, s, HG_WIDTH) * norm_w.astype(F32) * jax.nn.sigmoid(g_pre.astype(F32))


def _centred_depthwise_conv(x, w, bias):
    ch, width = x.shape[-1], w.shape[0]
    y = lax.conv_general_dilated(x, w[:, None, :].astype(x.dtype), window_strides=(1,),
                                 padding=[(width // 2, width // 2)],
                                 dimension_numbers=('NWC', 'WIO', 'NWC'),
                                 feature_group_count=ch)
    return y + bias.astype(x.dtype)


def _ssd_one_direction(x, dt, a, bm, cm):
    b, s, h, p = x.shape
    g, n = bm.shape[2], bm.shape[3]
    hg, l = h // g, M2_CHUNK
    nc = s // l
    xc = x.reshape(b, nc, l, g, hg, p)
    dtc = dt.reshape(b, nc, l, g, hg)
    bc = bm.reshape(b, nc, l, g, n)
    cc = cm.reshape(b, nc, l, g, n)
    acum = jnp.cumsum(dtc * a.reshape(g, hg), axis=2)
    mask = jnp.tril(jnp.ones((l, l), bool))[:, :, None, None]
    seg = acum[:, :, :, None] - acum[:, :, None, :]
    decay = _masked_exp(mask, seg)
    cb = jnp.einsum('bclgn,bcsgn->bclsg', cc, bc)
    w = cb[..., None] * decay * dtc[:, :, None]
    y_diag = jnp.einsum('bclsgh,bcsghp->bclghp', w, xc)
    to_end = jnp.exp(acum[:, :, -1:] - acum) * dtc
    states = jnp.einsum('bclgn,bclghp->bcghpn', bc, to_end[..., None] * xc)
    chunk_decay = jnp.exp(acum[:, :, -1])

    def step(hstate, inp):
        st, dc = inp
        return dc[..., None, None] * hstate + st, hstate

    h0 = jnp.zeros((b, g, hg, p, n), x.dtype)
    _, prev = lax.scan(step, h0, (jnp.moveaxis(states, 1, 0), jnp.moveaxis(chunk_decay, 1, 0)))
    prev = jnp.moveaxis(prev, 0, 1)
    y_off = jnp.einsum('bclgn,bcghpn->bclghp', cc, prev) * jnp.exp(acum)[..., None]
    return (y_diag + y_off).reshape(b, s, h, p)


def mamba2_mixer(z, xbc, dt_raw, conv_w, conv_b, dt_bias, a_log, d_skip, norm_w):
    b, s, _ = z.shape
    xbc = jax.nn.silu(_centred_depthwise_conv(xbc, conv_w, conv_b)).astype(F32)
    nb = M2_GROUPS * M2_STATE
    xs = xbc[..., :M2_INNER].reshape(b, s, M2_HEADS, M2_HEAD_DIM)
    bm = xbc[..., M2_INNER:M2_INNER + nb].reshape(b, s, M2_GROUPS, M2_STATE)
    cm = xbc[..., M2_INNER + nb:].reshape(b, s, M2_GROUPS, M2_STATE)
    dt = jax.nn.softplus(dt_raw.astype(F32).reshape(b, s, 2, M2_HEADS) + dt_bias.astype(F32))
    a = -jnp.exp(a_log.astype(F32))
    y = (_ssd_one_direction(xs, dt[:, :, 0], a[0], bm, cm)
         + _flip(_ssd_one_direction(_flip(xs), _flip(dt[:, :, 1]), a[1], _flip(bm), _flip(cm)))
         + d_skip.astype(F32)[:, None] * xs)
    y = y.reshape(b, s, M2_INNER) * jax.nn.silu(z.astype(F32))
    yg = y.reshape(b, s, M2_GROUPS, M2_INNER // M2_GROUPS)
    yg = yg * lax.rsqrt(jnp.mean(yg * yg, axis=-1, keepdims=True) + RMS_EPS)
    return yg.reshape(b, s, M2_INNER) * norm_w.astype(F32)


def even_mixer(hn, w_in, w_out, lb, hg_norm_w, conv_w, conv_b, dt_bias, a_log, d_skip, m2_norm_w):
    proj = hn @ w_in
    sizes = [HG_WIDTH] * 5 + [M2_INNER, M2_XBC, 2 * M2_HEADS]
    q, ff, fb, i, g, z, xbc, dt_raw = jnp.split(proj, list(np.cumsum(sizes)[:-1]), axis=-1)
    o_a = hgrn2_mixer(q, ff, fb, i, g, lb, hg_norm_w)
    o_b = mamba2_mixer(z, xbc, dt_raw, conv_w, conv_b, dt_bias, a_log, d_skip, m2_norm_w)
    return jnp.concatenate([o_a, o_b], axis=-1).astype(hn.dtype) @ w_out


def _complex_affine_combine(left, right):
    a1r, a1i, b1r, b1i = left
    a2r, a2i, b2r, b2i = right
    return (a2r * a1r - a2i * a1i, a2r * a1i + a2i * a1r,
            a2r * b1r - a2i * b1i + b2r, a2r * b1i + a2i * b1r + b2i)


def _s5_one_direction(ug, a_re, a_im, log_step, b_re, b_im, c_re, c_im):
    delta = jnp.exp(log_step)[:, None]
    mag = jnp.exp(a_re * delta)
    lam_re, lam_im = mag * jnp.cos(a_im * delta), mag * jnp.sin(a_im * delta)
    den = a_re * a_re + a_im * a_im
    num_re = lam_re - 1.0
    coef_re = (num_re * a_re + lam_im * a_im) / den
    coef_im = (lam_im * a_re - num_re * a_im) / den
    bb_re = coef_re[..., None] * b_re - coef_im[..., None] * b_im
    bb_im = coef_re[..., None] * b_im + coef_im[..., None] * b_re
    bu_re = jnp.einsum('bsgc,gnc->bsgn', ug, bb_re)
    bu_im = jnp.einsum('bsgc,gnc->bsgn', ug, bb_im)
    s = ug.shape[1]
    lam_re_s = jnp.broadcast_to(lam_re, (s,) + lam_re.shape)
    lam_im_s = jnp.broadcast_to(lam_im, (s,) + lam_im.shape)

    def scan_seq(br, bi):
        _, _, hr, hi = lax.associative_scan(_complex_affine_combine, (lam_re_s, lam_im_s, br, bi), axis=0)
        return hr, hi

    h_re, h_im = jax.vmap(scan_seq)(bu_re, bu_im)
    return jnp.einsum('bsgn,gcn->bsgc', h_re, c_re) - jnp.einsum('bsgn,gcn->bsgc', h_im, c_im)


def s5_mixer(hn, a_re, a_im, log_step, b_re, b_im, c_re, c_im, d_skip, glu_w):
    b, s, d = hn.shape
    hf = hn.astype(F32)
    u = hf.reshape(b, s, S5_GROUPS, S5_GROUP_SIZE)
    a_re, a_im, log_step = a_re.astype(F32), a_im.astype(F32), log_step.astype(F32)
    b_re, b_im, c_re, c_im = b_re.astype(F32), b_im.astype(F32), c_re.astype(F32), c_im.astype(F32)
    y = (_s5_one_direction(u, a_re[0], a_im[0], log_step[0], b_re, b_im, c_re[0], c_im[0])
         + _flip(_s5_one_direction(_flip(u), a_re[1], a_im[1], log_step[1], b_re, b_im, c_re[1], c_im[1])))
    y = y.reshape(b, s, d) + d_skip.astype(F32) * hf
    out, gate = jnp.split(jax.nn.gelu(y).astype(hn.dtype) @ glu_w, 2, axis=-1)
    return out * jax.nn.sigmoid(gate)


def swiglu(h, w_gu, w_down):
    g, u = jnp.split(h @ w_gu, 2, axis=-1)
    return (jax.nn.silu(g) * u) @ w_down


def moe_swiglu(h, w_router, w_gu, w_down):
    b, s, d = h.shape
    t = b * s
    xt = h.reshape(t, d)
    logits = (xt @ w_router).astype(F32)
    top_logit, top_e = lax.top_k(logits, TOP_K)
    gates = jax.nn.softmax(top_logit, axis=-1).astype(h.dtype)
    n_assign = t * TOP_K
    flat_e = top_e.reshape(-1)
    flat_tok = jnp.repeat(jnp.arange(t, dtype=jnp.int32), TOP_K)
    flat_w = gates.reshape(-1)
    order = jnp.argsort(flat_e)
    se, stok, sw = flat_e[order], flat_tok[order], flat_w[order]
    counts = jnp.bincount(flat_e, length=N_EXPERTS)
    start = jnp.cumsum(counts) - counts
    padded = (counts + MOE_BLOCK - 1) // MOE_BLOCK * MOE_BLOCK
    pend = jnp.cumsum(padded)
    pstart = pend - padded
    dest = pstart[se] + jnp.arange(n_assign, dtype=jnp.int32) - start[se]
    n_rows = (-(-n_assign // MOE_BLOCK) + N_EXPERTS) * MOE_BLOCK
    n_blocks = n_rows // MOE_BLOCK
    x_buf = jnp.zeros((n_rows, d), h.dtype).at[dest].set(xt[stok])
    w_buf = jnp.zeros((n_rows,), h.dtype).at[dest].set(sw)
    tok_buf = jnp.zeros((n_rows,), jnp.int32).at[dest].set(stok)
    blk_e = jnp.minimum(jnp.searchsorted(pend, jnp.arange(n_blocks, dtype=jnp.int32) * MOE_BLOCK, side='right'),
                        N_EXPERTS - 1)

    def expert_block(args):
        xb, e = args
        return swiglu(xb, w_gu[e], w_down[e])

    y_buf = lax.map(expert_block, (x_buf.reshape(n_blocks, MOE_BLOCK, d), blk_e)).reshape(n_rows, d)
    out = jnp.zeros((t, d), h.dtype).at[tok_buf].add(y_buf * w_buf[:, None])
    return out.reshape(b, s, d)


def setup_inputs(seed: int = 0) -> dict:
    key = jax.random.key(seed)
    ks = iter(jax.random.split(key, 48))
    n_even, n_odd = (DEPTH + 1) // 2, DEPTH // 2

    def nrm(shape, scale):
        return scale * jax.random.normal(next(ks), shape, F32)

    def gain(shape):
        return 1.0 + 0.02 * jax.random.normal(next(ks), shape, F32)

    def unif(shape, lo, hi):
        return jax.random.uniform(next(ks), shape, F32, lo, hi)

    dt0 = jnp.exp(unif((n_even, 2, M2_HEADS), float(np.log(1e-3)), float(np.log(1e-1))))
    s5_a_re = -0.5 * jnp.exp(nrm((n_odd, 2, S5_GROUPS, S5_STATE), 0.02))
    s5_a_im = jnp.pi * jnp.arange(S5_STATE, dtype=F32) + nrm((n_odd, 2, S5_GROUPS, S5_STATE), 0.02)
    return {
        'x': nrm((BATCH, SEQ, D_MODEL), 1.0),
        'p': nrm((DEPTH, BATCH, SEQ, PLE_DIM), 1.0),
        'norm_mix': gain((DEPTH, D_MODEL)),
        'norm_ffn': gain((DEPTH, D_MODEL)),
        'norm_ple': gain((DEPTH, D_MODEL)),
        'final_norm': gain((D_MODEL,)),
        'ple_gate': nrm((DEPTH, D_MODEL, D_MODEL), D_MODEL ** -0.5),
        'ple_proj': nrm((DEPTH, PLE_DIM, D_MODEL), PLE_DIM ** -0.5),
        'ev_w_in': nrm((n_even, D_MODEL, EVEN_IN_WIDTH), D_MODEL ** -0.5),
        'ev_w_out': nrm((n_even, EVEN_MIX_WIDTH, D_MODEL), EVEN_MIX_WIDTH ** -0.5),
        'hg_lb_logits': 1.0 + nrm((n_even, HG_WIDTH), 0.1),
        'hg_norm_w': gain((n_even, HG_WIDTH)),
        'm2_conv_w': nrm((n_even, M2_CONV, M2_XBC), M2_CONV ** -0.5),
        'm2_conv_b': nrm((n_even, M2_XBC), 0.01),
        'm2_dt_bias': dt0 + jnp.log(-jnp.expm1(-dt0)),
        'm2_a_log': jnp.log(unif((n_even, 2, M2_HEADS), 1.0, 16.0)),
        'm2_d': gain((n_even, M2_HEADS)),
        'm2_norm_w': gain((n_even, M2_INNER)),
        's5_a_re': s5_a_re,
        's5_a_im': s5_a_im,
        's5_log_step': unif((n_odd, 2, S5_GROUPS), float(np.log(S5_DT_MIN)), float(np.log(S5_DT_MAX))),
        's5_b_re': nrm((n_odd, S5_GROUPS, S5_STATE, S5_GROUP_SIZE), (2 * S5_GROUP_SIZE) ** -0.5),
        's5_b_im': nrm((n_odd, S5_GROUPS, S5_STATE, S5_GROUP_SIZE), (2 * S5_GROUP_SIZE) ** -0.5),
        's5_c_re': nrm((n_odd, 2, S5_GROUPS, S5_GROUP_SIZE, S5_STATE), (2 * S5_STATE) ** -0.5),
        's5_c_im': nrm((n_odd, 2, S5_GROUPS, S5_GROUP_SIZE, S5_STATE), (2 * S5_STATE) ** -0.5),
        's5_d': nrm((n_odd, D_MODEL), 1.0),
        's5_glu_w': nrm((n_odd, D_MODEL, 2 * D_MODEL), D_MODEL ** -0.5),
        'ffn_w_gu': nrm((n_even, D_MODEL, 2 * D_FF), D_MODEL ** -0.5),
        'ffn_w_down': nrm((n_even, D_FF, D_MODEL), D_FF ** -0.5),
        'moe_router': nrm((n_odd, D_MODEL, N_EXPERTS), D_MODEL ** -0.5),
        'moe_w_gu': nrm((n_odd, N_EXPERTS, D_MODEL, 2 * D_FF), D_MODEL ** -0.5),
        'moe_w_down': nrm((n_odd, N_EXPERTS, D_FF, D_MODEL), D_FF ** -0.5),
    }


def reference(x, p, norm_mix, norm_ffn, norm_ple, final_norm, ple_gate, ple_proj, ev_w_in, ev_w_out,
              hg_lb_logits, hg_norm_w, m2_conv_w, m2_conv_b, m2_dt_bias, m2_a_log, m2_d, m2_norm_w,
              s5_a_re, s5_a_im, s5_log_step, s5_b_re, s5_b_im, s5_c_re, s5_c_im, s5_d, s5_glu_w,
              ffn_w_gu, ffn_w_down, moe_router, moe_w_gu, moe_w_down):
    lb_soft = jax.nn.softmax(hg_lb_logits.astype(F32), axis=0)
    hg_lb = jnp.cumsum(lb_soft, axis=0) - lb_soft[0]
    h = x
    for layer in range(DEPTH):
        j = layer // 2
        hn = rmsnorm(h, norm_mix[layer])
        if layer % 2 == 0:
            h = h + even_mixer(hn, ev_w_in[j], ev_w_out[j], hg_lb[j], hg_norm_w[j], m2_conv_w[j], m2_conv_b[j],
                               m2_dt_bias[j], m2_a_log[j], m2_d[j], m2_norm_w[j])
            h = h + swiglu(rmsnorm(h, norm_ffn[layer]), ffn_w_gu[j], ffn_w_down[j])
        else:
            h = h + s5_mixer(hn, s5_a_re[j], s5_a_im[j], s5_log_step[j], s5_b_re[j], s5_b_im[j],
                             s5_c_re[j], s5_c_im[j], s5_d[j], s5_glu_w[j])
            h = h + moe_swiglu(rmsnorm(h, norm_ffn[layer]), moe_router[j], moe_w_gu[j], moe_w_down[j])
        gate = jax.nn.sigmoid(rmsnorm(h, norm_ple[layer]) @ ple_gate[layer])
        h = h + gate * (p[layer] @ ple_proj[layer])
    return rmsnorm(h, final_norm)
```

```python
import functools

import numpy as np
import jax
import jax.numpy as jnp
from jax import lax
from jax.experimental import pallas as pl
from jax.experimental.pallas import tpu as pltpu

F32 = jnp.float32
BF16 = jnp.bfloat16

RMS_EPS = 1e-6
LB_FLOOR = 1e-30
LANE = 128
VMEM_LIMIT = 56 * 1024 * 1024

HG_HEADS = 4
HG_DIM = 128
HG_WIDTH = HG_HEADS * HG_DIM
HG_CHUNK = 64
M2_HEADS = 8
M2_P = 64
M2_INNER = M2_HEADS * M2_P
M2_GROUPS = 2
M2_N = 128
M2_CHUNK = 128
M2_XBC = M2_INNER + 2 * M2_GROUPS * M2_N
S5_GROUP_SIZE = 16
S5_STATE = 64
S5_STEPS = 16
S5_SUPER = 32
N_EXPERTS = 8
MOE_BM = 512

OFF_Q, OFF_FF, OFF_FB, OFF_I, OFF_G = (k * HG_WIDTH for k in range(5))
OFF_Z = 5 * HG_WIDTH
OFF_XBC = OFF_Z + M2_INNER
OFF_DT = OFF_XBC + M2_XBC
EVEN_IN_PAD = OFF_DT + LANE


def _cparams(*sem):
    return pltpu.CompilerParams(dimension_semantics=sem, vmem_limit_bytes=VMEM_LIMIT)


def _rms(x, w):
    return x * lax.rsqrt(jnp.mean(x * x, axis=-1, keepdims=True) + RMS_EPS) * w


def _sigmoid(x):
    return 1.0 / (1.0 + jnp.exp(-x))


def _silu(x):
    return x * _sigmoid(x)


def _dot(a, b):
    return jnp.dot(a, b, preferred_element_type=F32)


def _dot_nt(a, b):
    return lax.dot_general(a, b, (((1,), (1,)), ((), ())), preferred_element_type=F32)


def _split3(x):
    hi = x.astype(BF16)
    r1 = x - hi.astype(F32)
    mid = r1.astype(BF16)
    lo = (r1 - mid.astype(F32)).astype(BF16)
    return hi, mid, lo


def _rms_matmul_kernel(x_ref, nw_ref, w_ref, o_ref):
    xn = _rms(x_ref[...], nw_ref[...])
    o_ref[...] = _dot(xn.astype(BF16), w_ref[...])


def rms_matmul(x, nw, w, tm=256):
    m, k = x.shape
    n = w.shape[1]
    return pl.pallas_call(
        _rms_matmul_kernel,
        grid=(m // tm,),
        in_specs=[pl.BlockSpec((tm, k), lambda i: (i, 0)),
                  pl.BlockSpec((1, k), lambda i: (0, 0)),
                  pl.BlockSpec((k, n), lambda i: (0, 0))],
        out_specs=pl.BlockSpec((tm, n), lambda i: (i, 0)),
        out_shape=jax.ShapeDtypeStruct((m, n), F32),
        compiler_params=_cparams("parallel"),
        name="rms_matmul",
    )(x, nw.reshape(1, k), w)


def _rms_cast_kernel(x_ref, nw_ref, o_ref):
    o_ref[...] = _rms(x_ref[...], nw_ref[...]).astype(o_ref.dtype)


def rms_cast(x, nw, tm=512):
    m, k = x.shape
    return pl.pallas_call(
        _rms_cast_kernel,
        grid=(m // tm,),
        in_specs=[pl.BlockSpec((tm, k), lambda i: (i, 0)),
                  pl.BlockSpec((1, k), lambda i: (0, 0))],
        out_specs=pl.BlockSpec((tm, k), lambda i: (i, 0)),
        out_shape=jax.ShapeDtypeStruct((m, k), BF16),
        compiler_params=_cparams("parallel"),
        name="rms_cast",
    )(x, nw.reshape(1, k))


def _hgrn2_tables(L):
    t = np.arange(L)
    rq, rk, masks = [], [], []
    w = L // 2
    while w >= 1:
        blk = t // w
        same = blk[:, None] == blk[None, :]
        odd = blk % 2 == 1
        rq.append(same & (t[None, :] <= t[:, None]) & odd[:, None])
        rk.append(same & (t[None, :] > t[:, None]) & (~odd)[:, None])
        masks.append(odd[:, None] & (blk[None, :] == blk[:, None] - 1))
        w //= 2
    tri = t[None, :] <= t[:, None]
    suffix = t[None, :] > t[:, None]
    blocks = rq + rk + [tri, suffix]
    masks.append(np.eye(L, dtype=bool))
    ones = np.ones((8, L), bool)
    e_f = np.concatenate(blocks + [ones], 0)
    e_b = np.concatenate([b[::-1, ::-1] for b in blocks] + [ones], 0)
    m_f = np.stack(masks)
    m_b = m_f[:, ::-1, ::-1]
    return (e_f.astype(np.float32), e_b.astype(np.float32),
            m_f.astype(np.float32), m_b.astype(np.float32), len(rq))


def _hgrn2_kernel(qp_ref, ffp_ref, fbp_ref, ip_ref, gp_ref, lb_ref, nw_ref,
                  ef_ref, eb_ref, mf_ref, mb_ref, o_ref, of_scr, st_scr, *, L, nl):
    S = qp_ref.shape[0]
    nc = S // L
    lb = lb_ref[...]
    lb_floor = jnp.maximum(lb, LB_FLOOR)
    one_m_lb = 1.0 - lb

    def chunk(c, fpre_ref, e_ref, m_ref):
        r0 = pl.multiple_of(c * L, L)
        q = _silu(qp_ref[pl.ds(r0, L), :])
        v = ip_ref[pl.ds(r0, L), :]
        f = lb_floor + one_m_lb * _sigmoid(fpre_ref[pl.ds(r0, L), :])
        logf = jnp.log(f)
        k = 1.0 - f
        hi, mid, lo = _split3(logf)
        e = e_ref[...]
        ex = _dot(e, hi) + _dot(e, mid) + _dot(e, lo)
        qb = q.astype(BF16)
        kb = k.astype(BF16)
        a = m_ref[nl] * _dot_nt(qb, kb)
        for lvl in range(nl):
            qt = (q * jnp.exp(ex[lvl * L:(lvl + 1) * L])).astype(BF16)
            kt = (k * jnp.exp(ex[(nl + lvl) * L:(nl + lvl + 1) * L])).astype(BF16)
            a = a + m_ref[lvl] * _dot_nt(qt, kt)
        base = 2 * nl * L
        q_in = (q * jnp.exp(ex[base:base + L])).astype(BF16)
        k_out = (k * jnp.exp(ex[base + L:base + 2 * L])).astype(BF16)
        tot = jnp.exp(ex[base + 2 * L:base + 2 * L + 1])
        st = st_scr[...]
        vb = v.astype(BF16)
        o = _dot(a.astype(BF16), vb) + _dot_nt(q_in, st.astype(BF16))
        st_scr[...] = st * tot + _dot(v.T.astype(BF16), k_out)
        return r0, o

    st_scr[...] = jnp.zeros_like(st_scr)

    def fwd_body(ci, carry):
        r0, o = chunk(ci, ffp_ref, ef_ref, mf_ref)
        of_scr[pl.ds(r0, L), :] = o
        return carry

    lax.fori_loop(0, nc, fwd_body, 0)
    st_scr[...] = jnp.zeros_like(st_scr)
    nw = nw_ref[...]

    def bwd_body(ci, carry):
        r0, o = chunk(nc - 1 - ci, fbp_ref, eb_ref, mb_ref)
        o = o + of_scr[pl.ds(r0, L), :]
        o = o * lax.rsqrt(jnp.mean(o * o, axis=-1, keepdims=True) + RMS_EPS)
        o = o * nw * _sigmoid(gp_ref[pl.ds(r0, L), :])
        o_ref[pl.ds(r0, L), :] = o.astype(o_ref.dtype)
        return carry

    lax.fori_loop(0, nc, bwd_body, 0)


def hgrn2(proj, lb, norm_w, L=HG_CHUNK):
    B, S, _ = proj.shape
    e_f, e_b, m_f, m_b, nl = _hgrn2_tables(L)
    d = HG_DIM
    nb = HG_WIDTH // d

    def col(off):
        return pl.BlockSpec((None, S, d), lambda b, h, off=off: (b, 0, off // d + h))

    vec = pl.BlockSpec((1, d), lambda b, h: (0, h))
    const2 = lambda a: pl.BlockSpec(a.shape, lambda b, h: (0, 0))
    const3 = lambda a: pl.BlockSpec(a.shape, lambda b, h: (0, 0, 0))
    return pl.pallas_call(
        functools.partial(_hgrn2_kernel, L=L, nl=nl),
        grid=(B, nb),
        in_specs=[col(OFF_Q), col(OFF_FF), col(OFF_FB), col(OFF_I), col(OFF_G), vec, vec,
                  const2(e_f), const2(e_b), const3(m_f), const3(m_b)],
        out_specs=pl.BlockSpec((None, S, d), lambda b, h: (b, 0, h)),
        out_shape=jax.ShapeDtypeStruct((B, S, HG_WIDTH), BF16),
        scratch_shapes=[pltpu.VMEM((S, d), F32), pltpu.VMEM((d, d), F32)],
        compiler_params=_cparams("parallel", "parallel"),
        name="hgrn2",
    )(proj, proj, proj, proj, proj, lb.reshape(1, HG_WIDTH), norm_w.reshape(1, HG_WIDTH),
      jnp.asarray(e_f, BF16), jnp.asarray(e_b, BF16), jnp.asarray(m_f), jnp.asarray(m_b))


def _shift_rows(x, k):
    if k == 0:
        return x
    n = x.shape[0]
    rolled = pltpu.roll(x, (-k) % n, 0)
    t = lax.broadcasted_iota(jnp.int32, x.shape, 0)
    ok = (t + k >= 0) & (t + k < n)
    return jnp.where(ok, rolled, 0.0)


def _conv_silu(x, w, b):
    half = w.shape[0] // 2
    acc = b
    for j in range(w.shape[0]):
        acc = acc + w[j:j + 1, :] * _shift_rows(x, j - half)
    return _silu(acc)


def _ssd_kernel(x_ref, b_ref, c_ref, z_ref, dtc_ref, dtr_ref,
                cwx_ref, cwb_ref, cwc_ref, cbx_ref, cbb_ref, cbc_ref,
                dtbc_ref, dtbr_ref, alr_ref, alc_ref, dsk_ref, nw_ref,
                tril_ref, triu_ref,
                o_ref, xs_scr, bs_scr, cs_scr, y_scr, st_scr, *, L, hg, P):
    g = pl.program_id(1)
    S = x_ref.shape[0]
    nc = S // L
    nh = 2 * hg
    xs_scr[...] = _conv_silu(x_ref[...], cwx_ref[...], cbx_ref[...])
    bs_scr[...] = _conv_silu(b_ref[...], cwb_ref[...], cbb_ref[...])
    cs_scr[...] = _conv_silu(c_ref[...], cwc_ref[...], cbc_ref[...])

    a_row = -jnp.exp(alr_ref[...])
    a_col = -jnp.exp(alc_ref[...])
    tril = tril_ref[...]
    triu = triu_ref[...]
    ti = lax.broadcasted_iota(jnp.int32, (L, L), 0)
    si = lax.broadcasted_iota(jnp.int32, (L, L), 1)
    lane2 = lax.broadcasted_iota(jnp.int32, (L, 2 * P), 1)

    def softplus(v):
        return jnp.maximum(v, 0.0) + jnp.log(1.0 + jnp.exp(-jnp.abs(v)))

    def per_head(cols):
        tiles = []
        for j in range(0, hg, 2):
            lo = jnp.broadcast_to(cols[j], (L, 2 * P))
            hi = jnp.broadcast_to(cols[j + 1], (L, 2 * P))
            tiles.append(jnp.where(lane2 < P, lo, hi))
        return jnp.concatenate(tiles, axis=1)

    def direction(d, first):
        st_scr[...] = jnp.zeros_like(st_scr)
        cum_c = tril if d == 0 else triu
        cum_r = triu if d == 0 else tril

        def body(ci, carry):
            c = ci if d == 0 else nc - 1 - ci
            r0 = pl.multiple_of(c * L, L)
            x = xs_scr[pl.ds(r0, L), :]
            bm = bs_scr[pl.ds(r0, L), :]
            cm = cs_scr[pl.ds(r0, L), :]
            dt_c = softplus(dtc_ref[pl.ds(r0, L), :] + dtbc_ref[...])
            dt_r = softplus(dtr_ref[:, pl.ds(r0, L)] + dtbr_ref[...])
            h1, h2, h3 = _split3(dt_c * a_row)
            acum_c = _dot(cum_c, h1) + _dot(cum_c, h2) + _dot(cum_c, h3)
            g1, g2, g3 = _split3(dt_r * a_col)
            acum_r = _dot(g1, cum_r) + _dot(g2, cum_r) + _dot(g3, cum_r)
            ones_rows = jnp.ones((8, L), BF16)
            tot = (_dot(ones_rows, h1) + _dot(ones_rows, h2) + _dot(ones_rows, h3))[0:1]
            cb = _dot_nt(cm.astype(BF16), bm.astype(BF16))
            keep = (si <= ti) if d == 0 else (si >= ti)
            ys, in_cols, out_cols, dec_cols = [], [], [], []
            for hh in range(hg):
                j = d * nh + g * hg + hh
                sel_c = (lax.broadcasted_iota(jnp.int32, (L, LANE), 1) == j)
                ac = jnp.sum(jnp.where(sel_c, acum_c, 0.0), axis=1, keepdims=True)
                dc = jnp.sum(jnp.where(sel_c, dt_c, 0.0), axis=1, keepdims=True)
                te = jnp.sum(jnp.where(sel_c[0:1], tot, 0.0), axis=1, keepdims=True)
                sel_r = (lax.broadcasted_iota(jnp.int32, (2 * nh, L), 0) == j)
                ar = jnp.sum(jnp.where(sel_r, acum_r, 0.0), axis=0, keepdims=True)
                dr = jnp.sum(jnp.where(sel_r, dt_r, 0.0), axis=0, keepdims=True)
                seg = ac - ar
                decay = jnp.where(keep, jnp.exp(jnp.where(keep, seg, 0.0)), 0.0)
                w = (cb * decay * dr).astype(BF16)
                ys.append(_dot(w, x[:, hh * P:(hh + 1) * P].astype(BF16)))
                in_cols.append(jnp.exp(te - ac) * dc)
                out_cols.append(jnp.exp(ac))
                dec_cols.append(jnp.broadcast_to(jnp.exp(te), (1, P)))
            y_diag = jnp.concatenate(ys, axis=1)
            st = st_scr[...]
            y_off = _dot(cm.astype(BF16), st.astype(BF16)) * per_head(out_cols)
            x_in = (x * per_head(in_cols)).astype(BF16)
            st_scr[...] = st * jnp.concatenate(dec_cols, axis=1) + _dot(bm.T.astype(BF16), x_in)
            y = y_diag + y_off
            if first:
                y_scr[pl.ds(r0, L), :] = y
            else:
                y = y + y_scr[pl.ds(r0, L), :] + dsk_ref[...] * x
                zz = z_ref[pl.ds(r0, L), :]
                y = y * _silu(zz)
                y = y * lax.rsqrt(jnp.mean(y * y, axis=-1, keepdims=True) + RMS_EPS)
                o_ref[pl.ds(r0, L), :] = (y * nw_ref[...]).astype(o_ref.dtype)
            return carry

        lax.fori_loop(0, nc, body, 0)

    direction(0, True)
    direction(1, False)


def ssd(proj, dt_rows, conv_w, conv_b, dt_bias, a_log, d_skip, norm_w, L=M2_CHUNK):
    B, S, _ = proj.shape
    hg = M2_HEADS // M2_GROUPS
    gw = hg * M2_P
    t = np.arange(L)
    tril = jnp.asarray((t[None, :] <= t[:, None]).astype(np.float32), BF16)
    triu = jnp.asarray((t[None, :] >= t[:, None]).astype(np.float32), BF16)
    nb_c = OFF_XBC + M2_INNER
    nc_c = nb_c + M2_GROUPS * M2_N
    xw = conv_w[:, :M2_INNER]
    bw = conv_w[:, M2_INNER:M2_INNER + M2_GROUPS * M2_N]
    cw = conv_w[:, M2_INNER + M2_GROUPS * M2_N:]
    cb2 = conv_b.reshape(1, -1)
    xb = cb2[:, :M2_INNER]
    bb = cb2[:, M2_INNER:M2_INNER + M2_GROUPS * M2_N]
    cbb = cb2[:, M2_INNER + M2_GROUPS * M2_N:]
    nh2 = 2 * M2_HEADS
    dtb_row = jnp.zeros((1, LANE), F32).at[0, :nh2].set(dt_bias.reshape(-1))
    al_row = jnp.zeros((1, LANE), F32).at[0, :nh2].set(a_log.reshape(-1))
    dtb_col = dt_bias.reshape(nh2, 1)
    al_col = a_log.reshape(nh2, 1)
    dsk = jnp.repeat(d_skip, M2_P).reshape(1, M2_INNER)
    nw = norm_w.reshape(1, M2_INNER)

    full2 = lambda a: pl.BlockSpec(a.shape, lambda b, g: (0, 0))
    return pl.pallas_call(
        functools.partial(_ssd_kernel, L=L, hg=hg, P=M2_P),
        grid=(B, M2_GROUPS),
        in_specs=[
            pl.BlockSpec((None, S, gw), lambda b, g: (b, 0, OFF_XBC // gw + g)),
            pl.BlockSpec((None, S, M2_N), lambda b, g: (b, 0, nb_c // M2_N + g)),
            pl.BlockSpec((None, S, M2_N), lambda b, g: (b, 0, nc_c // M2_N + g)),
            pl.BlockSpec((None, S, gw), lambda b, g: (b, 0, OFF_Z // gw + g)),
            pl.BlockSpec((None, S, LANE), lambda b, g: (b, 0, OFF_DT // LANE)),
            pl.BlockSpec((None, nh2, S), lambda b, g: (b, 0, 0)),
            pl.BlockSpec((xw.shape[0], gw), lambda b, g: (0, g)),
            pl.BlockSpec((bw.shape[0], M2_N), lambda b, g: (0, g)),
            pl.BlockSpec((cw.shape[0], M2_N), lambda b, g: (0, g)),
            pl.BlockSpec((1, gw), lambda b, g: (0, g)),
            pl.BlockSpec((1, M2_N), lambda b, g: (0, g)),
            pl.BlockSpec((1, M2_N), lambda b, g: (0, g)),
            full2(dtb_row), full2(dtb_col), full2(al_row), full2(al_col),
            pl.BlockSpec((1, gw), lambda b, g: (0, g)),
            pl.BlockSpec((1, gw), lambda b, g: (0, g)),
            full2(tril), full2(triu),
        ],
        out_specs=pl.BlockSpec((None, S, gw), lambda b, g: (b, 0, g)),
        out_shape=jax.ShapeDtypeStruct((B, S, M2_INNER), BF16),
        scratch_shapes=[pltpu.VMEM((S, gw), F32), pltpu.VMEM((S, M2_N), F32),
                        pltpu.VMEM((S, M2_N), F32), pltpu.VMEM((S, gw), F32),
                        pltpu.VMEM((M2_N, gw), F32)],
        compiler_params=_cparams("parallel", "parallel"),
        name="ssd",
    )(proj, proj, proj, proj, proj, dt_rows, xw, bw, cw, xb, bb, cbb,
      dtb_row, dtb_col, al_row, al_col, dsk, nw, tril, triu)


def _mix_out_kernel(h_ref, a_ref, b_ref, wa_ref, wb_ref, o_ref):
    o_ref[...] = h_ref[...] + _dot(a_ref[...], wa_ref[...]) + _dot(b_ref[...], wb_ref[...])


def mix_out(h, oa, ob, wa, wb, tm=512):
    m, n = h.shape
    return pl.pallas_call(
        _mix_out_kernel,
        grid=(m // tm,),
        in_specs=[pl.BlockSpec((tm, n), lambda i: (i, 0)),
                  pl.BlockSpec((tm, oa.shape[1]), lambda i: (i, 0)),
                  pl.BlockSpec((tm, ob.shape[1]), lambda i: (i, 0)),
                  pl.BlockSpec(wa.shape, lambda i: (0, 0)),
                  pl.BlockSpec(wb.shape, lambda i: (0, 0))],
        out_specs=pl.BlockSpec((tm, n), lambda i: (i, 0)),
        out_shape=jax.ShapeDtypeStruct((m, n), F32),
        compiler_params=_cparams("parallel"),
        name="mix_out",
    )(h, oa, ob, wa, wb)


def _ffn_kernel(h_ref, nw_ref, wg_ref, wu_ref, wd_ref, o_ref, xn_scr, acc_scr):
    f = pl.program_id(1)

    @pl.when(f == 0)
    def _():
        x = h_ref[...]
        xn_scr[...] = _rms(x, nw_ref[...]).astype(BF16)
        acc_scr[...] = x

    xn = xn_scr[...]
    act = _silu(_dot(xn, wg_ref[...])) * _dot(xn, wu_ref[...])
    acc_scr[...] += _dot(act.astype(BF16), wd_ref[...])

    @pl.when(f == pl.num_programs(1) - 1)
    def _():
        o_ref[...] = acc_scr[...]


def ffn(h, nw, w_gu, w_down, tm=1024, tf=512):
    m, d = h.shape
    dff = w_down.shape[0]
    nf = dff // tf
    return pl.pallas_call(
        _ffn_kernel,
        grid=(m // tm, nf),
        in_specs=[pl.BlockSpec((tm, d), lambda i, f: (i, 0)),
                  pl.BlockSpec((1, d), lambda i, f: (0, 0)),
                  pl.BlockSpec((d, tf), lambda i, f: (0, f)),
                  pl.BlockSpec((d, tf), lambda i, f: (0, nf + f)),
                  pl.BlockSpec((tf, d), lambda i, f: (f, 0))],
        out_specs=pl.BlockSpec((tm, d), lambda i, f: (i, 0)),
        out_shape=jax.ShapeDtypeStruct((m, d), F32),
        scratch_shapes=[pltpu.VMEM((tm, d), BF16), pltpu.VMEM((tm, d), F32)],
        compiler_params=_cparams("parallel", "arbitrary"),
        name="ffn",
    )(h, nw.reshape(1, d), w_gu, w_gu, w_down)


def _ple_kernel(*refs, moe, final):
    if moe:
        h_ref, y2_ref, meta_ref, p_ref, nw_ref, wg_ref, wp_ref, fw_ref, o_ref = refs
        meta = meta_ref[...]
        d = h_ref.shape[1]
        h = h_ref[...] + (meta[:, 2:3] * y2_ref[:, 0:d] + meta[:, 3:4] * y2_ref[:, d:2 * d])
    else:
        h_ref, p_ref, nw_ref, wg_ref, wp_ref, fw_ref, o_ref = refs
        h = h_ref[...]
    gate = _sigmoid(_dot(_rms(h, nw_ref[...]).astype(BF16), wg_ref[...]))
    h = h + gate * _dot(p_ref[...].astype(BF16), wp_ref[...])
    if final:
        h = _rms(h, fw_ref[...])
    o_ref[...] = h


def ple(h, p, nw, wg, wp, fw, y2=None, meta=None, final=False, tm=512):
    m, d = h.shape
    moe = y2 is not None
    row = lambda w: pl.BlockSpec((tm, w), lambda i: (i, 0))
    vec = pl.BlockSpec((1, d), lambda i: (0, 0))
    in_specs = [row(d)]
    args = [h]
    if moe:
        in_specs += [row(2 * d), row(meta.shape[1])]
        args += [y2, meta]
    in_specs += [row(p.shape[1]), vec, pl.BlockSpec(wg.shape, lambda i: (0, 0)),
                 pl.BlockSpec(wp.shape, lambda i: (0, 0)), vec]
    args += [p, nw.reshape(1, d), wg, wp, fw.reshape(1, d)]
    return pl.pallas_call(
        functools.partial(_ple_kernel, moe=moe, final=final),
        grid=(m // tm,),
        in_specs=in_specs,
        out_specs=row(d),
        out_shape=jax.ShapeDtypeStruct((m, d), F32),
        compiler_params=_cparams("parallel"),
        name="ple",
    )(*args)


def _s5_tables(a_re, a_im, log_step, b_re, b_im, c_re, c_im):
    G, N = a_re.shape[1:]
    C = S5_GROUP_SIZE
    T = S5_STEPS
    gp = S5_SUPER // C
    SG = G // gp
    tau = jnp.arange(T + 1, dtype=F32)

    def one_dir(d):
        delta = jnp.exp(log_step[d])[:, None]
        ar, ai = a_re[d], a_im[d]
        mag = jnp.exp(ar * delta)
        lam_re, lam_im = mag * jnp.cos(ai * delta), mag * jnp.sin(ai * delta)
        den = ar * ar + ai * ai
        num_re = lam_re - 1.0
        coef_re = (num_re * ar + lam_im * ai) / den
        coef_im = (lam_im * ar - num_re * ai) / den
        bb_re = coef_re[..., None] * b_re - coef_im[..., None] * b_im
        bb_im = coef_re[..., None] * b_im + coef_im[..., None] * b_re
        pm = jnp.exp((ar * delta)[None] * tau[:, None, None])
        ang = (ai * delta)[None] * tau[:, None, None]
        return (pm * jnp.cos(ang), pm * jnp.sin(ang)), (bb_re, bb_im), (c_re[d], c_im[d])

    def cmul(xr, xi, yr, yi):
        return xr * yr - xi * yi, xr * yi + xi * yr

    wst_cols, wc_rows, kin_parts, lam_rows = [], [], [], []
    for d in range(2):
        (pr, pi), (br, bi), (cr, ci) = one_dir(d)
        steps = jnp.arange(T)
        e_in = (T - 1 - steps) if d == 0 else steps
        sr, si = cmul(pr[e_in][..., None], pi[e_in][..., None], br[None], bi[None])
        e_out = (steps + 1) if d == 0 else (T - steps)
        cpr, cpi = cmul(cr[None], ci[None], pr[e_out][:, :, None, :], pi[e_out][:, :, None, :])
        lbr, lbi = cmul(pr[:T][..., None], pi[:T][..., None], br[None], bi[None])
        ktau = (jnp.einsum('gon,tgni->tgoi', cr, lbr) - jnp.einsum('gon,tgni->tgoi', ci, lbi))
        wst_cols.append((sr, si))
        wc_rows.append((cpr, -cpi))
        kin_parts.append(ktau)
        lam_rows += [pr[T], pi[T]]

    K = T * S5_SUPER
    eye_gp = jnp.eye(gp, dtype=F32)

    def to_in(m):
        m = m.reshape(T, SG, gp, N, C)
        out = jnp.einsum('tsgnc,gh->stgchn', m, eye_gp)
        return out.reshape(SG, K, gp * N)

    def to_out(m):
        m = m.reshape(T, SG, gp, C, N)
        out = jnp.einsum('tsgcn,gh->shntgc', m, eye_gp)
        return out.reshape(SG, gp * N, K)

    wst = jnp.concatenate([to_in(wst_cols[0][0]), to_in(wst_cols[0][1]),
                           to_in(wst_cols[1][0]), to_in(wst_cols[1][1])], axis=2)
    wc = jnp.concatenate([to_out(wc_rows[0][0]), to_out(wc_rows[0][1]),
                          to_out(wc_rows[1][0]), to_out(wc_rows[1][1])], axis=1)
    sp = jnp.arange(T)[:, None]
    tp = jnp.arange(T)[None, :]
    kf, kb = kin_parts
    lag_f = jnp.clip(tp - sp, 0, T - 1)
    lag_b = jnp.clip(sp - tp, 0, T - 1)
    toe = (jnp.where((tp >= sp)[..., None, None, None], kf[lag_f], 0.0)
           + jnp.where((sp >= tp)[..., None, None, None], kb[lag_b], 0.0))
    toe = toe.reshape(T, T, SG, gp, C, C)
    kin = jnp.einsum('stzgoi,gh->zsgitho', toe, eye_gp).reshape(SG, K, K)
    lam = jnp.stack([r.reshape(SG, gp * N) for r in lam_rows], axis=1)
    lam = jnp.concatenate([lam, jnp.zeros((SG, 4, gp * N), F32)], axis=1)
    return wst.astype(BF16), kin.astype(BF16), wc.astype(BF16), lam


def _s5_kernel(u_ref, wst_ref, kin_ref, wc_ref, lam_ref, y_ref, s_scr, hp_scr, *, nb):
    rows = u_ref.shape[0]
    nc = rows // nb
    ns = LANE
    u = u_ref[...]
    s_scr[...] = _dot(u, wst_ref[...])
    lam = lam_ref[...]
    lfr, lfi, lbr, lbi = lam[0:1], lam[1:2], lam[2:3], lam[3:4]

    def body(ci, carry):
        hfr, hfi, hbr, hbi = carry
        rf = pl.multiple_of(ci * nb, nb)
        rb = pl.multiple_of((nc - 1 - ci) * nb, nb)
        hp_scr[pl.ds(rf, nb), 0:ns] = hfr
        hp_scr[pl.ds(rf, nb), ns:2 * ns] = hfi
        hp_scr[pl.ds(rb, nb), 2 * ns:3 * ns] = hbr
        hp_scr[pl.ds(rb, nb), 3 * ns:4 * ns] = hbi
        sfr = s_scr[pl.ds(rf, nb), 0:ns]
        sfi = s_scr[pl.ds(rf, nb), ns:2 * ns]
        sbr = s_scr[pl.ds(rb, nb), 2 * ns:3 * ns]
        sbi = s_scr[pl.ds(rb, nb), 3 * ns:4 * ns]
        return (lfr * hfr - lfi * hfi + sfr, lfr * hfi + lfi * hfr + sfi,
                lbr * hbr - lbi * hbi + sbr, lbr * hbi + lbi * hbr + sbi)

    z = jnp.zeros((nb, ns), F32)
    lax.fori_loop(0, nc, body, (z, z, z, z))
    y_ref[...] = _dot(u, kin_ref[...]) + _dot(hp_scr[...].astype(BF16), wc_ref[...])


def s5_scan(u, wst, kin, wc, lam, nb):
    SG, rows, K = u.shape
    per = lambda a: pl.BlockSpec((None,) + a.shape[1:], lambda s: (s, 0, 0))
    return pl.pallas_call(
        functools.partial(_s5_kernel, nb=nb),
        grid=(SG,),
        in_specs=[per(u), per(wst), per(kin), per(wc), per(lam)],
        out_specs=pl.BlockSpec((None, rows, K), lambda s: (s, 0, 0)),
        out_shape=jax.ShapeDtypeStruct((SG, rows, K), F32),
        scratch_shapes=[pltpu.VMEM((rows, 4 * LANE), F32), pltpu.VMEM((rows, 4 * LANE), F32)],
        compiler_params=_cparams("parallel"),
        name="s5_scan",
    )(u, wst, kin, wc, lam)


def _s5_post_kernel(h_ref, y_ref, nw_ref, d_ref, wo_ref, wg_ref, o_ref):
    h = h_ref[...]
    y = y_ref[...] + d_ref[...] * _rms(h, nw_ref[...])
    act = jax.nn.gelu(y).astype(BF16)
    o_ref[...] = h + _dot(act, wo_ref[...]) * _sigmoid(_dot(act, wg_ref[...]))


def s5_post(h, y, nw, d_skip, glu_w, tm=512):
    m, d = h.shape
    row = pl.BlockSpec((tm, d), lambda i: (i, 0))
    vec = pl.BlockSpec((1, d), lambda i: (0, 0))
    return pl.pallas_call(
        _s5_post_kernel,
        grid=(m // tm,),
        in_specs=[row, row, vec, vec,
                  pl.BlockSpec((d, d), lambda i: (0, 0)),
                  pl.BlockSpec((d, d), lambda i: (0, 1))],
        out_specs=row,
        out_shape=jax.ShapeDtypeStruct((m, d), F32),
        compiler_params=_cparams("parallel"),
        name="s5_post",
    )(h, y, nw.reshape(1, d), d_skip.reshape(1, d), glu_w, glu_w)


def _router_kernel(h_ref, nw_ref, wr_ref, meta_ref):
    xn = _rms(h_ref[...], nw_ref[...])
    logits = jnp.dot(xn, wr_ref[...], preferred_element_type=F32, precision=lax.Precision.HIGHEST)
    lane = lax.broadcasted_iota(jnp.int32, logits.shape, 1)
    neg = jnp.float32(-jnp.inf)
    lg = jnp.where(lane < N_EXPERTS, logits, neg)
    t1 = jnp.max(lg, axis=1, keepdims=True)
    i1 = jnp.min(jnp.where(lg == t1, lane, LANE), axis=1, keepdims=True)
    lg2 = jnp.where(lane == i1, neg, lg)
    t2 = jnp.max(lg2, axis=1, keepdims=True)
    i2 = jnp.min(jnp.where(lg2 == t2, lane, LANE), axis=1, keepdims=True)
    ex = jnp.exp(t2 - t1)
    g1 = 1.0 / (1.0 + ex)
    g2 = ex / (1.0 + ex)
    meta = jnp.where(lane == 0, i1.astype(F32), 0.0)
    meta = jnp.where(lane == 1, i2.astype(F32), meta)
    meta = jnp.where(lane == 2, g1, meta)
    meta = jnp.where(lane == 3, g2, meta)
    meta_ref[...] = meta


def router(h, nw, w_router, tm=512):
    m, d = h.shape
    wr = jnp.zeros((d, LANE), F32).at[:, :N_EXPERTS].set(w_router)
    return pl.pallas_call(
        _router_kernel,
        grid=(m // tm,),
        in_specs=[pl.BlockSpec((tm, d), lambda i: (i, 0)),
                  pl.BlockSpec((1, d), lambda i: (0, 0)),
                  pl.BlockSpec((d, LANE), lambda i: (0, 0))],
        out_specs=pl.BlockSpec((tm, LANE), lambda i: (i, 0)),
        out_shape=jax.ShapeDtypeStruct((m, LANE), F32),
        compiler_params=_cparams("parallel"),
        name="router",
    )(h, nw.reshape(1, d), wr)


def _moe_kernel(arow_ref, blk_e_ref, nvalid_ref, h_hbm, nw_ref, wg_ref, wu_ref, wd_ref,
                y2_hbm, xg_scr, xn_scr, acc_scr, sem_in, sem_out, *, bm):
    i = pl.program_id(0)
    f = pl.program_id(1)
    base = i * bm
    nv = nvalid_ref[i]

    d = acc_scr.shape[1]

    def in_copy(r):
        a = arow_ref[base + r]
        return pltpu.make_async_copy(h_hbm.at[pl.ds(a >> 1, 1)], xg_scr.at[pl.ds(r, 1)], sem_in)

    def out_copy(r):
        a = arow_ref[base + r]
        col = pl.multiple_of((a & 1) * d, d)
        return pltpu.make_async_copy(acc_scr.at[pl.ds(r, 1)],
                                     y2_hbm.at[pl.ds(a >> 1, 1), pl.ds(col, d)], sem_out)

    @pl.when(f == 0)
    def _():
        xg_scr[...] = jnp.zeros_like(xg_scr)

        def start(r, c):
            in_copy(r).start()
            return c

        lax.fori_loop(0, nv, start, 0)

        def wait(r, c):
            in_copy(r).wait()
            return c

        lax.fori_loop(0, nv, wait, 0)
        xn_scr[...] = _rms(xg_scr[...], nw_ref[...]).astype(BF16)
        acc_scr[...] = jnp.zeros_like(acc_scr)

    @pl.when(nv > 0)
    def _():
        xn = xn_scr[...]
        act = _silu(_dot(xn, wg_ref[0])) * _dot(xn, wu_ref[0])
        acc_scr[...] += _dot(act.astype(BF16), wd_ref[0])

    @pl.when(f == pl.num_programs(1) - 1)
    def _():
        def start(r, c):
            out_copy(r).start()
            return c

        lax.fori_loop(0, nv, start, 0)

        def wait(r, c):
            out_copy(r).wait()
            return c

        lax.fori_loop(0, nv, wait, 0)


def moe_experts(h, nw, w_gu, w_down, arow, blk_e, nvalid, bm=MOE_BM, tf=512):
    t, d = h.shape
    dff = w_down.shape[1]
    nf = dff // tf
    n_blocks = blk_e.shape[0]
    grid_spec = pltpu.PrefetchScalarGridSpec(
        num_scalar_prefetch=3,
        grid=(n_blocks, nf),
        in_specs=[pl.BlockSpec(memory_space=pl.ANY),
                  pl.BlockSpec((1, d), lambda i, f, ar, be, nv: (0, 0)),
                  pl.BlockSpec((1, d, tf), lambda i, f, ar, be, nv: (be[i], 0, f)),
                  pl.BlockSpec((1, d, tf), lambda i, f, ar, be, nv: (be[i], 0, nf + f)),
                  pl.BlockSpec((1, tf, d), lambda i, f, ar, be, nv: (be[i], f, 0))],
        out_specs=pl.BlockSpec(memory_space=pl.ANY),
        scratch_shapes=[pltpu.VMEM((bm, d), F32), pltpu.VMEM((bm, d), BF16),
                        pltpu.VMEM((bm, d), F32),
                        pltpu.SemaphoreType.DMA(()), pltpu.SemaphoreType.DMA(())],
    )
    return pl.pallas_call(
        functools.partial(_moe_kernel, bm=bm),
        grid_spec=grid_spec,
        out_shape=jax.ShapeDtypeStruct((t, 2 * d), F32),
        compiler_params=_cparams("arbitrary", "arbitrary"),
        name="moe_experts",
    )(arow, blk_e, nvalid, h, nw.reshape(1, d), w_gu, w_gu, w_down)


def _moe_plan(meta, bm):
    t = meta.shape[0]
    n_assign = 2 * t
    flat_e = meta[:, 0:2].astype(jnp.int32).reshape(-1)
    onehot = (flat_e[:, None] == jnp.arange(N_EXPERTS, dtype=jnp.int32)[None, :]).astype(jnp.int32)
    incl = jnp.cumsum(onehot, axis=0)
    counts = incl[-1]
    rank = jnp.take_along_axis(incl, flat_e[:, None], axis=1)[:, 0] - 1
    padded = (counts + bm - 1) // bm * bm
    pend = jnp.cumsum(padded)
    pstart = pend - padded
    dest = pstart[flat_e] + rank
    n_blocks = -(-n_assign // bm) + N_EXPERTS
    n_rows = n_blocks * bm
    arow = jnp.zeros((n_rows,), jnp.int32).at[dest].set(jnp.arange(n_assign, dtype=jnp.int32))
    blk_start = jnp.arange(n_blocks, dtype=jnp.int32) * bm
    blk_e = jnp.minimum(jnp.searchsorted(pend, blk_start, side='right'), N_EXPERTS - 1).astype(jnp.int32)
    used_end = (pstart + counts)[blk_e]
    nvalid = jnp.clip(used_end - blk_start, 0, bm).astype(jnp.int32)
    nvalid = jnp.where(blk_start < pend[-1], nvalid, 0)
    return arow, blk_e, nvalid


def kernel(x, p, norm_mix, norm_ffn, norm_ple, final_norm, ple_gate, ple_proj, ev_w_in, ev_w_out,
           hg_lb_logits, hg_norm_w, m2_conv_w, m2_conv_b, m2_dt_bias, m2_a_log, m2_d, m2_norm_w,
           s5_a_re, s5_a_im, s5_log_step, s5_b_re, s5_b_im, s5_c_re, s5_c_im, s5_d, s5_glu_w,
           ffn_w_gu, ffn_w_down, moe_router, moe_w_gu, moe_w_down):
    B, S, D = x.shape
    T = B * S
    depth = norm_mix.shape[0]
    lb_soft = jax.nn.softmax(hg_lb_logits.astype(F32), axis=0)
    hg_lb = jnp.cumsum(lb_soft, axis=0) - lb_soft[0]
    h = x.reshape(T, D)
    for layer in range(depth):
        j = layer // 2
        if layer % 2 == 0:
            w_in = jnp.pad(ev_w_in[j], ((0, 0), (0, EVEN_IN_PAD - ev_w_in.shape[2]))).astype(BF16)
            proj = rms_matmul(h, norm_mix[layer], w_in).reshape(B, S, EVEN_IN_PAD)
            dt_rows = jnp.swapaxes(proj[:, :, OFF_DT:OFF_DT + 2 * M2_HEADS], 1, 2)
            o_a = hgrn2(proj, hg_lb[j], hg_norm_w[j])
            o_b = ssd(proj, dt_rows, m2_conv_w[j], m2_conv_b[j], m2_dt_bias[j], m2_a_log[j],
                      m2_d[j], m2_norm_w[j])
            w_out = ev_w_out[j].astype(BF16)
            h = mix_out(h, o_a.reshape(T, HG_WIDTH), o_b.reshape(T, M2_INNER),
                        w_out[:HG_WIDTH], w_out[HG_WIDTH:])
            h = ffn(h, norm_ffn[layer], ffn_w_gu[j].astype(BF16), ffn_w_down[j].astype(BF16))
            y2 = meta = None
        else:
            nc = S // S5_STEPS
            sg = D // S5_SUPER
            hn = rms_cast(h, norm_mix[layer])
            u = hn.reshape(B, nc, S5_STEPS, sg, S5_SUPER).transpose(3, 1, 0, 2, 4)
            u = u.reshape(sg, nc * B, S5_STEPS * S5_SUPER)
            wst, kin, wc, lam = _s5_tables(s5_a_re[j], s5_a_im[j], s5_log_step[j], s5_b_re[j],
                                           s5_b_im[j], s5_c_re[j], s5_c_im[j])
            y = s5_scan(u, wst, kin, wc, lam, B)
            y = y.reshape(sg, nc, B, S5_STEPS, S5_SUPER).transpose(2, 1, 3, 0, 4).reshape(T, D)
            h = s5_post(h, y, norm_mix[layer], s5_d[j], s5_glu_w[j].astype(BF16))
            meta = router(h, norm_ffn[layer], moe_router[j])
            arow, blk_e, nvalid = _moe_plan(meta, MOE_BM)
            y2 = moe_experts(h, norm_ffn[layer], moe_w_gu[j].astype(BF16), moe_w_down[j].astype(BF16),
                             arow, blk_e, nvalid)
        h = ple(h, p[layer].reshape(T, -1), norm_ple[layer], ple_gate[layer].astype(BF16),
                ple_proj[layer].astype(BF16), final_norm, y2=y2, meta=meta,
                final=(layer == depth - 1))
    return h.reshape(B, S, D)
```

```python
import functools

import numpy as np
import jax
import jax.numpy as jnp
from jax import lax
from jax.experimental import pallas as pl
from jax.experimental.pallas import tpu as pltpu

F32 = jnp.float32
BF16 = jnp.bfloat16

RMS_EPS = 1e-6
LB_FLOOR = 1e-30
LANE = 128
VMEM_LIMIT = 56 * 1024 * 1024

HG_HEADS = 4
HG_DIM = 128
HG_WIDTH = HG_HEADS * HG_DIM
HG_CHUNK = 64
M2_HEADS = 8
M2_P = 64
M2_INNER = M2_HEADS * M2_P
M2_GROUPS = 2
M2_N = 128
M2_CHUNK = 128
M2_XBC = M2_INNER + 2 * M2_GROUPS * M2_N
S5_GROUP_SIZE = 16
S5_STATE = 64
S5_STEPS = 16
S5_SUPER = 32
N_EXPERTS = 8
MOE_BM = 560

OFF_Q, OFF_FF, OFF_FB, OFF_I, OFF_G = (k * HG_WIDTH for k in range(5))
OFF_Z = 5 * HG_WIDTH
OFF_XBC = OFF_Z + M2_INNER
OFF_DT = OFF_XBC + M2_XBC
EVEN_IN_PAD = OFF_DT + LANE


def _cparams(*sem):
    return pltpu.CompilerParams(dimension_semantics=sem, vmem_limit_bytes=VMEM_LIMIT)


def _rms(x, w):
    return x * lax.rsqrt(jnp.mean(x * x, axis=-1, keepdims=True) + RMS_EPS) * w


def _sigmoid(x):
    return 1.0 / (1.0 + jnp.exp(-x))


def _silu(x):
    return x * _sigmoid(x)


def _dot(a, b):
    return jnp.dot(a, b, preferred_element_type=F32)


def _dot_nt(a, b):
    return lax.dot_general(a, b, (((1,), (1,)), ((), ())), preferred_element_type=F32)


def _split3(x):
    hi = x.astype(BF16)
    r1 = x - hi.astype(F32)
    mid = r1.astype(BF16)
    lo = (r1 - mid.astype(F32)).astype(BF16)
    return hi, mid, lo


def _rms_matmul_kernel(x_ref, nw_ref, w_ref, wdt_ref, o_ref, dtr_ref):
    xn = _rms(x_ref[...], nw_ref[...]).astype(BF16)
    o_ref[...] = _dot(xn, w_ref[...])
    dtr_ref[...] = _dot_nt(wdt_ref[...], xn)


def rms_matmul(x, nw, w, w_dt_t, tm=256):
    m, k = x.shape
    n = w.shape[1]
    nr = w_dt_t.shape[0]
    return pl.pallas_call(
        _rms_matmul_kernel,
        grid=(m // tm,),
        in_specs=[pl.BlockSpec((tm, k), lambda i: (i, 0)),
                  pl.BlockSpec((1, k), lambda i: (0, 0)),
                  pl.BlockSpec((k, n), lambda i: (0, 0)),
                  pl.BlockSpec((nr, k), lambda i: (0, 0))],
        out_specs=[pl.BlockSpec((tm, n), lambda i: (i, 0)),
                   pl.BlockSpec((nr, tm), lambda i: (0, i))],
        out_shape=[jax.ShapeDtypeStruct((m, n), F32), jax.ShapeDtypeStruct((nr, m), F32)],
        compiler_params=_cparams("parallel"),
        name="rms_matmul",
    )(x, nw.reshape(1, k), w, w_dt_t)


def _rms_cast_kernel(x_ref, nw_ref, o_ref):
    o_ref[...] = _rms(x_ref[...], nw_ref[...]).astype(o_ref.dtype)


def rms_cast(x, nw, tm=512):
    m, k = x.shape
    return pl.pallas_call(
        _rms_cast_kernel,
        grid=(m // tm,),
        in_specs=[pl.BlockSpec((tm, k), lambda i: (i, 0)),
                  pl.BlockSpec((1, k), lambda i: (0, 0))],
        out_specs=pl.BlockSpec((tm, k), lambda i: (i, 0)),
        out_shape=jax.ShapeDtypeStruct((m, k), BF16),
        compiler_params=_cparams("parallel"),
        name="rms_cast",
    )(x, nw.reshape(1, k))


def _hgrn2_masks(L):
    t = np.arange(L)
    masks = []
    w = L // 2
    while w >= 1:
        blk = t // w
        masks.append((blk % 2 == 1)[:, None] & (blk[None, :] == blk[:, None] - 1))
        w //= 2
    masks.append(np.eye(L, dtype=bool))
    m_f = np.stack(masks).astype(np.float32)
    return m_f, np.ascontiguousarray(m_f[:, ::-1, ::-1]), len(masks) - 1


def _pair_ref(b, w, reverse):
    L, d = b.shape
    off = w if reverse else w - 1
    if 2 * w > 8:
        pieces = [jnp.broadcast_to(b[p * 2 * w + off:p * 2 * w + off + 1, :], (2 * w, d))
                  for p in range(L // (2 * w))]
        return pieces[0] if len(pieces) == 1 else jnp.concatenate(pieces, axis=0)
    b3 = b.reshape(L // 8, 8, d)
    sub = lax.broadcasted_iota(jnp.int32, b3.shape, 1)
    ref = None
    for p in range(8 // (2 * w)):
        row = jnp.broadcast_to(b3[:, p * 2 * w + off:p * 2 * w + off + 1, :], b3.shape)
        ref = row if ref is None else jnp.where(sub >= p * 2 * w, row, ref)
    return ref.reshape(L, d)


def _hgrn2_kernel(qp_ref, ffp_ref, fbp_ref, ip_ref, gp_ref, lb_ref, nw_ref,
                  trif_ref, trib_ref, mf_ref, mb_ref, o_ref,
                  of_scr, ob_scr, stf_scr, stb_scr, *, L, nl):
    S = qp_ref.shape[0]
    nc = S // L
    lb = lb_ref[...]
    lb_floor = jnp.maximum(lb, LB_FLOOR)
    one_m_lb = 1.0 - lb

    def chunk(c, fpre_ref, tri_ref, m_ref, st_scr, reverse):
        r0 = pl.multiple_of(c * L, L)
        q = _silu(qp_ref[pl.ds(r0, L), :])
        v = ip_ref[pl.ds(r0, L), :]
        f = lb_floor + one_m_lb * _sigmoid(fpre_ref[pl.ds(r0, L), :])
        logf = jnp.log(f)
        k = 1.0 - f
        hi, mid, lo = _split3(logf)
        tri = tri_ref[...]
        b = _dot(tri, hi) + _dot(tri, mid) + _dot(tri, lo)
        a = m_ref[nl] * _dot_nt(q.astype(BF16), k.astype(BF16))
        w = L // 2
        for lvl in range(nl):
            dlt = b - _pair_ref(b, w, reverse)
            qt = (q * jnp.exp(jnp.minimum(dlt, 0.0))).astype(BF16)
            kt = (k * jnp.exp(jnp.minimum(-dlt, 0.0))).astype(BF16)
            a = a + m_ref[lvl] * _dot_nt(qt, kt)
            w //= 2
        tot_row = b[0:1] if reverse else b[L - 1:L]
        q_in = (q * jnp.exp(b)).astype(BF16)
        k_out = (k * jnp.exp(tot_row - b)).astype(BF16)
        st = st_scr[...]
        o = _dot(a.astype(BF16), v.astype(BF16)) + _dot_nt(q_in, st.astype(BF16))
        st_scr[...] = st * jnp.exp(tot_row) + _dot(v.T.astype(BF16), k_out)
        return r0, o

    stf_scr[...] = jnp.zeros_like(stf_scr)
    stb_scr[...] = jnp.zeros_like(stb_scr)

    def body(ci, carry):
        r0, o = chunk(ci, ffp_ref, trif_ref, mf_ref, stf_scr, False)
        of_scr[pl.ds(r0, L), :] = o
        r0, o = chunk(nc - 1 - ci, fbp_ref, trib_ref, mb_ref, stb_scr, True)
        ob_scr[pl.ds(r0, L), :] = o
        return carry

    lax.fori_loop(0, nc, body, 0)
    nw = nw_ref[...]
    R = 4 * L

    def finish(i, carry):
        r0 = pl.multiple_of(i * R, R)
        o = of_scr[pl.ds(r0, R), :] + ob_scr[pl.ds(r0, R), :]
        o = o * lax.rsqrt(jnp.mean(o * o, axis=-1, keepdims=True) + RMS_EPS)
        o = o * nw * _sigmoid(gp_ref[pl.ds(r0, R), :])
        o_ref[pl.ds(r0, R), :] = o.astype(o_ref.dtype)
        return carry

    lax.fori_loop(0, S // R, finish, 0)


def hgrn2(proj, lb, norm_w, L=HG_CHUNK):
    B, S, _ = proj.shape
    m_f, m_b, nl = _hgrn2_masks(L)
    t = np.arange(L)
    tri_f = jnp.asarray((t[None, :] <= t[:, None]).astype(np.float32), BF16)
    tri_b = jnp.asarray((t[None, :] >= t[:, None]).astype(np.float32), BF16)
    d = HG_DIM
    nb = HG_WIDTH // d

    def col(off):
        return pl.BlockSpec((None, S, d), lambda b, h, off=off: (b, 0, off // d + h))

    vec = pl.BlockSpec((1, d), lambda b, h: (0, h))
    const2 = lambda a: pl.BlockSpec(a.shape, lambda b, h: (0, 0))
    const3 = lambda a: pl.BlockSpec(a.shape, lambda b, h: (0, 0, 0))
    return pl.pallas_call(
        functools.partial(_hgrn2_kernel, L=L, nl=nl),
        grid=(B, nb),
        in_specs=[col(OFF_Q), col(OFF_FF), col(OFF_FB), col(OFF_I), col(OFF_G), vec, vec,
                  const2(tri_f), const2(tri_b), const3(m_f), const3(m_b)],
        out_specs=pl.BlockSpec((None, S, d), lambda b, h: (b, 0, h)),
        out_shape=jax.ShapeDtypeStruct((B, S, HG_WIDTH), BF16),
        scratch_shapes=[pltpu.VMEM((S, d), F32), pltpu.VMEM((S, d), F32),
                        pltpu.VMEM((d, d), F32), pltpu.VMEM((d, d), F32)],
        compiler_params=_cparams("parallel", "parallel"),
        name="hgrn2",
    )(proj, proj, proj, proj, proj, lb.reshape(1, HG_WIDTH), norm_w.reshape(1, HG_WIDTH),
      tri_f, tri_b, jnp.asarray(m_f), jnp.asarray(m_b))


def _shift_rows(x, k):
    if k == 0:
        return x
    n = x.shape[0]
    rolled = pltpu.roll(x, (-k) % n, 0)
    t = lax.broadcasted_iota(jnp.int32, x.shape, 0)
    ok = (t + k >= 0) & (t + k < n)
    return jnp.where(ok, rolled, 0.0)


def _conv_silu(x, w, b):
    half = w.shape[0] // 2
    acc = b
    for j in range(w.shape[0]):
        acc = acc + w[j:j + 1, :] * _shift_rows(x, j - half)
    return _silu(acc)


def _ssd_kernel(x_ref, b_ref, c_ref, z_ref, dtc_ref, dtr_ref,
                cwx_ref, cwb_ref, cwc_ref, cbx_ref, cbb_ref, cbc_ref,
                dtbc_ref, dtbr_ref, alr_ref, alc_ref, dsk_ref, nw_ref,
                tril_ref, triu_ref,
                o_ref, xs_scr, bs_scr, cs_scr, y_scr, st_scr, *, L, hg, P):
    g = pl.program_id(1)
    S = x_ref.shape[0]
    nc = S // L
    nh = 2 * hg
    xs_scr[...] = _conv_silu(x_ref[...], cwx_ref[...], cbx_ref[...])
    bs_scr[...] = _conv_silu(b_ref[...], cwb_ref[...], cbb_ref[...])
    cs_scr[...] = _conv_silu(c_ref[...], cwc_ref[...], cbc_ref[...])

    a_row = -jnp.exp(alr_ref[...])
    a_col = -jnp.exp(alc_ref[...])
    tril = tril_ref[...]
    triu = triu_ref[...]
    ti = lax.broadcasted_iota(jnp.int32, (L, L), 0)
    si = lax.broadcasted_iota(jnp.int32, (L, L), 1)
    lane2 = lax.broadcasted_iota(jnp.int32, (L, 2 * P), 1)

    def softplus(v):
        return jnp.maximum(v, 0.0) + jnp.log(1.0 + jnp.exp(-jnp.abs(v)))

    def per_head(cols):
        tiles = []
        for j in range(0, hg, 2):
            lo = jnp.broadcast_to(cols[j], (L, 2 * P))
            hi = jnp.broadcast_to(cols[j + 1], (L, 2 * P))
            tiles.append(jnp.where(lane2 < P, lo, hi))
        return jnp.concatenate(tiles, axis=1)

    def direction(d, first):
        st_scr[...] = jnp.zeros_like(st_scr)
        cum_c = tril if d == 0 else triu
        cum_r = triu if d == 0 else tril

        def body(ci, carry):
            c = ci if d == 0 else nc - 1 - ci
            r0 = pl.multiple_of(c * L, L)
            x = xs_scr[pl.ds(r0, L), :]
            bm = bs_scr[pl.ds(r0, L), :]
            cm = cs_scr[pl.ds(r0, L), :]
            dt_c = softplus(dtc_ref[pl.ds(r0, L), :] + dtbc_ref[...])
            dt_r = softplus(dtr_ref[:, pl.ds(r0, L)] + dtbr_ref[...])
            h1, h2, h3 = _split3(dt_c * a_row)
            acum_c = _dot(cum_c, h1) + _dot(cum_c, h2) + _dot(cum_c, h3)
            g1, g2, g3 = _split3(dt_r * a_col)
            acum_r = _dot(g1, cum_r) + _dot(g2, cum_r) + _dot(g3, cum_r)
            ones_rows = jnp.ones((8, L), BF16)
            tot = (_dot(ones_rows, h1) + _dot(ones_rows, h2) + _dot(ones_rows, h3))[0:1]
            cb = _dot_nt(cm.astype(BF16), bm.astype(BF16))
            keep = (si <= ti) if d == 0 else (si >= ti)
            ys, in_cols, out_cols, dec_cols = [], [], [], []
            for hh in range(hg):
                j = d * nh + g * hg + hh
                sel_c = (lax.broadcasted_iota(jnp.int32, (L, LANE), 1) == j)
                ac = jnp.sum(jnp.where(sel_c, acum_c, 0.0), axis=1, keepdims=True)
                dc = jnp.sum(jnp.where(sel_c, dt_c, 0.0), axis=1, keepdims=True)
                te = jnp.sum(jnp.where(sel_c[0:1], tot, 0.0), axis=1, keepdims=True)
                sel_r = (lax.broadcasted_iota(jnp.int32, (2 * nh, L), 0) == j)
                ar = jnp.sum(jnp.where(sel_r, acum_r, 0.0), axis=0, keepdims=True)
                dr = jnp.sum(jnp.where(sel_r, dt_r, 0.0), axis=0, keepdims=True)
                seg = ac - ar
                decay = jnp.where(keep, jnp.exp(jnp.where(keep, seg, 0.0)), 0.0)
                w = (cb * decay * dr).astype(BF16)
                ys.append(_dot(w, x[:, hh * P:(hh + 1) * P].astype(BF16)))
                in_cols.append(jnp.exp(te - ac) * dc)
                out_cols.append(jnp.exp(ac))
                dec_cols.append(jnp.broadcast_to(jnp.exp(te), (1, P)))
            y_diag = jnp.concatenate(ys, axis=1)
            st = st_scr[...]
            y_off = _dot(cm.astype(BF16), st.astype(BF16)) * per_head(out_cols)
            x_in = (x * per_head(in_cols)).astype(BF16)
            st_scr[...] = st * jnp.concatenate(dec_cols, axis=1) + _dot(bm.T.astype(BF16), x_in)
            y = y_diag + y_off
            if first:
                y_scr[pl.ds(r0, L), :] = y
            else:
                y = y + y_scr[pl.ds(r0, L), :] + dsk_ref[...] * x
                zz = z_ref[pl.ds(r0, L), :]
                y = y * _silu(zz)
                y = y * lax.rsqrt(jnp.mean(y * y, axis=-1, keepdims=True) + RMS_EPS)
                o_ref[pl.ds(r0, L), :] = (y * nw_ref[...]).astype(o_ref.dtype)
            return carry

        lax.fori_loop(0, nc, body, 0)

    direction(0, True)
    direction(1, False)


def ssd(proj, dt_rows, conv_w, conv_b, dt_bias, a_log, d_skip, norm_w, L=M2_CHUNK):
    B, S, _ = proj.shape
    hg = M2_HEADS // M2_GROUPS
    gw = hg * M2_P
    t = np.arange(L)
    tril = jnp.asarray((t[None, :] <= t[:, None]).astype(np.float32), BF16)
    triu = jnp.asarray((t[None, :] >= t[:, None]).astype(np.float32), BF16)
    nb_c = OFF_XBC + M2_INNER
    nc_c = nb_c + M2_GROUPS * M2_N
    xw = conv_w[:, :M2_INNER]
    bw = conv_w[:, M2_INNER:M2_INNER + M2_GROUPS * M2_N]
    cw = conv_w[:, M2_INNER + M2_GROUPS * M2_N:]
    cb2 = conv_b.reshape(1, -1)
    xb = cb2[:, :M2_INNER]
    bb = cb2[:, M2_INNER:M2_INNER + M2_GROUPS * M2_N]
    cbb = cb2[:, M2_INNER + M2_GROUPS * M2_N:]
    nh2 = 2 * M2_HEADS
    dtb_row = jnp.zeros((1, LANE), F32).at[0, :nh2].set(dt_bias.reshape(-1))
    al_row = jnp.zeros((1, LANE), F32).at[0, :nh2].set(a_log.reshape(-1))
    dtb_col = dt_bias.reshape(nh2, 1)
    al_col = a_log.reshape(nh2, 1)
    dsk = jnp.repeat(d_skip, M2_P).reshape(1, M2_INNER)
    nw = norm_w.reshape(1, M2_INNER)

    full2 = lambda a: pl.BlockSpec(a.shape, lambda b, g: (0, 0))
    return pl.pallas_call(
        functools.partial(_ssd_kernel, L=L, hg=hg, P=M2_P),
        grid=(B, M2_GROUPS),
        in_specs=[
            pl.BlockSpec((None, S, gw), lambda b, g: (b, 0, OFF_XBC // gw + g)),
            pl.BlockSpec((None, S, M2_N), lambda b, g: (b, 0, nb_c // M2_N + g)),
            pl.BlockSpec((None, S, M2_N), lambda b, g: (b, 0, nc_c // M2_N + g)),
            pl.BlockSpec((None, S, gw), lambda b, g: (b, 0, OFF_Z // gw + g)),
            pl.BlockSpec((None, S, LANE), lambda b, g: (b, 0, OFF_DT // LANE)),
            pl.BlockSpec((nh2, S), lambda b, g: (0, b)),
            pl.BlockSpec((xw.shape[0], gw), lambda b, g: (0, g)),
            pl.BlockSpec((bw.shape[0], M2_N), lambda b, g: (0, g)),
            pl.BlockSpec((cw.shape[0], M2_N), lambda b, g: (0, g)),
            pl.BlockSpec((1, gw), lambda b, g: (0, g)),
            pl.BlockSpec((1, M2_N), lambda b, g: (0, g)),
            pl.BlockSpec((1, M2_N), lambda b, g: (0, g)),
            full2(dtb_row), full2(dtb_col), full2(al_row), full2(al_col),
            pl.BlockSpec((1, gw), lambda b, g: (0, g)),
            pl.BlockSpec((1, gw), lambda b, g: (0, g)),
            full2(tril), full2(triu),
        ],
        out_specs=pl.BlockSpec((None, S, gw), lambda b, g: (b, 0, g)),
        out_shape=jax.ShapeDtypeStruct((B, S, M2_INNER), BF16),
        scratch_shapes=[pltpu.VMEM((S, gw), F32), pltpu.VMEM((S, M2_N), F32),
                        pltpu.VMEM((S, M2_N), F32), pltpu.VMEM((S, gw), F32),
                        pltpu.VMEM((M2_N, gw), F32)],
        compiler_params=_cparams("parallel", "parallel"),
        name="ssd",
    )(proj, proj, proj, proj, proj, dt_rows, xw, bw, cw, xb, bb, cbb,
      dtb_row, dtb_col, al_row, al_col, dsk, nw, tril, triu)


def _mix_out_kernel(h_ref, a_ref, b_ref, wa_ref, wb_ref, o_ref):
    o_ref[...] = h_ref[...] + _dot(a_ref[...], wa_ref[...]) + _dot(b_ref[...], wb_ref[...])


def mix_out(h, oa, ob, wa, wb, tm=512):
    m, n = h.shape
    return pl.pallas_call(
        _mix_out_kernel,
        grid=(m // tm,),
        in_specs=[pl.BlockSpec((tm, n), lambda i: (i, 0)),
                  pl.BlockSpec((tm, oa.shape[1]), lambda i: (i, 0)),
                  pl.BlockSpec((tm, ob.shape[1]), lambda i: (i, 0)),
                  pl.BlockSpec(wa.shape, lambda i: (0, 0)),
                  pl.BlockSpec(wb.shape, lambda i: (0, 0))],
        out_specs=pl.BlockSpec((tm, n), lambda i: (i, 0)),
        out_shape=jax.ShapeDtypeStruct((m, n), F32),
        compiler_params=_cparams("parallel"),
        name="mix_out",
    )(h, oa, ob, wa, wb)


def _ffn_kernel(h_ref, nw_ref, wg_ref, wu_ref, wd_ref, o_ref, xn_scr, acc_scr):
    f = pl.program_id(1)

    @pl.when(f == 0)
    def _():
        x = h_ref[...]
        xn_scr[...] = _rms(x, nw_ref[...]).astype(BF16)
        acc_scr[...] = x

    xn = xn_scr[...]
    act = _silu(_dot(xn, wg_ref[...])) * _dot(xn, wu_ref[...])
    acc_scr[...] += _dot(act.astype(BF16), wd_ref[...])

    @pl.when(f == pl.num_programs(1) - 1)
    def _():
        o_ref[...] = acc_scr[...]


def ffn(h, nw, w_gu, w_down, tm=1024, tf=512):
    m, d = h.shape
    dff = w_down.shape[0]
    nf = dff // tf
    return pl.pallas_call(
        _ffn_kernel,
        grid=(m // tm, nf),
        in_specs=[pl.BlockSpec((tm, d), lambda i, f: (i, 0)),
                  pl.BlockSpec((1, d), lambda i, f: (0, 0)),
                  pl.BlockSpec((d, tf), lambda i, f: (0, f)),
                  pl.BlockSpec((d, tf), lambda i, f: (0, nf + f)),
                  pl.BlockSpec((tf, d), lambda i, f: (f, 0))],
        out_specs=pl.BlockSpec((tm, d), lambda i, f: (i, 0)),
        out_shape=jax.ShapeDtypeStruct((m, d), F32),
        scratch_shapes=[pltpu.VMEM((tm, d), BF16), pltpu.VMEM((tm, d), F32)],
        compiler_params=_cparams("parallel", "arbitrary"),
        name="ffn",
    )(h, nw.reshape(1, d), w_gu, w_gu, w_down)


def _ple_kernel(*refs, moe, final):
    if moe:
        h_ref, y2_ref, meta_ref, p_ref, nw_ref, wg_ref, wp_ref, fw_ref, o_ref = refs
        meta = meta_ref[...]
        d = h_ref.shape[1]
        h = h_ref[...] + (meta[:, 2:3] * y2_ref[:, 0:d] + meta[:, 3:4] * y2_ref[:, d:2 * d])
    else:
        h_ref, p_ref, nw_ref, wg_ref, wp_ref, fw_ref, o_ref = refs
        h = h_ref[...]
    gate = _sigmoid(_dot(_rms(h, nw_ref[...]).astype(BF16), wg_ref[...]))
    h = h + gate * _dot(p_ref[...].astype(BF16), wp_ref[...])
    if final:
        h = _rms(h, fw_ref[...])
    o_ref[...] = h


def ple(h, p, nw, wg, wp, fw, y2=None, meta=None, final=False, tm=512):
    m, d = h.shape
    moe = y2 is not None
    row = lambda w: pl.BlockSpec((tm, w), lambda i: (i, 0))
    vec = pl.BlockSpec((1, d), lambda i: (0, 0))
    in_specs = [row(d)]
    args = [h]
    if moe:
        in_specs += [row(2 * d), row(meta.shape[1])]
        args += [y2, meta]
    in_specs += [row(p.shape[1]), vec, pl.BlockSpec(wg.shape, lambda i: (0, 0)),
                 pl.BlockSpec(wp.shape, lambda i: (0, 0)), vec]
    args += [p, nw.reshape(1, d), wg, wp, fw.reshape(1, d)]
    return pl.pallas_call(
        functools.partial(_ple_kernel, moe=moe, final=final),
        grid=(m // tm,),
        in_specs=in_specs,
        out_specs=row(d),
        out_shape=jax.ShapeDtypeStruct((m, d), F32),
        compiler_params=_cparams("parallel"),
        name="ple",
    )(*args)


def _s5_tables(a_re, a_im, log_step, b_re, b_im, c_re, c_im):
    G, N = a_re.shape[1:]
    C = S5_GROUP_SIZE
    T = S5_STEPS
    gp = S5_SUPER // C
    SG = G // gp
    tau = jnp.arange(T + 1, dtype=F32)

    def one_dir(d):
        delta = jnp.exp(log_step[d])[:, None]
        ar, ai = a_re[d], a_im[d]
        mag = jnp.exp(ar * delta)
        lam_re, lam_im = mag * jnp.cos(ai * delta), mag * jnp.sin(ai * delta)
        den = ar * ar + ai * ai
        num_re = lam_re - 1.0
        coef_re = (num_re * ar + lam_im * ai) / den
        coef_im = (lam_im * ar - num_re * ai) / den
        bb_re = coef_re[..., None] * b_re - coef_im[..., None] * b_im
        bb_im = coef_re[..., None] * b_im + coef_im[..., None] * b_re
        pm = jnp.exp((ar * delta)[None] * tau[:, None, None])
        ang = (ai * delta)[None] * tau[:, None, None]
        return (pm * jnp.cos(ang), pm * jnp.sin(ang)), (bb_re, bb_im), (c_re[d], c_im[d])

    def cmul(xr, xi, yr, yi):
        return xr * yr - xi * yi, xr * yi + xi * yr

    wst_cols, wc_rows, kin_parts, lam_rows = [], [], [], []
    for d in range(2):
        (pr, pi), (br, bi), (cr, ci) = one_dir(d)
        steps = jnp.arange(T)
        e_in = (T - 1 - steps) if d == 0 else steps
        sr, si = cmul(pr[e_in][..., None], pi[e_in][..., None], br[None], bi[None])
        e_out = (steps + 1) if d == 0 else (T - steps)
        cpr, cpi = cmul(cr[None], ci[None], pr[e_out][:, :, None, :], pi[e_out][:, :, None, :])
        lbr, lbi = cmul(pr[:T][..., None], pi[:T][..., None], br[None], bi[None])
        ktau = (jnp.einsum('gon,tgni->tgoi', cr, lbr) - jnp.einsum('gon,tgni->tgoi', ci, lbi))
        wst_cols.append((sr, si))
        wc_rows.append((cpr, -cpi))
        kin_parts.append(ktau)
        lam_rows += [pr[T], pi[T]]

    K = T * S5_SUPER
    eye_gp = jnp.eye(gp, dtype=F32)

    def to_in(m):
        m = m.reshape(T, SG, gp, N, C)
        out = jnp.einsum('tsgnc,gh->stgchn', m, eye_gp)
        return out.reshape(SG, K, gp * N)

    def to_out(m):
        m = m.reshape(T, SG, gp, C, N)
        out = jnp.einsum('tsgcn,gh->shntgc', m, eye_gp)
        return out.reshape(SG, gp * N, K)

    wst = jnp.concatenate([to_in(wst_cols[0][0]), to_in(wst_cols[0][1]),
                           to_in(wst_cols[1][0]), to_in(wst_cols[1][1])], axis=2)
    wc = jnp.concatenate([to_out(wc_rows[0][0]), to_out(wc_rows[0][1]),
                          to_out(wc_rows[1][0]), to_out(wc_rows[1][1])], axis=1)
    sp = jnp.arange(T)[:, None]
    tp = jnp.arange(T)[None, :]
    kf, kb = kin_parts
    lag_f = jnp.clip(tp - sp, 0, T - 1)
    lag_b = jnp.clip(sp - tp, 0, T - 1)
    toe = (jnp.where((tp >= sp)[..., None, None, None], kf[lag_f], 0.0)
           + jnp.where((sp >= tp)[..., None, None, None], kb[lag_b], 0.0))
    toe = toe.reshape(T, T, SG, gp, C, C)
    kin = jnp.einsum('stzgoi,gh->zsgitho', toe, eye_gp).reshape(SG, K, K)
    lam = jnp.stack([r.reshape(SG, gp * N) for r in lam_rows], axis=1)
    lam = jnp.concatenate([lam, jnp.zeros((SG, 4, gp * N), F32)], axis=1)
    return wst.astype(BF16), kin.astype(BF16), wc.astype(BF16), lam


def _s5_kernel(u_ref, wst_ref, kin_ref, wc_ref, lam_ref, y_ref, s_scr, hp_scr, *, nb):
    rows = u_ref.shape[0]
    nc = rows // nb
    ns = LANE
    u = u_ref[...]
    s_scr[...] = _dot(u, wst_ref[...])
    lam = lam_ref[...]
    lfr, lfi, lbr, lbi = lam[0:1], lam[1:2], lam[2:3], lam[3:4]

    def body(ci, carry):
        hfr, hfi, hbr, hbi = carry
        rf = pl.multiple_of(ci * nb, nb)
        rb = pl.multiple_of((nc - 1 - ci) * nb, nb)
        hp_scr[pl.ds(rf, nb), 0:ns] = hfr
        hp_scr[pl.ds(rf, nb), ns:2 * ns] = hfi
        hp_scr[pl.ds(rb, nb), 2 * ns:3 * ns] = hbr
        hp_scr[pl.ds(rb, nb), 3 * ns:4 * ns] = hbi
        sfr = s_scr[pl.ds(rf, nb), 0:ns]
        sfi = s_scr[pl.ds(rf, nb), ns:2 * ns]
        sbr = s_scr[pl.ds(rb, nb), 2 * ns:3 * ns]
        sbi = s_scr[pl.ds(rb, nb), 3 * ns:4 * ns]
        return (lfr * hfr - lfi * hfi + sfr, lfr * hfi + lfi * hfr + sfi,
                lbr * hbr - lbi * hbi + sbr, lbr * hbi + lbi * hbr + sbi)

    z = jnp.zeros((nb, ns), F32)
    lax.fori_loop(0, nc, body, (z, z, z, z))
    y_ref[...] = _dot(u, kin_ref[...]) + _dot(hp_scr[...].astype(BF16), wc_ref[...])


def s5_scan(u, wst, kin, wc, lam, nb):
    SG, rows, K = u.shape
    per = lambda a: pl.BlockSpec((None,) + a.shape[1:], lambda s: (s, 0, 0))
    return pl.pallas_call(
        functools.partial(_s5_kernel, nb=nb),
        grid=(SG,),
        in_specs=[per(u), per(wst), per(kin), per(wc), per(lam)],
        out_specs=pl.BlockSpec((None, rows, K), lambda s: (s, 0, 0)),
        out_shape=jax.ShapeDtypeStruct((SG, rows, K), F32),
        scratch_shapes=[pltpu.VMEM((rows, 4 * LANE), F32), pltpu.VMEM((rows, 4 * LANE), F32)],
        compiler_params=_cparams("parallel"),
        name="s5_scan",
    )(u, wst, kin, wc, lam)


def _s5_post_kernel(h_ref, y_ref, nw_ref, d_ref, wo_ref, wg_ref, o_ref):
    h = h_ref[...]
    y = y_ref[...] + d_ref[...] * _rms(h, nw_ref[...])
    act = jax.nn.gelu(y).astype(BF16)
    o_ref[...] = h + _dot(act, wo_ref[...]) * _sigmoid(_dot(act, wg_ref[...]))


def s5_post(h, y, nw, d_skip, glu_w, tm=512):
    m, d = h.shape
    row = pl.BlockSpec((tm, d), lambda i: (i, 0))
    vec = pl.BlockSpec((1, d), lambda i: (0, 0))
    return pl.pallas_call(
        _s5_post_kernel,
        grid=(m // tm,),
        in_specs=[row, row, vec, vec,
                  pl.BlockSpec((d, d), lambda i: (0, 0)),
                  pl.BlockSpec((d, d), lambda i: (0, 1))],
        out_specs=row,
        out_shape=jax.ShapeDtypeStruct((m, d), F32),
        compiler_params=_cparams("parallel"),
        name="s5_post",
    )(h, y, nw.reshape(1, d), d_skip.reshape(1, d), glu_w, glu_w)


def _router_kernel(h_ref, nw_ref, wr_ref, tri_ref, meta_ref, cnt_ref, carry_scr):
    @pl.when(pl.program_id(0) == 0)
    def _():
        carry_scr[...] = jnp.zeros_like(carry_scr)

    xn = _rms(h_ref[...], nw_ref[...])
    logits = jnp.dot(xn, wr_ref[...], preferred_element_type=F32, precision=lax.Precision.HIGHEST)
    lane = lax.broadcasted_iota(jnp.int32, logits.shape, 1)
    neg = jnp.float32(-jnp.inf)
    lg = jnp.where(lane < N_EXPERTS, logits, neg)
    t1 = jnp.max(lg, axis=1, keepdims=True)
    i1 = jnp.min(jnp.where(lg == t1, lane, LANE), axis=1, keepdims=True)
    lg2 = jnp.where(lane == i1, neg, lg)
    t2 = jnp.max(lg2, axis=1, keepdims=True)
    i2 = jnp.min(jnp.where(lg2 == t2, lane, LANE), axis=1, keepdims=True)
    ex = jnp.exp(t2 - t1)
    g1 = 1.0 / (1.0 + ex)
    g2 = ex / (1.0 + ex)
    oh1 = jnp.where(lane == i1, 1.0, 0.0)
    oh2 = jnp.where(lane == i2, 1.0, 0.0)
    tri = tri_ref[...]
    before1 = _dot(tri, oh1.astype(BF16))
    before2 = _dot(tri, oh2.astype(BF16))
    tot1 = jnp.sum(oh1, axis=0, keepdims=True)
    tot2 = jnp.sum(oh2, axis=0, keepdims=True)
    carry = carry_scr[0:1]
    rank1 = jnp.sum(oh1 * (carry + before1), axis=1, keepdims=True)
    rank2 = jnp.sum(oh2 * (carry + tot1 + before2), axis=1, keepdims=True)
    counts = jnp.broadcast_to(carry + tot1 + tot2, carry_scr.shape)
    carry_scr[...] = counts
    cnt_ref[...] = counts
    meta = jnp.where(lane == 0, i1.astype(F32), 0.0)
    meta = jnp.where(lane == 1, i2.astype(F32), meta)
    meta = jnp.where(lane == 2, g1, meta)
    meta = jnp.where(lane == 3, g2, meta)
    meta = jnp.where(lane == 4, rank1, meta)
    meta = jnp.where(lane == 5, rank2, meta)
    meta_ref[...] = meta


def router(h, nw, w_router, tm=512):
    m, d = h.shape
    wr = jnp.zeros((d, LANE), F32).at[:, :N_EXPERTS].set(w_router)
    t = np.arange(tm)
    tri = jnp.asarray((t[None, :] < t[:, None]).astype(np.float32), BF16)
    return pl.pallas_call(
        _router_kernel,
        grid=(m // tm,),
        in_specs=[pl.BlockSpec((tm, d), lambda i: (i, 0)),
                  pl.BlockSpec((1, d), lambda i: (0, 0)),
                  pl.BlockSpec((d, LANE), lambda i: (0, 0)),
                  pl.BlockSpec((tm, tm), lambda i: (0, 0))],
        out_specs=[pl.BlockSpec((tm, LANE), lambda i: (i, 0)),
                   pl.BlockSpec((8, LANE), lambda i: (0, 0))],
        out_shape=[jax.ShapeDtypeStruct((m, LANE), F32), jax.ShapeDtypeStruct((8, LANE), F32)],
        scratch_shapes=[pltpu.VMEM((8, LANE), F32)],
        compiler_params=_cparams("arbitrary"),
        name="router",
    )(h, nw.reshape(1, d), wr, tri)


def _moe_kernel(arow_ref, blk_e_ref, nvalid_ref, h_hbm, nw_ref, wg_ref, wu_ref, wd_ref,
                y2_hbm, xg_scr, xn_scr, acc_scr, sem_in, sem_out, *, bm, n_tok):
    g = pl.program_id(0)
    f = pl.program_id(1)
    nf = pl.num_programs(1)
    slot = g % 2
    other = 1 - slot
    d = acc_scr.shape[2]
    rows_per_step = bm // nf
    prv, cur, nxt = g * bm, (g + 1) * bm, (g + 2) * bm
    nv = nvalid_ref[g]

    def in_copy(off, r, s):
        tok = jnp.minimum(arow_ref[off + r] >> 1, n_tok - 1)
        return pltpu.make_async_copy(h_hbm.at[pl.ds(tok, 1)], xg_scr.at[s, pl.ds(r, 1)], sem_in.at[s])

    def out_copy(off, r, s):
        a = arow_ref[off + r]
        col = pl.multiple_of((a & 1) * d, d)
        return pltpu.make_async_copy(acc_scr.at[s, pl.ds(r, 1)],
                                     y2_hbm.at[pl.ds(a >> 1, 1), pl.ds(col, d)], sem_out.at[s])

    def wait_in(s):
        pltpu.make_async_copy(h_hbm.at[pl.ds(0, bm)], xg_scr.at[s], sem_in.at[s]).wait()

    def wait_out(s):
        pltpu.make_async_copy(acc_scr.at[s], y2_hbm.at[pl.ds(0, bm), pl.ds(0, d)], sem_out.at[s]).wait()

    @pl.when((g == 0) & (f == 0))
    def _():
        acc_scr[...] = jnp.zeros_like(acc_scr)

        def start(r, c):
            in_copy(cur, r, slot).start()
            return c

        lax.fori_loop(0, bm, start, 0)

    @pl.when(f == 0)
    def _():
        wait_in(slot)
        xn_scr[...] = _rms(xg_scr[slot], nw_ref[...]).astype(BF16)
        acc_scr[slot] = jnp.zeros((bm, d), F32)

    def move_rows(part, parts):
        base = pl.multiple_of(f * rows_per_step, 8)
        per = rows_per_step // parts
        for j in range(part * per, (part + 1) * per):
            in_copy(nxt, base + j, other).start()
            out_copy(prv, base + j, other).start()

    @pl.when(nv > 0)
    def _():
        parts = 2
        tc = wg_ref.shape[2] // parts
        xn = xn_scr[...]
        upd = None
        for c in range(parts):
            cs = slice(c * tc, (c + 1) * tc)
            act = _silu(_dot(xn, wg_ref[0, :, cs])) * _dot(xn, wu_ref[0, :, cs])
            part = _dot(act.astype(BF16), wd_ref[0, cs, :])
            upd = part if upd is None else upd + part
            move_rows(c, parts)
        acc_scr[slot] += upd

    @pl.when(nv == 0)
    def _():
        move_rows(0, 1)

    @pl.when(f == nf - 1)
    def _():
        wait_out(other)

    @pl.when((f == nf - 1) & (g == pl.num_programs(0) - 1))
    def _():
        wait_in(other)


def moe_experts(h, nw, w_gu, w_down, arow, blk_e, nvalid, bm=MOE_BM, tf=512):
    t, d = h.shape
    dff = w_down.shape[1]
    nf = dff // tf
    assert bm % nf == 0 and bm % 16 == 0
    n_steps = blk_e.shape[0]

    def wmap(col0):
        def index(g, f, ar, be, nv):
            return (be[g], 0, col0 + jnp.where(nv[g] > 0, f, nf - 1))
        return index

    def dmap(g, f, ar, be, nv):
        return (be[g], jnp.where(nv[g] > 0, f, nf - 1), 0)

    grid_spec = pltpu.PrefetchScalarGridSpec(
        num_scalar_prefetch=3,
        grid=(n_steps, nf),
        in_specs=[pl.BlockSpec(memory_space=pl.ANY),
                  pl.BlockSpec((1, d), lambda g, f, ar, be, nv: (0, 0)),
                  pl.BlockSpec((1, d, tf), wmap(0)),
                  pl.BlockSpec((1, d, tf), wmap(nf)),
                  pl.BlockSpec((1, tf, d), dmap)],
        out_specs=pl.BlockSpec(memory_space=pl.ANY),
        scratch_shapes=[pltpu.VMEM((2, bm, d), F32), pltpu.VMEM((bm, d), BF16),
                        pltpu.VMEM((2, bm, d), F32),
                        pltpu.SemaphoreType.DMA((2,)), pltpu.SemaphoreType.DMA((2,))],
    )
    return pl.pallas_call(
        functools.partial(_moe_kernel, bm=bm, n_tok=t),
        grid_spec=grid_spec,
        out_shape=jax.ShapeDtypeStruct((t + bm // 2, 2 * d), F32),
        compiler_params=_cparams("arbitrary", "arbitrary"),
        name="moe_experts",
    )(arow, blk_e, nvalid, h, nw.reshape(1, d), w_gu, w_gu, w_down)


def _moe_plan(meta, counts, bm):
    t = meta.shape[0]
    n_assign = 2 * t
    experts = jnp.arange(N_EXPERTS, dtype=jnp.int32)
    cnt = counts[0, :N_EXPERTS].astype(jnp.int32)
    padded = (cnt + bm - 1) // bm * bm
    pend = jnp.cumsum(padded)
    pstart = pend - padded
    e = meta[:, 0:2].astype(jnp.int32)
    rank = meta[:, 4:6].astype(jnp.int32)
    dest = jnp.sum(jnp.where(e[..., None] == experts, pstart, 0), axis=-1) + rank
    n_blocks = -(-n_assign // bm) + N_EXPERTS
    spare = 2 * t + jnp.arange(bm, dtype=jnp.int32)
    arow = jnp.tile(spare, n_blocks).at[dest.reshape(-1)].set(jnp.arange(n_assign, dtype=jnp.int32))
    arow = jnp.concatenate([spare, arow, spare, spare])
    blk_start = jnp.arange(n_blocks, dtype=jnp.int32) * bm
    blk_e = jnp.minimum(jnp.sum((pend[None, :] <= blk_start[:, None]).astype(jnp.int32), axis=1),
                        N_EXPERTS - 1)
    nvalid = jnp.clip((pstart + cnt)[blk_e] - blk_start, 0, bm)
    nvalid = jnp.where(blk_start < pend[-1], nvalid, 0).astype(jnp.int32)
    blk_e = jnp.concatenate([blk_e, blk_e[-1:]]).astype(jnp.int32)
    nvalid = jnp.concatenate([nvalid, jnp.zeros((1,), jnp.int32)])
    return arow, blk_e, nvalid


def kernel(x, p, norm_mix, norm_ffn, norm_ple, final_norm, ple_gate, ple_proj, ev_w_in, ev_w_out,
           hg_lb_logits, hg_norm_w, m2_conv_w, m2_conv_b, m2_dt_bias, m2_a_log, m2_d, m2_norm_w,
           s5_a_re, s5_a_im, s5_log_step, s5_b_re, s5_b_im, s5_c_re, s5_c_im, s5_d, s5_glu_w,
           ffn_w_gu, ffn_w_down, moe_router, moe_w_gu, moe_w_down):
    B, S, D = x.shape
    T = B * S
    depth = norm_mix.shape[0]
    lb_soft = jax.nn.softmax(hg_lb_logits.astype(F32), axis=0)
    hg_lb = jnp.cumsum(lb_soft, axis=0) - lb_soft[0]
    h = x.reshape(T, D)
    for layer in range(depth):
        j = layer // 2
        if layer % 2 == 0:
            w_in = jnp.pad(ev_w_in[j], ((0, 0), (0, EVEN_IN_PAD - ev_w_in.shape[2]))).astype(BF16)
            w_dt_t = ev_w_in[j][:, OFF_DT:].T.astype(BF16)
            proj, dt_rows = rms_matmul(h, norm_mix[layer], w_in, w_dt_t)
            proj = proj.reshape(B, S, EVEN_IN_PAD)
            o_a = hgrn2(proj, hg_lb[j], hg_norm_w[j])
            o_b = ssd(proj, dt_rows, m2_conv_w[j], m2_conv_b[j], m2_dt_bias[j], m2_a_log[j],
                      m2_d[j], m2_norm_w[j])
            w_out = ev_w_out[j].astype(BF16)
            h = mix_out(h, o_a.reshape(T, HG_WIDTH), o_b.reshape(T, M2_INNER),
                        w_out[:HG_WIDTH], w_out[HG_WIDTH:])
            h = ffn(h, norm_ffn[layer], ffn_w_gu[j].astype(BF16), ffn_w_down[j].astype(BF16))
            y2 = meta = None
        else:
            nc = S // S5_STEPS
            sg = D // S5_SUPER
            hn = rms_cast(h, norm_mix[layer])
            u = hn.reshape(B, nc, S5_STEPS, sg, S5_SUPER).transpose(3, 1, 0, 2, 4)
            u = u.reshape(sg, nc * B, S5_STEPS * S5_SUPER)
            wst, kin, wc, lam = _s5_tables(s5_a_re[j], s5_a_im[j], s5_log_step[j], s5_b_re[j],
                                           s5_b_im[j], s5_c_re[j], s5_c_im[j])
            y = s5_scan(u, wst, kin, wc, lam, B)
            y = y.reshape(sg, nc, B, S5_STEPS, S5_SUPER).transpose(2, 1, 3, 0, 4).reshape(T, D)
            h = s5_post(h, y, norm_mix[layer], s5_d[j], s5_glu_w[j].astype(BF16))
            meta, counts = router(h, norm_ffn[layer], moe_router[j])
            arow, blk_e, nvalid = _moe_plan(meta, counts, MOE_BM)
            y2 = moe_experts(h, norm_ffn[layer], moe_w_gu[j].astype(BF16), moe_w_down[j].astype(BF16),
                             arow, blk_e, nvalid)
        h = ple(h, p[layer].reshape(T, -1), norm_ple[layer], ple_gate[layer].astype(BF16),
                ple_proj[layer].astype(BF16), final_norm, y2=y2, meta=meta,
                final=(layer == depth - 1))
    return h.reshape(B, S, D)
```

```python
import functools

import numpy as np
import jax
import jax.numpy as jnp
from jax import lax
from jax.experimental import pallas as pl
from jax.experimental.pallas import tpu as pltpu

F32 = jnp.float32
BF16 = jnp.bfloat16

RMS_EPS = 1e-6
LB_FLOOR = 1e-30
LANE = 128
VMEM_LIMIT = 56 * 1024 * 1024

HG_HEADS = 4
HG_DIM = 128
HG_WIDTH = HG_HEADS * HG_DIM
HG_CHUNK = 64
M2_HEADS = 8
M2_P = 64
M2_INNER = M2_HEADS * M2_P
M2_GROUPS = 2
M2_N = 128
M2_CHUNK = 128
M2_XBC = M2_INNER + 2 * M2_GROUPS * M2_N
S5_GROUP_SIZE = 16
S5_STATE = 64
S5_STEPS = 8
N_EXPERTS = 8
MOE_BM = 560

OFF_Q, OFF_FF, OFF_FB, OFF_I, OFF_G = (k * HG_WIDTH for k in range(5))
OFF_Z = 5 * HG_WIDTH
OFF_XBC = OFF_Z + M2_INNER
OFF_DT = OFF_XBC + M2_XBC
EVEN_IN_PAD = OFF_DT + LANE


def _cparams(*sem):
    return pltpu.CompilerParams(dimension_semantics=sem, vmem_limit_bytes=VMEM_LIMIT)


def _rms(x, w):
    return x * lax.rsqrt(jnp.mean(x * x, axis=-1, keepdims=True) + RMS_EPS) * w


def _sigmoid(x):
    return 1.0 / (1.0 + jnp.exp(-x))


def _silu(x):
    return x * _sigmoid(x)


def _dot(a, b):
    return jnp.dot(a, b, preferred_element_type=F32)


def _dot_nt(a, b):
    return lax.dot_general(a, b, (((1,), (1,)), ((), ())), preferred_element_type=F32)


def _split3(x):
    hi = x.astype(BF16)
    r1 = x - hi.astype(F32)
    mid = r1.astype(BF16)
    lo = (r1 - mid.astype(F32)).astype(BF16)
    return hi, mid, lo


def _rms_matmul_kernel(x_ref, nw_ref, w_ref, wdt_ref, o_ref, dtr_ref):
    xn = _rms(x_ref[...], nw_ref[...]).astype(BF16)
    o_ref[...] = _dot(xn, w_ref[...])
    dtr_ref[...] = _dot_nt(wdt_ref[...], xn)


def rms_matmul(x, nw, w, w_dt_t, layer, tm=256):
    m, k = x.shape
    n = w.shape[2]
    nr = w_dt_t.shape[1]
    return pl.pallas_call(
        _rms_matmul_kernel,
        grid=(m // tm,),
        in_specs=[pl.BlockSpec((tm, k), lambda i: (i, 0)),
                  pl.BlockSpec((1, k), lambda i: (0, 0)),
                  pl.BlockSpec((None, k, n), lambda i: (layer, 0, 0)),
                  pl.BlockSpec((None, nr, k), lambda i: (layer, 0, 0))],
        out_specs=[pl.BlockSpec((tm, n), lambda i: (i, 0)),
                   pl.BlockSpec((nr, tm), lambda i: (0, i))],
        out_shape=[jax.ShapeDtypeStruct((m, n), F32), jax.ShapeDtypeStruct((nr, m), F32)],
        compiler_params=_cparams("parallel"),
        name="rms_matmul",
    )(x, nw.reshape(1, k), w, w_dt_t)


def _hgrn2_masks(L):
    t = np.arange(L)
    masks = []
    w = L // 2
    while w >= 1:
        blk = t // w
        masks.append((blk % 2 == 1)[:, None] & (blk[None, :] == blk[:, None] - 1))
        w //= 2
    masks.append(np.eye(L, dtype=bool))
    m_f = np.stack(masks).astype(np.float32)
    return m_f, np.ascontiguousarray(m_f[:, ::-1, ::-1]), len(masks) - 1


def _pair_ref(b, w, reverse):
    L, d = b.shape
    off = w if reverse else w - 1
    if 2 * w > 8:
        pieces = [jnp.broadcast_to(b[p * 2 * w + off:p * 2 * w + off + 1, :], (2 * w, d))
                  for p in range(L // (2 * w))]
        return pieces[0] if len(pieces) == 1 else jnp.concatenate(pieces, axis=0)
    b3 = b.reshape(L // 8, 8, d)
    sub = lax.broadcasted_iota(jnp.int32, b3.shape, 1)
    ref = None
    for p in range(8 // (2 * w)):
        row = jnp.broadcast_to(b3[:, p * 2 * w + off:p * 2 * w + off + 1, :], b3.shape)
        ref = row if ref is None else jnp.where(sub >= p * 2 * w, row, ref)
    return ref.reshape(L, d)


def _hgrn2_kernel(qp_ref, ffp_ref, fbp_ref, ip_ref, gp_ref, lb_ref, nw_ref,
                  trif_ref, trib_ref, mf_ref, mb_ref, o_ref,
                  of_scr, ob_scr, stf_scr, stb_scr, *, L, nl):
    S = qp_ref.shape[0]
    nc = S // L
    lb = lb_ref[...]
    lb_floor = jnp.maximum(lb, LB_FLOOR)
    one_m_lb = 1.0 - lb

    def chunk(c, fpre_ref, tri_ref, m_ref, st_scr, reverse):
        r0 = pl.multiple_of(c * L, L)
        q = _silu(qp_ref[pl.ds(r0, L), :])
        v = ip_ref[pl.ds(r0, L), :]
        f = lb_floor + one_m_lb * _sigmoid(fpre_ref[pl.ds(r0, L), :])
        logf = jnp.log(f)
        k = 1.0 - f
        hi, mid, lo = _split3(logf)
        tri = tri_ref[...]
        b = _dot(tri, hi) + _dot(tri, mid) + _dot(tri, lo)
        a = m_ref[nl] * _dot_nt(q.astype(BF16), k.astype(BF16))
        w = L // 2
        for lvl in range(nl):
            e = jnp.exp(-jnp.abs(b - _pair_ref(b, w, reverse)))
            a = a + m_ref[lvl] * _dot_nt((q * e).astype(BF16), (k * e).astype(BF16))
            w //= 2
        tot_row = b[0:1] if reverse else b[L - 1:L]
        q_in = (q * jnp.exp(b)).astype(BF16)
        k_out = (k * jnp.exp(tot_row - b)).astype(BF16)
        st = st_scr[...]
        o = _dot(a.astype(BF16), v.astype(BF16)) + _dot_nt(q_in, st.astype(BF16))
        st_scr[...] = st * jnp.exp(tot_row) + _dot(v.T.astype(BF16), k_out)
        return r0, o

    stf_scr[...] = jnp.zeros_like(stf_scr)
    stb_scr[...] = jnp.zeros_like(stb_scr)

    def body(ci, carry):
        r0, o = chunk(ci, ffp_ref, trif_ref, mf_ref, stf_scr, False)
        of_scr[pl.ds(r0, L), :] = o
        r0, o = chunk(nc - 1 - ci, fbp_ref, trib_ref, mb_ref, stb_scr, True)
        ob_scr[pl.ds(r0, L), :] = o
        return carry

    lax.fori_loop(0, nc, body, 0, unroll=4)
    nw = nw_ref[...]
    R = 4 * L

    def finish(i, carry):
        r0 = pl.multiple_of(i * R, R)
        o = of_scr[pl.ds(r0, R), :] + ob_scr[pl.ds(r0, R), :]
        o = o * lax.rsqrt(jnp.mean(o * o, axis=-1, keepdims=True) + RMS_EPS)
        o = o * nw * _sigmoid(gp_ref[pl.ds(r0, R), :])
        o_ref[pl.ds(r0, R), :] = o.astype(o_ref.dtype)
        return carry

    lax.fori_loop(0, S // R, finish, 0)


def hgrn2(proj, lb, norm_w, L=HG_CHUNK):
    B, S, _ = proj.shape
    m_f, m_b, nl = _hgrn2_masks(L)
    t = np.arange(L)
    tri_f = jnp.asarray((t[None, :] <= t[:, None]).astype(np.float32), BF16)
    tri_b = jnp.asarray((t[None, :] >= t[:, None]).astype(np.float32), BF16)
    d = HG_DIM
    nb = HG_WIDTH // d

    def col(off):
        return pl.BlockSpec((None, S, d), lambda b, h, off=off: (b, 0, off // d + h))

    vec = pl.BlockSpec((1, d), lambda b, h: (0, h))
    const2 = lambda a: pl.BlockSpec(a.shape, lambda b, h: (0, 0))
    const3 = lambda a: pl.BlockSpec(a.shape, lambda b, h: (0, 0, 0))
    return pl.pallas_call(
        functools.partial(_hgrn2_kernel, L=L, nl=nl),
        grid=(B, nb),
        in_specs=[col(OFF_Q), col(OFF_FF), col(OFF_FB), col(OFF_I), col(OFF_G), vec, vec,
                  const2(tri_f), const2(tri_b), const3(m_f), const3(m_b)],
        out_specs=pl.BlockSpec((None, S, d), lambda b, h: (b, 0, h)),
        out_shape=jax.ShapeDtypeStruct((B, S, HG_WIDTH), BF16),
        scratch_shapes=[pltpu.VMEM((S, d), F32), pltpu.VMEM((S, d), F32),
                        pltpu.VMEM((d, d), F32), pltpu.VMEM((d, d), F32)],
        compiler_params=_cparams("parallel", "parallel"),
        name="hgrn2",
    )(proj, proj, proj, proj, proj, lb.reshape(1, HG_WIDTH), norm_w.reshape(1, HG_WIDTH),
      tri_f, tri_b, jnp.asarray(m_f), jnp.asarray(m_b))


def _shift_rows(x, k):
    if k == 0:
        return x
    n = x.shape[0]
    rolled = pltpu.roll(x, (-k) % n, 0)
    t = lax.broadcasted_iota(jnp.int32, x.shape, 0)
    ok = (t + k >= 0) & (t + k < n)
    return jnp.where(ok, rolled, 0.0)


def _conv_silu(x, w, b):
    half = w.shape[0] // 2
    acc = b
    for j in range(w.shape[0]):
        acc = acc + w[j:j + 1, :] * _shift_rows(x, j - half)
    return _silu(acc)


def _ssd_kernel(x_ref, b_ref, c_ref, z_ref, dtc_ref, dtr_ref,
                cwx_ref, cwb_ref, cwc_ref, cbx_ref, cbb_ref, cbc_ref,
                dtbc_ref, dtbr_ref, alr_ref, alc_ref, dsk_ref, nw_ref,
                tril_ref, triu_ref,
                o_ref, xs_scr, bs_scr, cs_scr, y_scr, st_scr, *, L, hg, P):
    g = pl.program_id(1)
    S = x_ref.shape[0]
    nc = S // L
    nh = 2 * hg
    xs_scr[...] = _conv_silu(x_ref[...], cwx_ref[...], cbx_ref[...])
    bs_scr[...] = _conv_silu(b_ref[...], cwb_ref[...], cbb_ref[...])
    cs_scr[...] = _conv_silu(c_ref[...], cwc_ref[...], cbc_ref[...])

    a_row = -jnp.exp(alr_ref[...])
    a_col = -jnp.exp(alc_ref[...])
    tril = tril_ref[...]
    triu = triu_ref[...]
    ti = lax.broadcasted_iota(jnp.int32, (L, L), 0)
    si = lax.broadcasted_iota(jnp.int32, (L, L), 1)
    lane2 = lax.broadcasted_iota(jnp.int32, (L, 2 * P), 1)

    def softplus(v):
        return jnp.maximum(v, 0.0) + jnp.log(1.0 + jnp.exp(-jnp.abs(v)))

    def per_head(cols):
        tiles = []
        for j in range(0, hg, 2):
            lo = jnp.broadcast_to(cols[j], (L, 2 * P))
            hi = jnp.broadcast_to(cols[j + 1], (L, 2 * P))
            tiles.append(jnp.where(lane2 < P, lo, hi))
        return jnp.concatenate(tiles, axis=1)

    def direction(d, first):
        st_scr[...] = jnp.zeros_like(st_scr)
        cum_c = tril if d == 0 else triu
        cum_r = triu if d == 0 else tril

        def body(ci, carry):
            c = ci if d == 0 else nc - 1 - ci
            r0 = pl.multiple_of(c * L, L)
            x = xs_scr[pl.ds(r0, L), :]
            bm = bs_scr[pl.ds(r0, L), :]
            cm = cs_scr[pl.ds(r0, L), :]
            dt_c = softplus(dtc_ref[pl.ds(r0, L), :] + dtbc_ref[...])
            dt_r = softplus(dtr_ref[:, pl.ds(r0, L)] + dtbr_ref[...])
            h1, h2, h3 = _split3(dt_c * a_row)
            acum_c = _dot(cum_c, h1) + _dot(cum_c, h2) + _dot(cum_c, h3)
            g1, g2, g3 = _split3(dt_r * a_col)
            acum_r = _dot(g1, cum_r) + _dot(g2, cum_r) + _dot(g3, cum_r)
            ones_rows = jnp.ones((8, L), BF16)
            tot = (_dot(ones_rows, h1) + _dot(ones_rows, h2) + _dot(ones_rows, h3))[0:1]
            cb = _dot_nt(cm.astype(BF16), bm.astype(BF16))
            keep = (si <= ti) if d == 0 else (si >= ti)
            ys, in_cols, out_cols, dec_cols = [], [], [], []
            for hh in range(hg):
                j = d * nh + g * hg + hh
                sel_c = (lax.broadcasted_iota(jnp.int32, (L, LANE), 1) == j)
                ac = jnp.sum(jnp.where(sel_c, acum_c, 0.0), axis=1, keepdims=True)
                dc = jnp.sum(jnp.where(sel_c, dt_c, 0.0), axis=1, keepdims=True)
                te = jnp.sum(jnp.where(sel_c[0:1], tot, 0.0), axis=1, keepdims=True)
                sel_r = (lax.broadcasted_iota(jnp.int32, (2 * nh, L), 0) == j)
                ar = jnp.sum(jnp.where(sel_r, acum_r, 0.0), axis=0, keepdims=True)
                dr = jnp.sum(jnp.where(sel_r, dt_r, 0.0), axis=0, keepdims=True)
                seg = ac - ar
                decay = jnp.where(keep, jnp.exp(jnp.where(keep, seg, 0.0)), 0.0)
                w = (cb * decay * dr).astype(BF16)
                ys.append(_dot(w, x[:, hh * P:(hh + 1) * P].astype(BF16)))
                in_cols.append(jnp.exp(te - ac) * dc)
                out_cols.append(jnp.exp(ac))
                dec_cols.append(jnp.broadcast_to(jnp.exp(te), (1, P)))
            y_diag = jnp.concatenate(ys, axis=1)
            st = st_scr[...]
            y_off = _dot(cm.astype(BF16), st.astype(BF16)) * per_head(out_cols)
            x_in = (x * per_head(in_cols)).astype(BF16)
            st_scr[...] = st * jnp.concatenate(dec_cols, axis=1) + _dot(bm.T.astype(BF16), x_in)
            y = y_diag + y_off
            if first:
                y_scr[pl.ds(r0, L), :] = y
            else:
                y = y + y_scr[pl.ds(r0, L), :] + dsk_ref[...] * x
                zz = z_ref[pl.ds(r0, L), :]
                y = y * _silu(zz)
                y = y * lax.rsqrt(jnp.mean(y * y, axis=-1, keepdims=True) + RMS_EPS)
                o_ref[pl.ds(r0, L), :] = (y * nw_ref[...]).astype(o_ref.dtype)
            return carry

        lax.fori_loop(0, nc, body, 0)

    direction(0, True)
    direction(1, False)


def ssd(proj, dt_rows, conv_w, conv_b, dt_bias, a_log, d_skip, norm_w, L=M2_CHUNK):
    B, S, _ = proj.shape
    hg = M2_HEADS // M2_GROUPS
    gw = hg * M2_P
    t = np.arange(L)
    tril = jnp.asarray((t[None, :] <= t[:, None]).astype(np.float32), BF16)
    triu = jnp.asarray((t[None, :] >= t[:, None]).astype(np.float32), BF16)
    nb_c = OFF_XBC + M2_INNER
    nc_c = nb_c + M2_GROUPS * M2_N
    xw = conv_w[:, :M2_INNER]
    bw = conv_w[:, M2_INNER:M2_INNER + M2_GROUPS * M2_N]
    cw = conv_w[:, M2_INNER + M2_GROUPS * M2_N:]
    cb2 = conv_b.reshape(1, -1)
    xb = cb2[:, :M2_INNER]
    bb = cb2[:, M2_INNER:M2_INNER + M2_GROUPS * M2_N]
    cbb = cb2[:, M2_INNER + M2_GROUPS * M2_N:]
    nh2 = 2 * M2_HEADS
    dtb_row = jnp.zeros((1, LANE), F32).at[0, :nh2].set(dt_bias.reshape(-1))
    al_row = jnp.zeros((1, LANE), F32).at[0, :nh2].set(a_log.reshape(-1))
    dtb_col = dt_bias.reshape(nh2, 1)
    al_col = a_log.reshape(nh2, 1)
    dsk = jnp.repeat(d_skip, M2_P).reshape(1, M2_INNER)
    nw = norm_w.reshape(1, M2_INNER)

    full2 = lambda a: pl.BlockSpec(a.shape, lambda b, g: (0, 0))
    return pl.pallas_call(
        functools.partial(_ssd_kernel, L=L, hg=hg, P=M2_P),
        grid=(B, M2_GROUPS),
        in_specs=[
            pl.BlockSpec((None, S, gw), lambda b, g: (b, 0, OFF_XBC // gw + g)),
            pl.BlockSpec((None, S, M2_N), lambda b, g: (b, 0, nb_c // M2_N + g)),
            pl.BlockSpec((None, S, M2_N), lambda b, g: (b, 0, nc_c // M2_N + g)),
            pl.BlockSpec((None, S, gw), lambda b, g: (b, 0, OFF_Z // gw + g)),
            pl.BlockSpec((None, S, LANE), lambda b, g: (b, 0, OFF_DT // LANE)),
            pl.BlockSpec((nh2, S), lambda b, g: (0, b)),
            pl.BlockSpec((xw.shape[0], gw), lambda b, g: (0, g)),
            pl.BlockSpec((bw.shape[0], M2_N), lambda b, g: (0, g)),
            pl.BlockSpec((cw.shape[0], M2_N), lambda b, g: (0, g)),
            pl.BlockSpec((1, gw), lambda b, g: (0, g)),
            pl.BlockSpec((1, M2_N), lambda b, g: (0, g)),
            pl.BlockSpec((1, M2_N), lambda b, g: (0, g)),
            full2(dtb_row), full2(dtb_col), full2(al_row), full2(al_col),
            pl.BlockSpec((1, gw), lambda b, g: (0, g)),
            pl.BlockSpec((1, gw), lambda b, g: (0, g)),
            full2(tril), full2(triu),
        ],
        out_specs=pl.BlockSpec((None, S, gw), lambda b, g: (b, 0, g)),
        out_shape=jax.ShapeDtypeStruct((B, S, M2_INNER), BF16),
        scratch_shapes=[pltpu.VMEM((S, gw), F32), pltpu.VMEM((S, M2_N), F32),
                        pltpu.VMEM((S, M2_N), F32), pltpu.VMEM((S, gw), F32),
                        pltpu.VMEM((M2_N, gw), F32)],
        compiler_params=_cparams("parallel", "parallel"),
        name="ssd",
    )(proj, proj, proj, proj, proj, dt_rows, xw, bw, cw, xb, bb, cbb,
      dtb_row, dtb_col, al_row, al_col, dsk, nw, tril, triu)


def _mix_out_kernel(h_ref, a_ref, b_ref, wa_ref, wb_ref, o_ref):
    o_ref[...] = h_ref[...] + _dot(a_ref[...], wa_ref[...]) + _dot(b_ref[...], wb_ref[...])


def mix_out(h, oa, ob, w, layer, tm=512):
    m, n = h.shape
    ka, kb = oa.shape[1], ob.shape[1]
    assert ka == kb
    return pl.pallas_call(
        _mix_out_kernel,
        grid=(m // tm,),
        in_specs=[pl.BlockSpec((tm, n), lambda i: (i, 0)),
                  pl.BlockSpec((tm, ka), lambda i: (i, 0)),
                  pl.BlockSpec((tm, kb), lambda i: (i, 0)),
                  pl.BlockSpec((None, ka, n), lambda i: (layer, 0, 0)),
                  pl.BlockSpec((None, kb, n), lambda i: (layer, 1, 0))],
        out_specs=pl.BlockSpec((tm, n), lambda i: (i, 0)),
        out_shape=jax.ShapeDtypeStruct((m, n), F32),
        compiler_params=_cparams("parallel"),
        name="mix_out",
    )(h, oa, ob, w, w)


def _ffn_kernel(h_ref, nw_ref, wg_ref, wu_ref, wd_ref, o_ref, xn_scr, acc_scr):
    f = pl.program_id(1)

    @pl.when(f == 0)
    def _():
        x = h_ref[...]
        xn_scr[...] = _rms(x, nw_ref[...]).astype(BF16)
        acc_scr[...] = x

    xn = xn_scr[...]
    act = _silu(_dot(xn, wg_ref[...])) * _dot(xn, wu_ref[...])
    acc_scr[...] += _dot(act.astype(BF16), wd_ref[...])

    @pl.when(f == pl.num_programs(1) - 1)
    def _():
        o_ref[...] = acc_scr[...]


def ffn(h, nw, w_gu, w_down, layer, tm=1024, tf=512):
    m, d = h.shape
    dff = w_down.shape[1]
    nf = dff // tf
    return pl.pallas_call(
        _ffn_kernel,
        grid=(m // tm, nf),
        in_specs=[pl.BlockSpec((tm, d), lambda i, f: (i, 0)),
                  pl.BlockSpec((1, d), lambda i, f: (0, 0)),
                  pl.BlockSpec((None, d, tf), lambda i, f: (layer, 0, f)),
                  pl.BlockSpec((None, d, tf), lambda i, f: (layer, 0, nf + f)),
                  pl.BlockSpec((None, tf, d), lambda i, f: (layer, f, 0))],
        out_specs=pl.BlockSpec((tm, d), lambda i, f: (i, 0)),
        out_shape=jax.ShapeDtypeStruct((m, d), F32),
        scratch_shapes=[pltpu.VMEM((tm, d), BF16), pltpu.VMEM((tm, d), F32)],
        compiler_params=_cparams("parallel", "arbitrary"),
        name="ffn",
    )(h, nw.reshape(1, d), w_gu, w_gu, w_down)


def _ple_kernel(*refs, moe, final):
    if moe:
        h_ref, y2_ref, meta_ref, p_ref, nw_ref, wg_ref, wp_ref, fw_ref, o_ref = refs
        meta = meta_ref[...]
        d = h_ref.shape[1]
        h = h_ref[...] + (meta[:, 2:3] * y2_ref[:, 0:d] + meta[:, 3:4] * y2_ref[:, d:2 * d])
    else:
        h_ref, p_ref, nw_ref, wg_ref, wp_ref, fw_ref, o_ref = refs
        h = h_ref[...]
    gate = _sigmoid(_dot(_rms(h, nw_ref[...]).astype(BF16), wg_ref[...]))
    h = h + gate * _dot(p_ref[...].astype(BF16), wp_ref[...])
    if final:
        h = _rms(h, fw_ref[...])
    o_ref[...] = h


def ple(h, p, nw, wg, wp, fw, layer, y2=None, meta=None, final=False, tm=512):
    m, d = h.shape
    moe = y2 is not None
    row = lambda w: pl.BlockSpec((tm, w), lambda i: (i, 0))
    slab = lambda a: pl.BlockSpec((None,) + a.shape[1:], lambda i: (layer, 0, 0))
    vec = pl.BlockSpec((1, d), lambda i: (0, 0))
    in_specs = [row(d)]
    args = [h]
    if moe:
        in_specs += [row(2 * d), row(meta.shape[1])]
        args += [y2, meta]
    in_specs += [pl.BlockSpec((None, tm, p.shape[2]), lambda i: (layer, i, 0)), vec, slab(wg), slab(wp), vec]
    args += [p, nw.reshape(1, d), wg, wp, fw.reshape(1, d)]
    return pl.pallas_call(
        functools.partial(_ple_kernel, moe=moe, final=final),
        grid=(m // tm,),
        in_specs=in_specs,
        out_specs=row(d),
        out_shape=jax.ShapeDtypeStruct((m, d), F32),
        compiler_params=_cparams("parallel"),
        name="ple",
    )(*args)


def _s5_tables(a_re, a_im, log_step, b_re, b_im, c_re, c_im):
    G, N = a_re.shape[1:]
    C = S5_GROUP_SIZE
    T = S5_STEPS
    gt = LANE // C
    Z = G // gt
    tau = jnp.arange(T + 1, dtype=F32)
    steps = jnp.arange(T)

    def cmul(xr, xi, yr, yi):
        return xr * yr - xi * yi, xr * yi + xi * yr

    st_parts, rd_parts, k_parts, lam_rows = [], [], [], []
    for d in range(2):
        delta = jnp.exp(log_step[d])[:, None]
        ar, ai = a_re[d], a_im[d]
        mag = jnp.exp(ar * delta)
        lam_re, lam_im = mag * jnp.cos(ai * delta), mag * jnp.sin(ai * delta)
        den = ar * ar + ai * ai
        num_re = lam_re - 1.0
        coef_re = (num_re * ar + lam_im * ai) / den
        coef_im = (lam_im * ar - num_re * ai) / den
        br = coef_re[..., None] * b_re - coef_im[..., None] * b_im
        bi = coef_re[..., None] * b_im + coef_im[..., None] * b_re
        cr, ci = c_re[d], c_im[d]
        pm = jnp.exp((ar * delta)[None] * tau[:, None, None])
        ang = (ai * delta)[None] * tau[:, None, None]
        pr, pi = pm * jnp.cos(ang), pm * jnp.sin(ang)
        e_in = (T - 1 - steps) if d == 0 else steps
        sr, si = cmul(pr[e_in][..., None], pi[e_in][..., None], br[None], bi[None])
        e_out = (steps + 1) if d == 0 else (T - steps)
        cpr, cpi = cmul(cr[None], ci[None], pr[e_out][:, :, None, :], pi[e_out][:, :, None, :])
        lbr, lbi = cmul(pr[:T][..., None], pi[:T][..., None], br[None], bi[None])
        ktau = jnp.einsum('gon,tgni->tgoi', cr, lbr) - jnp.einsum('gon,tgni->tgoi', ci, lbi)
        st_parts += [sr, si]
        rd_parts += [cpr, -cpi]
        k_parts.append(ktau)
        lam_rows += [pr[T], pi[T]]

    eye = jnp.eye(gt, dtype=F32)
    K = T * LANE
    NS = gt * N
    st = jnp.stack(st_parts).reshape(4, T, Z, gt, N, C).transpose(2, 1, 3, 5, 0, 4)
    wst = (st[:, :, :, :, :, None, :] * eye[None, None, :, None, None, :, None]).reshape(Z, K, 4 * NS)
    rd = jnp.stack(rd_parts).reshape(4, T, Z, gt, C, N).transpose(2, 0, 5, 1, 3, 4)
    wc = (rd[:, :, None, :, :, :, :] * eye[None, None, :, None, None, :, None]).reshape(Z, 4 * NS, K)
    lag = steps[None, :] - steps[:, None]
    sel_f = (lag[:, :, None] == steps[None, None, :]).astype(F32)
    sel_b = (-lag[:, :, None] == steps[None, None, :]).astype(F32)
    toe = (jnp.einsum('stk,kgoi->stgoi', sel_f, k_parts[0])
           + jnp.einsum('stk,kgoi->stgoi', sel_b, k_parts[1]))
    toe = toe.reshape(T, T, Z, gt, C, C).transpose(2, 0, 3, 5, 1, 4)
    kin = (toe[:, :, :, :, :, None, :] * eye[None, None, :, None, None, :, None]).reshape(Z, K, K)
    lam = jnp.stack([r.reshape(Z, NS) for r in lam_rows], axis=1)
    lam = jnp.concatenate([lam, jnp.zeros_like(lam)], axis=1)
    return wst.astype(BF16), kin.astype(BF16), wc.astype(BF16), lam


def _s5_pre_kernel(x_ref, nw_ref, u_ref, xn_scr, tmp_scr):
    nb, R, D = x_ref.shape
    T = S5_STEPS
    rc = R // T
    xn = _rms(x_ref[...], nw_ref[...])
    for z in range(D // LANE):
        xn_scr[z] = xn[:, :, z * LANE:(z + 1) * LANE]
    for z in range(D // LANE):
        for t in range(T):
            for b in range(nb):
                tmp_scr[z, t, pl.ds(b, rc, stride=nb), :] = xn_scr[z, b, pl.ds(t, rc, stride=T), :]
            u_ref[z, :, t * LANE:(t + 1) * LANE] = tmp_scr[z, t].astype(u_ref.dtype)


def s5_pre(h3, nw, R=64):
    B, S, D = h3.shape
    T = S5_STEPS
    Z = D // LANE
    rc = R // T
    return pl.pallas_call(
        _s5_pre_kernel,
        grid=(S // R,),
        in_specs=[pl.BlockSpec((B, R, D), lambda i: (0, i, 0)),
                  pl.BlockSpec((1, D), lambda i: (0, 0))],
        out_specs=pl.BlockSpec((Z, rc * B, T * LANE), lambda i: (0, i, 0)),
        out_shape=jax.ShapeDtypeStruct((Z, S // T * B, T * LANE), BF16),
        scratch_shapes=[pltpu.VMEM((Z, B, R, LANE), F32), pltpu.VMEM((Z, T, rc * B, LANE), F32)],
        compiler_params=_cparams("parallel"),
        name="s5_pre",
    )(h3, nw.reshape(1, D))


def _s5_kernel(u_ref, wst_ref, kin_ref, wc_ref, lam_ref, y_ref, s_scr, *, nb, rblk):
    rows = u_ref.shape[0]
    nc = rows // nb
    ns = lam_ref.shape[1]
    for r0 in range(0, rows, rblk):
        s_scr[r0:r0 + rblk, :] = _dot(u_ref[r0:r0 + rblk, :], wst_ref[...])
    lam = lam_ref[...]
    lfr, lfi, lbr, lbi = lam[0:1], lam[1:2], lam[2:3], lam[3:4]

    def body(ci, carry):
        hfr, hfi, hbr, hbi = carry
        rf = pl.multiple_of(ci * nb, nb)
        rb = pl.multiple_of((nc - 1 - ci) * nb, nb)
        sfr = s_scr[pl.ds(rf, nb), 0:ns]
        sfi = s_scr[pl.ds(rf, nb), ns:2 * ns]
        sbr = s_scr[pl.ds(rb, nb), 2 * ns:3 * ns]
        sbi = s_scr[pl.ds(rb, nb), 3 * ns:4 * ns]
        s_scr[pl.ds(rf, nb), 0:ns] = hfr
        s_scr[pl.ds(rf, nb), ns:2 * ns] = hfi
        s_scr[pl.ds(rb, nb), 2 * ns:3 * ns] = hbr
        s_scr[pl.ds(rb, nb), 3 * ns:4 * ns] = hbi
        return (lfr * hfr - lfi * hfi + sfr, lfr * hfi + lfi * hfr + sfi,
                lbr * hbr - lbi * hbi + sbr, lbr * hbi + lbi * hbr + sbi)

    z = jnp.zeros((nb, ns), F32)
    lax.fori_loop(0, nc, body, (z, z, z, z))
    for r0 in range(0, rows, rblk):
        y = (_dot(u_ref[r0:r0 + rblk, :], kin_ref[...])
             + _dot(s_scr[r0:r0 + rblk, :].astype(BF16), wc_ref[...]))
        y_ref[r0:r0 + rblk, :] = y.astype(y_ref.dtype)


def s5_scan(u, wst, kin, wc, lam, nb, rblk=512):
    Z, rows, K = u.shape
    per = lambda a: pl.BlockSpec((None,) + a.shape[1:], lambda s: (s, 0, 0))
    once = lambda a: pl.BlockSpec((None,) + a.shape[1:], lambda s: (s, 0, 0),
                                  pipeline_mode=pl.Buffered(1))
    return pl.pallas_call(
        functools.partial(_s5_kernel, nb=nb, rblk=rblk),
        grid=(Z,),
        in_specs=[per(u), once(wst), once(kin), once(wc), per(lam)],
        out_specs=pl.BlockSpec((None, rows, K), lambda s: (s, 0, 0)),
        out_shape=jax.ShapeDtypeStruct((Z, rows, K), BF16),
        scratch_shapes=[pltpu.VMEM((rows, wst.shape[2]), F32)],
        compiler_params=_cparams("parallel"),
        name="s5_scan",
    )(u, wst, kin, wc, lam)


def _s5_post_kernel(h_ref, y_ref, nw_ref, d_ref, wo_ref, wg_ref, o_ref, yf_scr, yt_scr):
    nb, R, D = h_ref.shape
    T = S5_STEPS
    rc = R // T
    for z in range(D // LANE):
        for t in range(T):
            yf_scr[z, t] = y_ref[z, :, t * LANE:(t + 1) * LANE].astype(F32)
            for b in range(nb):
                yt_scr[z, b, pl.ds(t, rc, stride=T), :] = yf_scr[z, t, pl.ds(b, rc, stride=nb), :]
    yt = jnp.concatenate([yt_scr[z] for z in range(D // LANE)], axis=-1)
    h = h_ref[...].reshape(nb * R, D)
    y = yt.reshape(nb * R, D) + d_ref[...] * _rms(h, nw_ref[...])
    act = jax.nn.gelu(y).astype(BF16)
    out = h + _dot(act, wo_ref[...]) * _sigmoid(_dot(act, wg_ref[...]))
    o_ref[...] = out.reshape(nb, R, D)


def s5_post(h3, y, nw, d_skip, glu_w, layer, R=64):
    B, S, D = h3.shape
    T = S5_STEPS
    Z = D // LANE
    rc = R // T
    row = pl.BlockSpec((B, R, D), lambda i: (0, i, 0))
    vec = pl.BlockSpec((1, D), lambda i: (0, 0))
    return pl.pallas_call(
        _s5_post_kernel,
        grid=(S // R,),
        in_specs=[row, pl.BlockSpec((Z, rc * B, T * LANE), lambda i: (0, i, 0)), vec, vec,
                  pl.BlockSpec((None, D, D), lambda i: (layer, 0, 0)),
                  pl.BlockSpec((None, D, D), lambda i: (layer, 0, 1))],
        out_specs=row,
        out_shape=jax.ShapeDtypeStruct((B, S, D), F32),
        scratch_shapes=[pltpu.VMEM((Z, T, rc * B, LANE), F32), pltpu.VMEM((Z, B, R, LANE), F32)],
        compiler_params=_cparams("parallel"),
        name="s5_post",
    )(h3, y, nw.reshape(1, D), d_skip.reshape(1, D), glu_w, glu_w)


def _router_kernel(h_ref, nw_ref, wr_ref, tri_ref, meta_ref, cnt_ref, carry_scr):
    @pl.when(pl.program_id(0) == 0)
    def _():
        carry_scr[...] = jnp.zeros_like(carry_scr)

    xn = _rms(h_ref[...], nw_ref[...])
    logits = jnp.dot(xn, wr_ref[...], preferred_element_type=F32, precision=lax.Precision.HIGHEST)
    lane = lax.broadcasted_iota(jnp.int32, logits.shape, 1)
    neg = jnp.float32(-jnp.inf)
    lg = jnp.where(lane < N_EXPERTS, logits, neg)
    t1 = jnp.max(lg, axis=1, keepdims=True)
    i1 = jnp.min(jnp.where(lg == t1, lane, LANE), axis=1, keepdims=True)
    lg2 = jnp.where(lane == i1, neg, lg)
    t2 = jnp.max(lg2, axis=1, keepdims=True)
    i2 = jnp.min(jnp.where(lg2 == t2, lane, LANE), axis=1, keepdims=True)
    ex = jnp.exp(t2 - t1)
    g1 = 1.0 / (1.0 + ex)
    g2 = ex / (1.0 + ex)
    oh1 = jnp.where(lane == i1, 1.0, 0.0)
    oh2 = jnp.where(lane == i2, 1.0, 0.0)
    tri = tri_ref[...]
    before1 = _dot(tri, oh1.astype(BF16))
    before2 = _dot(tri, oh2.astype(BF16))
    tot1 = jnp.sum(oh1, axis=0, keepdims=True)
    tot2 = jnp.sum(oh2, axis=0, keepdims=True)
    carry = carry_scr[0:1]
    rank1 = jnp.sum(oh1 * (carry + before1), axis=1, keepdims=True)
    rank2 = jnp.sum(oh2 * (carry + tot1 + before2), axis=1, keepdims=True)
    counts = jnp.broadcast_to(carry + tot1 + tot2, carry_scr.shape)
    carry_scr[...] = counts
    cnt_ref[...] = counts
    meta = jnp.where(lane == 0, i1.astype(F32), 0.0)
    meta = jnp.where(lane == 1, i2.astype(F32), meta)
    meta = jnp.where(lane == 2, g1, meta)
    meta = jnp.where(lane == 3, g2, meta)
    meta = jnp.where(lane == 4, rank1, meta)
    meta = jnp.where(lane == 5, rank2, meta)
    meta_ref[...] = meta


def router(h, nw, w_router, tm=512):
    m, d = h.shape
    wr = jnp.zeros((d, LANE), F32).at[:, :N_EXPERTS].set(w_router)
    t = np.arange(tm)
    tri = jnp.asarray((t[None, :] < t[:, None]).astype(np.float32), BF16)
    return pl.pallas_call(
        _router_kernel,
        grid=(m // tm,),
        in_specs=[pl.BlockSpec((tm, d), lambda i: (i, 0)),
                  pl.BlockSpec((1, d), lambda i: (0, 0)),
                  pl.BlockSpec((d, LANE), lambda i: (0, 0)),
                  pl.BlockSpec((tm, tm), lambda i: (0, 0))],
        out_specs=[pl.BlockSpec((tm, LANE), lambda i: (i, 0)),
                   pl.BlockSpec((8, LANE), lambda i: (0, 0))],
        out_shape=[jax.ShapeDtypeStruct((m, LANE), F32), jax.ShapeDtypeStruct((8, LANE), F32)],
        scratch_shapes=[pltpu.VMEM((8, LANE), F32)],
        compiler_params=_cparams("arbitrary"),
        name="router",
    )(h, nw.reshape(1, d), wr, tri)


def _moe_kernel(arow_ref, blk_e_ref, nvalid_ref, h_hbm, nw_ref, wg_ref, wu_ref, wd_ref,
                y2_hbm, xg_scr, xn_scr, acc_scr, sem_in, sem_out, *, bm, n_tok):
    g = pl.program_id(0)
    f = pl.program_id(1)
    nf = pl.num_programs(1)
    slot = g % 2
    other = 1 - slot
    d = acc_scr.shape[2]
    rows_per_step = bm // nf
    prv, cur, nxt = g * bm, (g + 1) * bm, (g + 2) * bm
    nv = nvalid_ref[g]

    def in_copy(off, r, s):
        tok = jnp.minimum(arow_ref[off + r] >> 1, n_tok - 1)
        return pltpu.make_async_copy(h_hbm.at[pl.ds(tok, 1)], xg_scr.at[s, pl.ds(r, 1)], sem_in.at[s])

    def out_copy(off, r, s):
        a = arow_ref[off + r]
        col = pl.multiple_of((a & 1) * d, d)
        return pltpu.make_async_copy(acc_scr.at[s, pl.ds(r, 1)],
                                     y2_hbm.at[pl.ds(a >> 1, 1), pl.ds(col, d)], sem_out.at[s])

    def wait_in(s):
        pltpu.make_async_copy(h_hbm.at[pl.ds(0, bm)], xg_scr.at[s], sem_in.at[s]).wait()

    def wait_out(s):
        pltpu.make_async_copy(acc_scr.at[s], y2_hbm.at[pl.ds(0, bm), pl.ds(0, d)], sem_out.at[s]).wait()

    @pl.when((g == 0) & (f == 0))
    def _():
        acc_scr[...] = jnp.zeros_like(acc_scr)

        def start(r, c):
            in_copy(cur, r, slot).start()
            return c

        lax.fori_loop(0, bm, start, 0)

    @pl.when(f == 0)
    def _():
        wait_in(slot)
        xn_scr[...] = _rms(xg_scr[slot], nw_ref[...]).astype(BF16)
        acc_scr[slot] = jnp.zeros((bm, d), F32)

    def move_rows(part, parts):
        base = pl.multiple_of(f * rows_per_step, 8)
        per = rows_per_step // parts
        for j in range(part * per, (part + 1) * per):
            in_copy(nxt, base + j, other).start()
            out_copy(prv, base + j, other).start()

    @pl.when(nv > 0)
    def _():
        parts = 2
        tc = wg_ref.shape[2] // parts
        xn = xn_scr[...]
        upd = None
        for c in range(parts):
            cs = slice(c * tc, (c + 1) * tc)
            act = _silu(_dot(xn, wg_ref[0, :, cs])) * _dot(xn, wu_ref[0, :, cs])
            part = _dot(act.astype(BF16), wd_ref[0, cs, :])
            upd = part if upd is None else upd + part
            move_rows(c, parts)
        acc_scr[slot] += upd

    @pl.when(nv == 0)
    def _():
        move_rows(0, 1)

    @pl.when(f == nf - 1)
    def _():
        wait_out(other)

    @pl.when((f == nf - 1) & (g == pl.num_programs(0) - 1))
    def _():
        wait_in(other)


def moe_experts(h, nw, w_gu, w_down, arow, blk_e, nvalid, layer, bm=MOE_BM, tf=512):
    t, d = h.shape
    dff = w_down.shape[1]
    nf = dff // tf
    assert bm % nf == 0 and bm % 16 == 0
    n_steps = blk_e.shape[0]
    e0 = layer * N_EXPERTS

    def wmap(col0):
        def index(g, f, ar, be, nv):
            return (e0 + be[g], 0, col0 + jnp.where(nv[g] > 0, f, nf - 1))
        return index

    def dmap(g, f, ar, be, nv):
        return (e0 + be[g], jnp.where(nv[g] > 0, f, nf - 1), 0)

    grid_spec = pltpu.PrefetchScalarGridSpec(
        num_scalar_prefetch=3,
        grid=(n_steps, nf),
        in_specs=[pl.BlockSpec(memory_space=pl.ANY),
                  pl.BlockSpec((1, d), lambda g, f, ar, be, nv: (0, 0)),
                  pl.BlockSpec((1, d, tf), wmap(0)),
                  pl.BlockSpec((1, d, tf), wmap(nf)),
                  pl.BlockSpec((1, tf, d), dmap)],
        out_specs=pl.BlockSpec(memory_space=pl.ANY),
        scratch_shapes=[pltpu.VMEM((2, bm, d), F32), pltpu.VMEM((bm, d), BF16),
                        pltpu.VMEM((2, bm, d), F32),
                        pltpu.SemaphoreType.DMA((2,)), pltpu.SemaphoreType.DMA((2,))],
    )
    return pl.pallas_call(
        functools.partial(_moe_kernel, bm=bm, n_tok=t),
        grid_spec=grid_spec,
        out_shape=jax.ShapeDtypeStruct((t + bm // 2, 2 * d), F32),
        compiler_params=_cparams("arbitrary", "arbitrary"),
        name="moe_experts",
    )(arow, blk_e, nvalid, h, nw.reshape(1, d), w_gu, w_gu, w_down)


def _moe_plan(meta, counts, bm):
    t = meta.shape[0]
    n_assign = 2 * t
    experts = jnp.arange(N_EXPERTS, dtype=jnp.int32)
    cnt = counts[0, :N_EXPERTS].astype(jnp.int32)
    padded = (cnt + bm - 1) // bm * bm
    pend = jnp.cumsum(padded)
    pstart = pend - padded
    e = meta[:, 0:2].astype(jnp.int32)
    rank = meta[:, 4:6].astype(jnp.int32)
    dest = jnp.sum(jnp.where(e[..., None] == experts, pstart, 0), axis=-1) + rank
    n_blocks = -(-n_assign // bm) + N_EXPERTS
    spare = 2 * t + jnp.arange(bm, dtype=jnp.int32)
    arow = jnp.tile(spare, n_blocks).at[dest.reshape(-1)].set(jnp.arange(n_assign, dtype=jnp.int32))
    arow = jnp.concatenate([spare, arow, spare, spare])
    blk_start = jnp.arange(n_blocks, dtype=jnp.int32) * bm
    blk_e = jnp.minimum(jnp.sum((pend[None, :] <= blk_start[:, None]).astype(jnp.int32), axis=1),
                        N_EXPERTS - 1)
    nvalid = jnp.clip((pstart + cnt)[blk_e] - blk_start, 0, bm)
    nvalid = jnp.where(blk_start < pend[-1], nvalid, 0).astype(jnp.int32)
    blk_e = jnp.concatenate([blk_e, blk_e[-1:]]).astype(jnp.int32)
    nvalid = jnp.concatenate([nvalid, jnp.zeros((1,), jnp.int32)])
    return arow, blk_e, nvalid


def kernel(x, p, norm_mix, norm_ffn, norm_ple, final_norm, ple_gate, ple_proj, ev_w_in, ev_w_out,
           hg_lb_logits, hg_norm_w, m2_conv_w, m2_conv_b, m2_dt_bias, m2_a_log, m2_d, m2_norm_w,
           s5_a_re, s5_a_im, s5_log_step, s5_b_re, s5_b_im, s5_c_re, s5_c_im, s5_d, s5_glu_w,
           ffn_w_gu, ffn_w_down, moe_router, moe_w_gu, moe_w_down):
    B, S, D = x.shape
    T = B * S
    depth = norm_mix.shape[0]
    lb_soft = jax.nn.softmax(hg_lb_logits.astype(F32), axis=0)
    hg_lb = jnp.cumsum(lb_soft, axis=0) - lb_soft[0]
    w_in = jnp.pad(ev_w_in, ((0, 0), (0, 0), (0, EVEN_IN_PAD - ev_w_in.shape[2]))).astype(BF16)
    w_dt_t = jnp.swapaxes(ev_w_in[:, :, OFF_DT:], 1, 2).astype(BF16)
    w_out = ev_w_out.astype(BF16)
    w_ffn_gu, w_ffn_down = ffn_w_gu.astype(BF16), ffn_w_down.astype(BF16)
    w_moe_gu = moe_w_gu.astype(BF16).reshape((-1,) + moe_w_gu.shape[2:])
    w_moe_down = moe_w_down.astype(BF16).reshape((-1,) + moe_w_down.shape[2:])
    w_glu = s5_glu_w.astype(BF16)
    w_ple_gate, w_ple_proj = ple_gate.astype(BF16), ple_proj.astype(BF16)
    p_rows = p.reshape(depth, T, p.shape[-1])
    h = x.reshape(T, D)
    for layer in range(depth):
        j = layer // 2
        if layer % 2 == 0:
            proj, dt_rows = rms_matmul(h, norm_mix[layer], w_in, w_dt_t, j)
            proj = proj.reshape(B, S, EVEN_IN_PAD)
            o_a = hgrn2(proj, hg_lb[j], hg_norm_w[j])
            o_b = ssd(proj, dt_rows, m2_conv_w[j], m2_conv_b[j], m2_dt_bias[j], m2_a_log[j],
                      m2_d[j], m2_norm_w[j])
            h = mix_out(h, o_a.reshape(T, HG_WIDTH), o_b.reshape(T, M2_INNER), w_out, j)
            h = ffn(h, norm_ffn[layer], w_ffn_gu, w_ffn_down, j)
            y2 = meta = None
        else:
            h3 = h.reshape(B, S, D)
            wst, kin, wc, lam = _s5_tables(s5_a_re[j], s5_a_im[j], s5_log_step[j], s5_b_re[j],
                                           s5_b_im[j], s5_c_re[j], s5_c_im[j])
            y = s5_scan(s5_pre(h3, norm_mix[layer]), wst, kin, wc, lam, B)
            h = s5_post(h3, y, norm_mix[layer], s5_d[j], w_glu, j).reshape(T, D)
            meta, counts = router(h, norm_ffn[layer], moe_router[j])
            arow, blk_e, nvalid = _moe_plan(meta, counts, MOE_BM)
            y2 = moe_experts(h, norm_ffn[layer], w_moe_gu, w_moe_down, arow, blk_e, nvalid, j)
        h = ple(h, p_rows, norm_ple[layer], w_ple_gate, w_ple_proj, final_norm, layer,
                y2=y2, meta=meta, final=(layer == depth - 1))
    return h.reshape(B, S, D)
```

```python
import functools

import numpy as np
import jax
import jax.numpy as jnp
from jax import lax
from jax.experimental import pallas as pl
from jax.experimental.pallas import tpu as pltpu

F32 = jnp.float32
BF16 = jnp.bfloat16

RMS_EPS = 1e-6
LB_FLOOR = 1e-30
LANE = 128
VMEM_LIMIT = 56 * 1024 * 1024

HG_HEADS = 4
HG_DIM = 128
HG_WIDTH = HG_HEADS * HG_DIM
HG_CHUNK = 64
M2_HEADS = 8
M2_P = 64
M2_INNER = M2_HEADS * M2_P
M2_GROUPS = 2
M2_N = 128
M2_CHUNK = 128
M2_XBC = M2_INNER + 2 * M2_GROUPS * M2_N
S5_GROUP_SIZE = 16
S5_STATE = 64
S5_STEPS = 8
N_EXPERTS = 8
MOE_BM = 576

OFF_Q, OFF_FF, OFF_FB, OFF_I, OFF_G = (k * HG_WIDTH for k in range(5))
OFF_Z = 5 * HG_WIDTH
OFF_XBC = OFF_Z + M2_INNER
OFF_DT = OFF_XBC + M2_XBC
EVEN_IN_PAD = OFF_DT + LANE


def _cparams(*sem):
    return pltpu.CompilerParams(dimension_semantics=sem, vmem_limit_bytes=VMEM_LIMIT)


def _rms(x, w):
    return x * lax.rsqrt(jnp.mean(x * x, axis=-1, keepdims=True) + RMS_EPS) * w


def _sigmoid(x):
    return 1.0 / (1.0 + jnp.exp(-x))


def _silu(x):
    return x * _sigmoid(x)


def _dot(a, b):
    return jnp.dot(a, b, preferred_element_type=F32)


def _dot_nt(a, b):
    return lax.dot_general(a, b, (((1,), (1,)), ((), ())), preferred_element_type=F32)


def _split3(x):
    hi = x.astype(BF16)
    r1 = x - hi.astype(F32)
    mid = r1.astype(BF16)
    lo = (r1 - mid.astype(F32)).astype(BF16)
    return hi, mid, lo


def _rms_matmul_kernel(x_ref, nw_ref, w_ref, wdt_ref, o_ref, dtr_ref):
    xn = _rms(x_ref[...], nw_ref[...]).astype(BF16)
    o_ref[...] = _dot(xn, w_ref[...])
    dtr_ref[...] = _dot_nt(wdt_ref[...], xn)


def rms_matmul(x, nw, w, w_dt_t, layer, tm=256):
    m, k = x.shape
    n = w.shape[2]
    nr = w_dt_t.shape[1]
    return pl.pallas_call(
        _rms_matmul_kernel,
        grid=(m // tm,),
        in_specs=[pl.BlockSpec((tm, k), lambda i: (i, 0)),
                  pl.BlockSpec((1, k), lambda i: (0, 0)),
                  pl.BlockSpec((None, k, n), lambda i: (layer, 0, 0)),
                  pl.BlockSpec((None, nr, k), lambda i: (layer, 0, 0))],
        out_specs=[pl.BlockSpec((tm, n), lambda i: (i, 0)),
                   pl.BlockSpec((nr, tm), lambda i: (0, i))],
        out_shape=[jax.ShapeDtypeStruct((m, n), F32), jax.ShapeDtypeStruct((nr, m), F32)],
        compiler_params=_cparams("parallel"),
        name="rms_matmul",
    )(x, nw.reshape(1, k), w, w_dt_t)


def _hgrn2_masks(L):
    t = np.arange(L)
    masks = []
    w = L // 2
    while w >= 1:
        blk = t // w
        masks.append((blk % 2 == 1)[:, None] & (blk[None, :] == blk[:, None] - 1))
        w //= 2
    masks.append(np.eye(L, dtype=bool))
    m_f = np.stack(masks).astype(np.float32)
    return m_f, np.ascontiguousarray(m_f[:, ::-1, ::-1]), len(masks) - 1


def _pair_ref(b, w, reverse):
    L, d = b.shape
    off = w if reverse else w - 1
    if 2 * w > 8:
        pieces = [jnp.broadcast_to(b[p * 2 * w + off:p * 2 * w + off + 1, :], (2 * w, d))
                  for p in range(L // (2 * w))]
        return pieces[0] if len(pieces) == 1 else jnp.concatenate(pieces, axis=0)
    b3 = b.reshape(L // 8, 8, d)
    sub = lax.broadcasted_iota(jnp.int32, b3.shape, 1)
    ref = None
    for p in range(8 // (2 * w)):
        row = jnp.broadcast_to(b3[:, p * 2 * w + off:p * 2 * w + off + 1, :], b3.shape)
        ref = row if ref is None else jnp.where(sub >= p * 2 * w, row, ref)
    return ref.reshape(L, d)


def _hgrn2_kernel(qp_ref, ffp_ref, fbp_ref, ip_ref, gp_ref, lb_ref, nw_ref,
                  trif_ref, trib_ref, mf_ref, mb_ref, o_ref,
                  of_scr, ob_scr, stf_scr, stb_scr, *, L, nl):
    S = qp_ref.shape[0]
    nc = S // L
    lb = lb_ref[...]
    lb_floor = jnp.maximum(lb, LB_FLOOR)
    one_m_lb = 1.0 - lb

    def chunk(c, fpre_ref, tri_ref, m_ref, st_scr, reverse):
        r0 = pl.multiple_of(c * L, L)
        q = _silu(qp_ref[pl.ds(r0, L), :])
        v = ip_ref[pl.ds(r0, L), :]
        f = lb_floor + one_m_lb * _sigmoid(fpre_ref[pl.ds(r0, L), :])
        logf = jnp.log(f)
        k = 1.0 - f
        hi, mid, lo = _split3(logf)
        tri = tri_ref[...]
        b = _dot(tri, hi) + _dot(tri, mid) + _dot(tri, lo)
        a = m_ref[nl] * _dot_nt(q.astype(BF16), k.astype(BF16))
        w = L // 2
        for lvl in range(nl):
            e = jnp.exp(-jnp.abs(b - _pair_ref(b, w, reverse)))
            a = a + m_ref[lvl] * _dot_nt((q * e).astype(BF16), (k * e).astype(BF16))
            w //= 2
        tot_row = b[0:1] if reverse else b[L - 1:L]
        q_in = (q * jnp.exp(b)).astype(BF16)
        k_out = (k * jnp.exp(tot_row - b)).astype(BF16)
        st = st_scr[...]
        o = _dot(a.astype(BF16), v.astype(BF16)) + _dot_nt(q_in, st.astype(BF16))
        st_scr[...] = st * jnp.exp(tot_row) + _dot(v.T.astype(BF16), k_out)
        return r0, o

    stf_scr[...] = jnp.zeros_like(stf_scr)
    stb_scr[...] = jnp.zeros_like(stb_scr)

    def body(ci, carry):
        r0, o = chunk(ci, ffp_ref, trif_ref, mf_ref, stf_scr, False)
        of_scr[pl.ds(r0, L), :] = o
        r0, o = chunk(nc - 1 - ci, fbp_ref, trib_ref, mb_ref, stb_scr, True)
        ob_scr[pl.ds(r0, L), :] = o
        return carry

    lax.fori_loop(0, nc, body, 0, unroll=4)
    nw = nw_ref[...]
    R = 4 * L

    def finish(i, carry):
        r0 = pl.multiple_of(i * R, R)
        o = of_scr[pl.ds(r0, R), :] + ob_scr[pl.ds(r0, R), :]
        o = o * lax.rsqrt(jnp.mean(o * o, axis=-1, keepdims=True) + RMS_EPS)
        o = o * nw * _sigmoid(gp_ref[pl.ds(r0, R), :])
        o_ref[pl.ds(r0, R), :] = o.astype(o_ref.dtype)
        return carry

    lax.fori_loop(0, S // R, finish, 0)


def hgrn2(proj, lb, norm_w, L=HG_CHUNK):
    B, S, _ = proj.shape
    m_f, m_b, nl = _hgrn2_masks(L)
    t = np.arange(L)
    tri_f = jnp.asarray((t[None, :] <= t[:, None]).astype(np.float32), BF16)
    tri_b = jnp.asarray((t[None, :] >= t[:, None]).astype(np.float32), BF16)
    d = HG_DIM
    nb = HG_WIDTH // d

    def col(off):
        return pl.BlockSpec((None, S, d), lambda b, h, off=off: (b, 0, off // d + h))

    vec = pl.BlockSpec((1, d), lambda b, h: (0, h))
    const2 = lambda a: pl.BlockSpec(a.shape, lambda b, h: (0, 0))
    const3 = lambda a: pl.BlockSpec(a.shape, lambda b, h: (0, 0, 0))
    return pl.pallas_call(
        functools.partial(_hgrn2_kernel, L=L, nl=nl),
        grid=(B, nb),
        in_specs=[col(OFF_Q), col(OFF_FF), col(OFF_FB), col(OFF_I), col(OFF_G), vec, vec,
                  const2(tri_f), const2(tri_b), const3(m_f), const3(m_b)],
        out_specs=pl.BlockSpec((None, S, d), lambda b, h: (b, 0, h)),
        out_shape=jax.ShapeDtypeStruct((B, S, HG_WIDTH), BF16),
        scratch_shapes=[pltpu.VMEM((S, d), F32), pltpu.VMEM((S, d), F32),
                        pltpu.VMEM((d, d), F32), pltpu.VMEM((d, d), F32)],
        compiler_params=_cparams("parallel", "parallel"),
        name="hgrn2",
    )(proj, proj, proj, proj, proj, lb.reshape(1, HG_WIDTH), norm_w.reshape(1, HG_WIDTH),
      tri_f, tri_b, jnp.asarray(m_f), jnp.asarray(m_b))


def _shift_rows(x, k):
    if k == 0:
        return x
    n = x.shape[0]
    rolled = pltpu.roll(x, (-k) % n, 0)
    t = lax.broadcasted_iota(jnp.int32, x.shape, 0)
    ok = (t + k >= 0) & (t + k < n)
    return jnp.where(ok, rolled, 0.0)


def _conv_silu(x, w, b):
    half = w.shape[0] // 2
    acc = b
    for j in range(w.shape[0]):
        acc = acc + w[j:j + 1, :] * _shift_rows(x, j - half)
    return _silu(acc)


def _ssd_kernel(x_ref, b_ref, c_ref, z_ref, dtc_ref, dtr_ref,
                cwx_ref, cwb_ref, cwc_ref, cbx_ref, cbb_ref, cbc_ref,
                dtbc_ref, dtbr_ref, alr_ref, alc_ref, dsk_ref, nw_ref,
                tril_ref, triu_ref,
                o_ref, xs_scr, bs_scr, cs_scr, yf_scr, yb_scr, stf_scr, stb_scr, *, L, hg, P):
    S = x_ref.shape[0]
    nc = S // L
    nh = 2 * hg
    gw = hg * P
    N = bs_scr.shape[1]

    a_row = -jnp.exp(alr_ref[...])
    a_col = -jnp.exp(alc_ref[...])
    tril = tril_ref[...]
    triu = triu_ref[...]
    ti = lax.broadcasted_iota(jnp.int32, (L, L), 0)
    si = lax.broadcasted_iota(jnp.int32, (L, L), 1)
    lane2 = lax.broadcasted_iota(jnp.int32, (L, 2 * P), 1)

    def softplus(v):
        return jnp.maximum(v, 0.0) + jnp.log(1.0 + jnp.exp(-jnp.abs(v)))

    def per_head(cols):
        tiles = []
        for j in range(0, hg, 2):
            lo = jnp.broadcast_to(cols[j], (L, 2 * P))
            hi = jnp.broadcast_to(cols[j + 1], (L, 2 * P))
            tiles.append(jnp.where(lane2 < P, lo, hi))
        return jnp.concatenate(tiles, axis=1)

    def chunk(c, d, g, st_scr):
        cum_c = tril if d == 0 else triu
        cum_r = triu if d == 0 else tril
        r0 = pl.multiple_of(c * L, L)
        x = xs_scr[pl.ds(r0, L), :]
        bm = bs_scr[pl.ds(r0, L), :]
        cm = cs_scr[pl.ds(r0, L), :]
        dt_c = softplus(dtc_ref[pl.ds(r0, L), :] + dtbc_ref[...])
        dt_r = softplus(dtr_ref[:, pl.ds(r0, L)] + dtbr_ref[...])
        h1, h2, h3 = _split3(dt_c * a_row)
        acum_c = _dot(cum_c, h1) + _dot(cum_c, h2) + _dot(cum_c, h3)
        g1, g2, g3 = _split3(dt_r * a_col)
        acum_r = _dot(g1, cum_r) + _dot(g2, cum_r) + _dot(g3, cum_r)
        ones_rows = jnp.ones((8, L), BF16)
        tot = (_dot(ones_rows, h1) + _dot(ones_rows, h2) + _dot(ones_rows, h3))[0:1]
        cb = _dot_nt(cm.astype(BF16), bm.astype(BF16))
        keep = (si <= ti) if d == 0 else (si >= ti)
        ys, in_cols, out_cols, dec_cols = [], [], [], []
        for hh in range(hg):
            j = d * nh + g * hg + hh
            ac = acum_c[:, j:j + 1]
            dc = dt_c[:, j:j + 1]
            te = tot[:, j:j + 1]
            ar = acum_r[j:j + 1, :]
            dr = dt_r[j:j + 1, :]
            seg = ac - ar
            decay = jnp.where(keep, jnp.exp(jnp.where(keep, seg, 0.0)), 0.0)
            w = (cb * decay * dr).astype(BF16)
            ys.append(_dot(w, x[:, hh * P:(hh + 1) * P].astype(BF16)))
            in_cols.append(jnp.exp(te - ac) * dc)
            out_cols.append(jnp.exp(ac))
            dec_cols.append(jnp.broadcast_to(jnp.exp(te), (1, P)))
        y_diag = jnp.concatenate(ys, axis=1)
        st = st_scr[...]
        y_off = _dot(cm.astype(BF16), st.astype(BF16)) * per_head(out_cols)
        x_in = (x * per_head(in_cols)).astype(BF16)
        st_scr[...] = st * jnp.concatenate(dec_cols, axis=1) + _dot(bm.T.astype(BF16), x_in)
        return r0, y_diag + y_off

    R = 2 * L
    for g in range(x_ref.shape[1] // gw):
        cols = slice(g * gw, (g + 1) * gw)
        ncols = slice(g * N, (g + 1) * N)
        xs_scr[...] = _conv_silu(x_ref[:, cols], cwx_ref[:, cols], cbx_ref[:, cols])
        bs_scr[...] = _conv_silu(b_ref[:, ncols], cwb_ref[:, ncols], cbb_ref[:, ncols])
        cs_scr[...] = _conv_silu(c_ref[:, ncols], cwc_ref[:, ncols], cbc_ref[:, ncols])
        stf_scr[...] = jnp.zeros_like(stf_scr)
        stb_scr[...] = jnp.zeros_like(stb_scr)

        def body(ci, carry, g=g):
            r0, y = chunk(ci, 0, g, stf_scr)
            yf_scr[pl.ds(r0, L), :] = y
            r0, y = chunk(nc - 1 - ci, 1, g, stb_scr)
            yb_scr[pl.ds(r0, L), :] = y
            return carry

        lax.fori_loop(0, nc, body, 0, unroll=2)

        def finish(i, carry, cols=cols):
            r0 = pl.multiple_of(i * R, R)
            y = (yf_scr[pl.ds(r0, R), :] + yb_scr[pl.ds(r0, R), :]
                 + dsk_ref[:, cols] * xs_scr[pl.ds(r0, R), :])
            y = y * _silu(z_ref[pl.ds(r0, R), cols])
            y = y * lax.rsqrt(jnp.mean(y * y, axis=-1, keepdims=True) + RMS_EPS)
            o_ref[pl.ds(r0, R), cols] = (y * nw_ref[:, cols]).astype(o_ref.dtype)
            return carry

        lax.fori_loop(0, S // R, finish, 0)


def ssd(proj, dt_rows, conv_w, conv_b, dt_bias, a_log, d_skip, norm_w, L=M2_CHUNK):
    B, S, _ = proj.shape
    hg = M2_HEADS // M2_GROUPS
    gw = hg * M2_P
    t = np.arange(L)
    tril = jnp.asarray((t[None, :] <= t[:, None]).astype(np.float32), BF16)
    triu = jnp.asarray((t[None, :] >= t[:, None]).astype(np.float32), BF16)
    nb_c = OFF_XBC + M2_INNER
    nc_c = nb_c + M2_GROUPS * M2_N
    xw = conv_w[:, :M2_INNER]
    bw = conv_w[:, M2_INNER:M2_INNER + M2_GROUPS * M2_N]
    cw = conv_w[:, M2_INNER + M2_GROUPS * M2_N:]
    cb2 = conv_b.reshape(1, -1)
    xb = cb2[:, :M2_INNER]
    bb = cb2[:, M2_INNER:M2_INNER + M2_GROUPS * M2_N]
    cbb = cb2[:, M2_INNER + M2_GROUPS * M2_N:]
    nh2 = 2 * M2_HEADS
    dtb_row = jnp.zeros((1, LANE), F32).at[0, :nh2].set(dt_bias.reshape(-1))
    al_row = jnp.zeros((1, LANE), F32).at[0, :nh2].set(a_log.reshape(-1))
    dtb_col = dt_bias.reshape(nh2, 1)
    al_col = a_log.reshape(nh2, 1)
    dsk = jnp.repeat(d_skip, M2_P).reshape(1, M2_INNER)
    nw = norm_w.reshape(1, M2_INNER)

    gn = M2_GROUPS * M2_N
    full2 = lambda a: pl.BlockSpec(a.shape, lambda b: (0, 0))
    return pl.pallas_call(
        functools.partial(_ssd_kernel, L=L, hg=hg, P=M2_P),
        grid=(B,),
        in_specs=[
            pl.BlockSpec((None, S, M2_INNER), lambda b: (b, 0, OFF_XBC // M2_INNER)),
            pl.BlockSpec((None, S, gn), lambda b: (b, 0, nb_c // gn)),
            pl.BlockSpec((None, S, gn), lambda b: (b, 0, nc_c // gn)),
            pl.BlockSpec((None, S, M2_INNER), lambda b: (b, 0, OFF_Z // M2_INNER)),
            pl.BlockSpec((None, S, LANE), lambda b: (b, 0, OFF_DT // LANE)),
            pl.BlockSpec((nh2, S), lambda b: (0, b)),
            full2(xw), full2(bw), full2(cw), full2(xb), full2(bb), full2(cbb),
            full2(dtb_row), full2(dtb_col), full2(al_row), full2(al_col),
            full2(dsk), full2(nw), full2(tril), full2(triu),
        ],
        out_specs=pl.BlockSpec((None, S, M2_INNER), lambda b: (b, 0, 0)),
        out_shape=jax.ShapeDtypeStruct((B, S, M2_INNER), BF16),
        scratch_shapes=[pltpu.VMEM((S, gw), F32), pltpu.VMEM((S, M2_N), F32),
                        pltpu.VMEM((S, M2_N), F32), pltpu.VMEM((S, gw), F32),
                        pltpu.VMEM((S, gw), F32), pltpu.VMEM((M2_N, gw), F32),
                        pltpu.VMEM((M2_N, gw), F32)],
        compiler_params=_cparams("parallel"),
        name="ssd",
    )(proj, proj, proj, proj, proj, dt_rows, xw, bw, cw, xb, bb, cbb,
      dtb_row, dtb_col, al_row, al_col, dsk, nw, tril, triu)


def _mix_out_kernel(h_ref, a_ref, b_ref, wa_ref, wb_ref, o_ref):
    o_ref[...] = h_ref[...] + _dot(a_ref[...], wa_ref[...]) + _dot(b_ref[...], wb_ref[...])


def mix_out(h, oa, ob, w, layer, tm=512):
    m, n = h.shape
    ka, kb = oa.shape[1], ob.shape[1]
    assert ka == kb
    return pl.pallas_call(
        _mix_out_kernel,
        grid=(m // tm,),
        in_specs=[pl.BlockSpec((tm, n), lambda i: (i, 0)),
                  pl.BlockSpec((tm, ka), lambda i: (i, 0)),
                  pl.BlockSpec((tm, kb), lambda i: (i, 0)),
                  pl.BlockSpec((None, ka, n), lambda i: (layer, 0, 0)),
                  pl.BlockSpec((None, kb, n), lambda i: (layer, 1, 0))],
        out_specs=pl.BlockSpec((tm, n), lambda i: (i, 0)),
        out_shape=jax.ShapeDtypeStruct((m, n), F32),
        compiler_params=_cparams("parallel"),
        name="mix_out",
    )(h, oa, ob, w, w)


def _ffn_kernel(h_ref, nw_ref, wg_ref, wu_ref, wd_ref, o_ref, xn_scr, acc_scr):
    f = pl.program_id(1)

    @pl.when(f == 0)
    def _():
        x = h_ref[...]
        xn_scr[...] = _rms(x, nw_ref[...]).astype(BF16)
        acc_scr[...] = x

    xn = xn_scr[...]
    act = _silu(_dot(xn, wg_ref[...])) * _dot(xn, wu_ref[...])
    acc_scr[...] += _dot(act.astype(BF16), wd_ref[...])

    @pl.when(f == pl.num_programs(1) - 1)
    def _():
        o_ref[...] = acc_scr[...]


def ffn(h, nw, w_gu, w_down, layer, tm=1024, tf=512):
    m, d = h.shape
    dff = w_down.shape[1]
    nf = dff // tf
    return pl.pallas_call(
        _ffn_kernel,
        grid=(m // tm, nf),
        in_specs=[pl.BlockSpec((tm, d), lambda i, f: (i, 0)),
                  pl.BlockSpec((1, d), lambda i, f: (0, 0)),
                  pl.BlockSpec((None, d, tf), lambda i, f: (layer, 0, f)),
                  pl.BlockSpec((None, d, tf), lambda i, f: (layer, 0, nf + f)),
                  pl.BlockSpec((None, tf, d), lambda i, f: (layer, f, 0))],
        out_specs=pl.BlockSpec((tm, d), lambda i, f: (i, 0)),
        out_shape=jax.ShapeDtypeStruct((m, d), F32),
        scratch_shapes=[pltpu.VMEM((tm, d), BF16), pltpu.VMEM((tm, d), F32)],
        compiler_params=_cparams("parallel", "arbitrary"),
        name="ffn",
    )(h, nw.reshape(1, d), w_gu, w_gu, w_down)


def _ple_kernel(*refs, moe, final):
    if moe:
        h_ref, y2_ref, meta_ref, p_ref, nw_ref, wg_ref, wp_ref, fw_ref, o_ref = refs
        meta = meta_ref[...]
        d = h_ref.shape[1]
        h = h_ref[...] + (meta[:, 2:3] * y2_ref[:, 0:d] + meta[:, 3:4] * y2_ref[:, d:2 * d])
    else:
        h_ref, p_ref, nw_ref, wg_ref, wp_ref, fw_ref, o_ref = refs
        h = h_ref[...]
    gate = _sigmoid(_dot(_rms(h, nw_ref[...]).astype(BF16), wg_ref[...]))
    h = h + gate * _dot(p_ref[...].astype(BF16), wp_ref[...])
    if final:
        h = _rms(h, fw_ref[...])
    o_ref[...] = h


def ple(h, p, nw, wg, wp, fw, layer, y2=None, meta=None, final=False, tm=512):
    m, d = h.shape
    moe = y2 is not None
    row = lambda w: pl.BlockSpec((tm, w), lambda i: (i, 0))
    slab = lambda a: pl.BlockSpec((None,) + a.shape[1:], lambda i: (layer, 0, 0))
    vec = pl.BlockSpec((1, d), lambda i: (0, 0))
    in_specs = [row(d)]
    args = [h]
    if moe:
        in_specs += [row(2 * d), row(meta.shape[1])]
        args += [y2, meta]
    in_specs += [pl.BlockSpec((None, tm, p.shape[2]), lambda i: (layer, i, 0)), vec, slab(wg), slab(wp), vec]
    args += [p, nw.reshape(1, d), wg, wp, fw.reshape(1, d)]
    return pl.pallas_call(
        functools.partial(_ple_kernel, moe=moe, final=final),
        grid=(m // tm,),
        in_specs=in_specs,
        out_specs=row(d),
        out_shape=jax.ShapeDtypeStruct((m, d), F32),
        compiler_params=_cparams("parallel"),
        name="ple",
    )(*args)


def _s5_tables(a_re, a_im, log_step, b_re, b_im, c_re, c_im):
    G, N = a_re.shape[1:]
    C = S5_GROUP_SIZE
    T = S5_STEPS
    gt = LANE // C
    Z = G // gt
    tau = jnp.arange(T + 1, dtype=F32)
    steps = jnp.arange(T)

    def cmul(xr, xi, yr, yi):
        return xr * yr - xi * yi, xr * yi + xi * yr

    st_parts, rd_parts, k_parts, lam_rows = [], [], [], []
    for d in range(2):
        delta = jnp.exp(log_step[d])[:, None]
        ar, ai = a_re[d], a_im[d]
        mag = jnp.exp(ar * delta)
        lam_re, lam_im = mag * jnp.cos(ai * delta), mag * jnp.sin(ai * delta)
        den = ar * ar + ai * ai
        num_re = lam_re - 1.0
        coef_re = (num_re * ar + lam_im * ai) / den
        coef_im = (lam_im * ar - num_re * ai) / den
        br = coef_re[..., None] * b_re - coef_im[..., None] * b_im
        bi = coef_re[..., None] * b_im + coef_im[..., None] * b_re
        cr, ci = c_re[d], c_im[d]
        pm = jnp.exp((ar * delta)[None] * tau[:, None, None])
        ang = (ai * delta)[None] * tau[:, None, None]
        pr, pi = pm * jnp.cos(ang), pm * jnp.sin(ang)
        e_in = (T - 1 - steps) if d == 0 else steps
        sr, si = cmul(pr[e_in][..., None], pi[e_in][..., None], br[None], bi[None])
        e_out = (steps + 1) if d == 0 else (T - steps)
        cpr, cpi = cmul(cr[None], ci[None], pr[e_out][:, :, None, :], pi[e_out][:, :, None, :])
        lbr, lbi = cmul(pr[:T][..., None], pi[:T][..., None], br[None], bi[None])
        ktau = jnp.einsum('gon,tgni->tgoi', cr, lbr) - jnp.einsum('gon,tgni->tgoi', ci, lbi)
        st_parts += [sr, si]
        rd_parts += [cpr, -cpi]
        k_parts.append(ktau)
        lam_rows += [pr[T], pi[T]]

    NS = gt * N
    st = jnp.stack(st_parts).reshape(4, T, Z, gt, N, C).transpose(2, 1, 5, 0, 3, 4)
    wst = block_diag_rows(st.reshape(Z, T * C, 4 * NS), C, N)
    rd = jnp.stack(rd_parts).reshape(4, T, Z, gt, C, N).transpose(2, 1, 4, 0, 3, 5)
    wc = jnp.swapaxes(block_diag_rows(rd.reshape(Z, T * C, 4 * NS), C, N), 1, 2)
    lag = steps[None, :] - steps[:, None]
    sel_f = (lag[:, :, None] == steps[None, None, :]).astype(F32)
    sel_b = (-lag[:, :, None] == steps[None, None, :]).astype(F32)
    toe = (jnp.einsum('stk,kgoi->stgoi', sel_f, k_parts[0])
           + jnp.einsum('stk,kgoi->stgoi', sel_b, k_parts[1]))
    toe = toe.reshape(T, T, Z, gt, C, C).transpose(2, 0, 5, 1, 3, 4)
    kin = block_diag_rows(toe.reshape(Z, T * C, T * LANE), C, C)
    lam = jnp.stack([r.reshape(Z, NS) for r in lam_rows], axis=1)
    lam = jnp.concatenate([lam, jnp.zeros_like(lam)], axis=1)
    return wst, kin, wc, lam


def _block_diag_kernel(s_ref, o_ref, *, C, gcol):
    n = o_ref.shape[1]
    gt = LANE // C
    row_g = lax.broadcasted_iota(jnp.int32, (LANE, n), 0) // C
    col_g = (lax.broadcasted_iota(jnp.int32, (LANE, n), 1) // gcol) % gt
    keep = row_g == col_g
    for t in range(o_ref.shape[0] // LANE):
        strip = s_ref[t * C:(t + 1) * C, :]
        tiled = jnp.broadcast_to(strip[None], (gt, C, n)).reshape(LANE, n)
        o_ref[t * LANE:(t + 1) * LANE, :] = jnp.where(keep, tiled, 0.0).astype(o_ref.dtype)


def block_diag_rows(strips, C, gcol):
    Z, rows, n = strips.shape
    gt = LANE // C
    return pl.pallas_call(
        functools.partial(_block_diag_kernel, C=C, gcol=gcol),
        grid=(Z,),
        in_specs=[pl.BlockSpec((None, rows, n), lambda z: (z, 0, 0))],
        out_specs=pl.BlockSpec((None, rows * gt, n), lambda z: (z, 0, 0)),
        out_shape=jax.ShapeDtypeStruct((Z, rows * gt, n), BF16),
        compiler_params=_cparams("parallel"),
        name="block_diag_rows",
    )(strips)


def _s5_pre_kernel(x_ref, nw_ref, u_ref, xn_scr, tmp_scr):
    nb, R, D = x_ref.shape
    T = S5_STEPS
    rc = R // T
    xn = _rms(x_ref[...], nw_ref[...])
    for z in range(D // LANE):
        xn_scr[z] = xn[:, :, z * LANE:(z + 1) * LANE]
    for z in range(D // LANE):
        for t in range(T):
            for b in range(nb):
                tmp_scr[z, t, pl.ds(b, rc, stride=nb), :] = xn_scr[z, b, pl.ds(t, rc, stride=T), :]
            u_ref[z, :, t * LANE:(t + 1) * LANE] = tmp_scr[z, t].astype(u_ref.dtype)


def s5_pre(h3, nw, R=64):
    B, S, D = h3.shape
    T = S5_STEPS
    Z = D // LANE
    rc = R // T
    return pl.pallas_call(
        _s5_pre_kernel,
        grid=(S // R,),
        in_specs=[pl.BlockSpec((B, R, D), lambda i: (0, i, 0)),
                  pl.BlockSpec((1, D), lambda i: (0, 0))],
        out_specs=pl.BlockSpec((Z, rc * B, T * LANE), lambda i: (0, i, 0)),
        out_shape=jax.ShapeDtypeStruct((Z, S // T * B, T * LANE), BF16),
        scratch_shapes=[pltpu.VMEM((Z, B, R, LANE), F32), pltpu.VMEM((Z, T, rc * B, LANE), F32)],
        compiler_params=_cparams("parallel"),
        name="s5_pre",
    )(h3, nw.reshape(1, D))


def _s5_kernel(u_ref, wst_ref, kin_ref, wc_ref, lam_ref, y_ref, s_scr, *, nb, rblk):
    rows = u_ref.shape[0]
    nc = rows // nb
    ns = lam_ref.shape[1]
    for r0 in range(0, rows, rblk):
        s_scr[r0:r0 + rblk, :] = _dot(u_ref[r0:r0 + rblk, :], wst_ref[...])
    lam = lam_ref[...]
    lfr, lfi, lbr, lbi = lam[0:1], lam[1:2], lam[2:3], lam[3:4]

    def body(ci, carry):
        hfr, hfi, hbr, hbi = carry
        rf = pl.multiple_of(ci * nb, nb)
        rb = pl.multiple_of((nc - 1 - ci) * nb, nb)
        sfr = s_scr[pl.ds(rf, nb), 0:ns]
        sfi = s_scr[pl.ds(rf, nb), ns:2 * ns]
        sbr = s_scr[pl.ds(rb, nb), 2 * ns:3 * ns]
        sbi = s_scr[pl.ds(rb, nb), 3 * ns:4 * ns]
        s_scr[pl.ds(rf, nb), 0:ns] = hfr
        s_scr[pl.ds(rf, nb), ns:2 * ns] = hfi
        s_scr[pl.ds(rb, nb), 2 * ns:3 * ns] = hbr
        s_scr[pl.ds(rb, nb), 3 * ns:4 * ns] = hbi
        return (lfr * hfr - lfi * hfi + sfr, lfr * hfi + lfi * hfr + sfi,
                lbr * hbr - lbi * hbi + sbr, lbr * hbi + lbi * hbr + sbi)

    z = jnp.zeros((nb, ns), F32)
    lax.fori_loop(0, nc, body, (z, z, z, z))
    for r0 in range(0, rows, rblk):
        y = (_dot(u_ref[r0:r0 + rblk, :], kin_ref[...])
             + _dot(s_scr[r0:r0 + rblk, :].astype(BF16), wc_ref[...]))
        y_ref[r0:r0 + rblk, :] = y.astype(y_ref.dtype)


def s5_scan(u, wst, kin, wc, lam, nb, rblk=512):
    Z, rows, K = u.shape
    per = lambda a: pl.BlockSpec((None,) + a.shape[1:], lambda s: (s, 0, 0))
    once = lambda a: pl.BlockSpec((None,) + a.shape[1:], lambda s: (s, 0, 0),
                                  pipeline_mode=pl.Buffered(1))
    return pl.pallas_call(
        functools.partial(_s5_kernel, nb=nb, rblk=rblk),
        grid=(Z,),
        in_specs=[per(u), once(wst), once(kin), once(wc), per(lam)],
        out_specs=pl.BlockSpec((None, rows, K), lambda s: (s, 0, 0)),
        out_shape=jax.ShapeDtypeStruct((Z, rows, K), BF16),
        scratch_shapes=[pltpu.VMEM((rows, wst.shape[2]), F32)],
        compiler_params=_cparams("parallel"),
        name="s5_scan",
    )(u, wst, kin, wc, lam)


def _s5_post_kernel(h_ref, y_ref, nw_ref, d_ref, wo_ref, wg_ref, o_ref, yf_scr, yt_scr):
    nb, R, D = h_ref.shape
    T = S5_STEPS
    rc = R // T
    for z in range(D // LANE):
        for t in range(T):
            yf_scr[z, t] = y_ref[z, :, t * LANE:(t + 1) * LANE].astype(F32)
            for b in range(nb):
                yt_scr[z, b, pl.ds(t, rc, stride=T), :] = yf_scr[z, t, pl.ds(b, rc, stride=nb), :]
    yt = jnp.concatenate([yt_scr[z] for z in range(D // LANE)], axis=-1)
    h = h_ref[...].reshape(nb * R, D)
    y = yt.reshape(nb * R, D) + d_ref[...] * _rms(h, nw_ref[...])
    act = jax.nn.gelu(y).astype(BF16)
    out = h + _dot(act, wo_ref[...]) * _sigmoid(_dot(act, wg_ref[...]))
    o_ref[...] = out.reshape(nb, R, D)


def s5_post(h3, y, nw, d_skip, glu_w, layer, R=64):
    B, S, D = h3.shape
    T = S5_STEPS
    Z = D // LANE
    rc = R // T
    row = pl.BlockSpec((B, R, D), lambda i: (0, i, 0))
    vec = pl.BlockSpec((1, D), lambda i: (0, 0))
    return pl.pallas_call(
        _s5_post_kernel,
        grid=(S // R,),
        in_specs=[row, pl.BlockSpec((Z, rc * B, T * LANE), lambda i: (0, i, 0)), vec, vec,
                  pl.BlockSpec((None, D, D), lambda i: (layer, 0, 0)),
                  pl.BlockSpec((None, D, D), lambda i: (layer, 0, 1))],
        out_specs=row,
        out_shape=jax.ShapeDtypeStruct((B, S, D), F32),
        scratch_shapes=[pltpu.VMEM((Z, T, rc * B, LANE), F32), pltpu.VMEM((Z, B, R, LANE), F32)],
        compiler_params=_cparams("parallel"),
        name="s5_post",
    )(h3, y, nw.reshape(1, D), d_skip.reshape(1, D), glu_w, glu_w)


def _router_kernel(h_ref, nw_ref, wr_ref, tri_ref, meta_ref, cnt_ref, carry_scr):
    @pl.when(pl.program_id(0) == 0)
    def _():
        carry_scr[...] = jnp.zeros_like(carry_scr)

    xn = _rms(h_ref[...], nw_ref[...])
    logits = jnp.dot(xn, wr_ref[...], preferred_element_type=F32, precision=lax.Precision.HIGHEST)
    lane = lax.broadcasted_iota(jnp.int32, logits.shape, 1)
    neg = jnp.float32(-jnp.inf)
    lg = jnp.where(lane < N_EXPERTS, logits, neg)
    t1 = jnp.max(lg, axis=1, keepdims=True)
    i1 = jnp.min(jnp.where(lg == t1, lane, LANE), axis=1, keepdims=True)
    lg2 = jnp.where(lane == i1, neg, lg)
    t2 = jnp.max(lg2, axis=1, keepdims=True)
    i2 = jnp.min(jnp.where(lg2 == t2, lane, LANE), axis=1, keepdims=True)
    ex = jnp.exp(t2 - t1)
    g1 = 1.0 / (1.0 + ex)
    g2 = ex / (1.0 + ex)
    oh1 = jnp.where(lane == i1, 1.0, 0.0)
    oh2 = jnp.where(lane == i2, 1.0, 0.0)
    tri = tri_ref[...]
    before1 = _dot(tri, oh1.astype(BF16))
    before2 = _dot(tri, oh2.astype(BF16))
    tot1 = jnp.sum(oh1, axis=0, keepdims=True)
    tot2 = jnp.sum(oh2, axis=0, keepdims=True)
    carry = carry_scr[0:1]
    rank1 = jnp.sum(oh1 * (carry + before1), axis=1, keepdims=True)
    rank2 = jnp.sum(oh2 * (carry + tot1 + before2), axis=1, keepdims=True)
    counts = jnp.broadcast_to(carry + tot1 + tot2, carry_scr.shape)
    carry_scr[...] = counts
    cnt_ref[...] = counts
    meta = jnp.where(lane == 0, i1.astype(F32), 0.0)
    meta = jnp.where(lane == 1, i2.astype(F32), meta)
    meta = jnp.where(lane == 2, g1, meta)
    meta = jnp.where(lane == 3, g2, meta)
    meta = jnp.where(lane == 4, rank1, meta)
    meta = jnp.where(lane == 5, rank2, meta)
    meta_ref[...] = meta


def router(h, nw, w_router, tm=512):
    m, d = h.shape
    wr = jnp.zeros((d, LANE), F32).at[:, :N_EXPERTS].set(w_router)
    t = np.arange(tm)
    tri = jnp.asarray((t[None, :] < t[:, None]).astype(np.float32), BF16)
    return pl.pallas_call(
        _router_kernel,
        grid=(m // tm,),
        in_specs=[pl.BlockSpec((tm, d), lambda i: (i, 0)),
                  pl.BlockSpec((1, d), lambda i: (0, 0)),
                  pl.BlockSpec((d, LANE), lambda i: (0, 0)),
                  pl.BlockSpec((tm, tm), lambda i: (0, 0))],
        out_specs=[pl.BlockSpec((tm, LANE), lambda i: (i, 0)),
                   pl.BlockSpec((8, LANE), lambda i: (0, 0))],
        out_shape=[jax.ShapeDtypeStruct((m, LANE), F32), jax.ShapeDtypeStruct((8, LANE), F32)],
        scratch_shapes=[pltpu.VMEM((8, LANE), F32)],
        compiler_params=_cparams("arbitrary"),
        name="router",
    )(h, nw.reshape(1, d), wr, tri)


def _moe_kernel(arow_ref, blk_e_ref, nvalid_ref, h_hbm, nw_ref, wg_ref, wu_ref, wd_ref,
                y2_hbm, xg_scr, xn_scr, acc_scr, sem_in, sem_out, *, bm, n_tok):
    g = pl.program_id(0)
    f = pl.program_id(1)
    nf = pl.num_programs(1)
    slot = g % 2
    other = 1 - slot
    d = acc_scr.shape[2]
    rows_per_step = bm // (nf - 1)
    prv, cur, nxt = g * bm, (g + 1) * bm, (g + 2) * bm
    nv = nvalid_ref[g]

    def in_copy(off, r, s):
        tok = jnp.minimum(arow_ref[off + r] >> 1, n_tok - 1)
        return pltpu.make_async_copy(h_hbm.at[pl.ds(tok, 1)], xg_scr.at[s, pl.ds(r, 1)], sem_in.at[s])

    def out_copy(off, r, s):
        a = arow_ref[off + r]
        col = pl.multiple_of((a & 1) * d, d)
        return pltpu.make_async_copy(acc_scr.at[s, pl.ds(r, 1)],
                                     y2_hbm.at[pl.ds(a >> 1, 1), pl.ds(col, d)], sem_out.at[s])

    def wait_in(s):
        pltpu.make_async_copy(h_hbm.at[pl.ds(0, bm)], xg_scr.at[s], sem_in.at[s]).wait()

    def wait_out(s):
        pltpu.make_async_copy(acc_scr.at[s], y2_hbm.at[pl.ds(0, bm), pl.ds(0, d)], sem_out.at[s]).wait()

    @pl.when((g == 0) & (f == 0))
    def _():
        acc_scr[...] = jnp.zeros_like(acc_scr)

        def start(r, c):
            in_copy(cur, r, slot).start()
            return c

        lax.fori_loop(0, bm, start, 0)

    @pl.when(f == 0)
    def _():
        wait_in(slot)
        xn_scr[...] = _rms(xg_scr[slot], nw_ref[...]).astype(BF16)
        acc_scr[slot] = jnp.zeros((bm, d), F32)

    def move_rows(part, parts):
        base = pl.multiple_of(f * rows_per_step, 8)
        per = rows_per_step // parts
        for j in range(part * per, (part + 1) * per):
            in_copy(nxt, base + j, other).start()
            out_copy(prv, base + j, other).start()

    def experts(with_moves):
        parts = 2
        tc = wg_ref.shape[2] // parts
        xn = xn_scr[...]
        upd = None
        for c in range(parts):
            cs = slice(c * tc, (c + 1) * tc)
            act = _silu(_dot(xn, wg_ref[0, :, cs])) * _dot(xn, wu_ref[0, :, cs])
            part = _dot(act.astype(BF16), wd_ref[0, cs, :])
            upd = part if upd is None else upd + part
            if with_moves:
                move_rows(c, parts)
        acc_scr[slot] += upd

    @pl.when((nv > 0) & (f < nf - 1))
    def _():
        experts(True)

    @pl.when((nv > 0) & (f == nf - 1))
    def _():
        experts(False)

    @pl.when((nv == 0) & (f < nf - 1))
    def _():
        move_rows(0, 1)

    @pl.when(f == nf - 1)
    def _():
        wait_out(other)

    @pl.when((f == nf - 1) & (g == pl.num_programs(0) - 1))
    def _():
        wait_in(other)


def moe_experts(h, nw, w_gu, w_down, arow, blk_e, nvalid, layer, bm=MOE_BM, tf=512):
    t, d = h.shape
    dff = w_down.shape[1]
    nf = dff // tf
    assert bm % (2 * (nf - 1)) == 0 and bm % 16 == 0
    n_steps = blk_e.shape[0]
    e0 = layer * N_EXPERTS

    def wmap(col0):
        def index(g, f, ar, be, nv):
            return (e0 + be[g], 0, col0 + jnp.where(nv[g] > 0, f, nf - 1))
        return index

    def dmap(g, f, ar, be, nv):
        return (e0 + be[g], jnp.where(nv[g] > 0, f, nf - 1), 0)

    grid_spec = pltpu.PrefetchScalarGridSpec(
        num_scalar_prefetch=3,
        grid=(n_steps, nf),
        in_specs=[pl.BlockSpec(memory_space=pl.ANY),
                  pl.BlockSpec((1, d), lambda g, f, ar, be, nv: (0, 0)),
                  pl.BlockSpec((1, d, tf), wmap(0)),
                  pl.BlockSpec((1, d, tf), wmap(nf)),
                  pl.BlockSpec((1, tf, d), dmap)],
        out_specs=pl.BlockSpec(memory_space=pl.ANY),
        scratch_shapes=[pltpu.VMEM((2, bm, d), F32), pltpu.VMEM((bm, d), BF16),
                        pltpu.VMEM((2, bm, d), F32),
                        pltpu.SemaphoreType.DMA((2,)), pltpu.SemaphoreType.DMA((2,))],
    )
    return pl.pallas_call(
        functools.partial(_moe_kernel, bm=bm, n_tok=t),
        grid_spec=grid_spec,
        out_shape=jax.ShapeDtypeStruct((t + bm // 2, 2 * d), F32),
        compiler_params=_cparams("arbitrary", "arbitrary"),
        name="moe_experts",
    )(arow, blk_e, nvalid, h, nw.reshape(1, d), w_gu, w_gu, w_down)


def _moe_plan(meta, counts, bm):
    t = meta.shape[0]
    n_assign = 2 * t
    experts = jnp.arange(N_EXPERTS, dtype=jnp.int32)
    cnt = counts[0, :N_EXPERTS].astype(jnp.int32)
    padded = (cnt + bm - 1) // bm * bm
    pend = jnp.cumsum(padded)
    pstart = pend - padded
    e = meta[:, 0:2].astype(jnp.int32)
    rank = meta[:, 4:6].astype(jnp.int32)
    dest = jnp.sum(jnp.where(e[..., None] == experts, pstart, 0), axis=-1) + rank
    n_blocks = -(-n_assign // bm) + N_EXPERTS
    spare = 2 * t + jnp.arange(bm, dtype=jnp.int32)
    arow = jnp.tile(spare, n_blocks).at[dest.reshape(-1)].set(jnp.arange(n_assign, dtype=jnp.int32))
    arow = jnp.concatenate([spare, arow, spare, spare])
    blk_start = jnp.arange(n_blocks, dtype=jnp.int32) * bm
    blk_e = jnp.minimum(jnp.sum((pend[None, :] <= blk_start[:, None]).astype(jnp.int32), axis=1),
                        N_EXPERTS - 1)
    nvalid = jnp.clip((pstart + cnt)[blk_e] - blk_start, 0, bm)
    nvalid = jnp.where(blk_start < pend[-1], nvalid, 0).astype(jnp.int32)
    blk_e = jnp.concatenate([blk_e, blk_e[-1:]]).astype(jnp.int32)
    nvalid = jnp.concatenate([nvalid, jnp.zeros((1,), jnp.int32)])
    return arow, blk_e, nvalid


def kernel(x, p, norm_mix, norm_ffn, norm_ple, final_norm, ple_gate, ple_proj, ev_w_in, ev_w_out,
           hg_lb_logits, hg_norm_w, m2_conv_w, m2_conv_b, m2_dt_bias, m2_a_log, m2_d, m2_norm_w,
           s5_a_re, s5_a_im, s5_log_step, s5_b_re, s5_b_im, s5_c_re, s5_c_im, s5_d, s5_glu_w,
           ffn_w_gu, ffn_w_down, moe_router, moe_w_gu, moe_w_down):
    B, S, D = x.shape
    T = B * S
    depth = norm_mix.shape[0]
    lb_soft = jax.nn.softmax(hg_lb_logits.astype(F32), axis=0)
    hg_lb = jnp.cumsum(lb_soft, axis=0) - lb_soft[0]
    w_in = jnp.pad(ev_w_in, ((0, 0), (0, 0), (0, EVEN_IN_PAD - ev_w_in.shape[2]))).astype(BF16)
    w_dt_t = jnp.swapaxes(ev_w_in[:, :, OFF_DT:], 1, 2).astype(BF16)
    w_out = ev_w_out.astype(BF16)
    w_ffn_gu, w_ffn_down = ffn_w_gu.astype(BF16), ffn_w_down.astype(BF16)
    w_moe_gu = moe_w_gu.astype(BF16).reshape((-1,) + moe_w_gu.shape[2:])
    w_moe_down = moe_w_down.astype(BF16).reshape((-1,) + moe_w_down.shape[2:])
    w_glu = s5_glu_w.astype(BF16)
    w_ple_gate, w_ple_proj = ple_gate.astype(BF16), ple_proj.astype(BF16)
    p_rows = p.reshape(depth, T, p.shape[-1])
    h = x.reshape(T, D)
    for layer in range(depth):
        j = layer // 2
        if layer % 2 == 0:
            proj, dt_rows = rms_matmul(h, norm_mix[layer], w_in, w_dt_t, j)
            proj = proj.reshape(B, S, EVEN_IN_PAD)
            o_a = hgrn2(proj, hg_lb[j], hg_norm_w[j])
            o_b = ssd(proj, dt_rows, m2_conv_w[j], m2_conv_b[j], m2_dt_bias[j], m2_a_log[j],
                      m2_d[j], m2_norm_w[j])
            h = mix_out(h, o_a.reshape(T, HG_WIDTH), o_b.reshape(T, M2_INNER), w_out, j)
            h = ffn(h, norm_ffn[layer], w_ffn_gu, w_ffn_down, j)
            y2 = meta = None
        else:
            h3 = h.reshape(B, S, D)
            wst, kin, wc, lam = _s5_tables(s5_a_re[j], s5_a_im[j], s5_log_step[j], s5_b_re[j],
                                           s5_b_im[j], s5_c_re[j], s5_c_im[j])
            y = s5_scan(s5_pre(h3, norm_mix[layer]), wst, kin, wc, lam, B)
            h = s5_post(h3, y, norm_mix[layer], s5_d[j], w_glu, j).reshape(T, D)
            meta, counts = router(h, norm_ffn[layer], moe_router[j])
            arow, blk_e, nvalid = _moe_plan(meta, counts, MOE_BM)
            y2 = moe_experts(h, norm_ffn[layer], w_moe_gu, w_moe_down, arow, blk_e, nvalid, j)
        h = ple(h, p_rows, norm_ple[layer], w_ple_gate, w_ple_proj, final_norm, layer,
                y2=y2, meta=meta, final=(layer == depth - 1))
    return h.reshape(B, S, D)
```

```python
import functools

import numpy as np
import jax
import jax.numpy as jnp
from jax import lax
from jax.experimental import pallas as pl
from jax.experimental.pallas import tpu as pltpu

F32 = jnp.float32
BF16 = jnp.bfloat16

RMS_EPS = 1e-6
LB_FLOOR = 1e-30
NEG_LOG2E = -1.4426950408889634
LANE = 128
VMEM_LIMIT = 56 * 1024 * 1024

HG_HEADS = 4
HG_DIM = 128
HG_WIDTH = HG_HEADS * HG_DIM
HG_CHUNK = 128
M2_HEADS = 8
M2_P = 64
M2_INNER = M2_HEADS * M2_P
M2_GROUPS = 2
M2_N = 128
M2_CHUNK = 128
M2_XBC = M2_INNER + 2 * M2_GROUPS * M2_N
S5_GROUP_SIZE = 16
S5_STATE = 64
S5_STEPS = 8
N_EXPERTS = 8
MOE_BM = 576

OFF_Q, OFF_FF, OFF_FB, OFF_I, OFF_G = (k * HG_WIDTH for k in range(5))
OFF_Z = 5 * HG_WIDTH
OFF_XBC = OFF_Z + M2_INNER
OFF_DT = OFF_XBC + M2_XBC
EVEN_IN_PAD = OFF_DT + LANE


def _cparams(*sem):
    return pltpu.CompilerParams(dimension_semantics=sem, vmem_limit_bytes=VMEM_LIMIT)


def _rms(x, w):
    return x * lax.rsqrt(jnp.mean(x * x, axis=-1, keepdims=True) + RMS_EPS) * w


def _sigmoid(x):
    return 1.0 / (1.0 + jnp.exp2(x * NEG_LOG2E))


def _silu(x):
    return x * _sigmoid(x)


def _dot(a, b):
    return jnp.dot(a, b, preferred_element_type=F32)


def _dot_nt(a, b):
    return lax.dot_general(a, b, (((1,), (1,)), ((), ())), preferred_element_type=F32)


def _split3(x):
    hi = x.astype(BF16)
    r1 = x - hi.astype(F32)
    mid = r1.astype(BF16)
    lo = (r1 - mid.astype(F32)).astype(BF16)
    return hi, mid, lo


def _rms_matmul_kernel(x_ref, nw_ref, w_ref, wdt_ref, o_ref, dtr_ref):
    xn = _rms(x_ref[...], nw_ref[...]).astype(BF16)
    o_ref[...] = _dot(xn, w_ref[...])
    dtr_ref[...] = _dot_nt(wdt_ref[...], xn)


def rms_matmul(x, nw, w, w_dt_t, layer, tm=256):
    m, k = x.shape
    n = w.shape[2]
    nr = w_dt_t.shape[1]
    return pl.pallas_call(
        _rms_matmul_kernel,
        grid=(m // tm,),
        in_specs=[pl.BlockSpec((tm, k), lambda i: (i, 0)),
                  pl.BlockSpec((1, k), lambda i: (0, 0)),
                  pl.BlockSpec((None, k, n), lambda i: (layer, 0, 0)),
                  pl.BlockSpec((None, nr, k), lambda i: (layer, 0, 0))],
        out_specs=[pl.BlockSpec((tm, n), lambda i: (i, 0)),
                   pl.BlockSpec((nr, tm), lambda i: (0, i))],
        out_shape=[jax.ShapeDtypeStruct((m, n), F32), jax.ShapeDtypeStruct((nr, m), F32)],
        compiler_params=_cparams("parallel"),
        name="rms_matmul",
    )(x, nw.reshape(1, k), w, w_dt_t)


def _hgrn2_masks(L):
    t = np.arange(L)
    masks = []
    w = L // 2
    while w >= 1:
        blk = t // w
        masks.append((blk % 2 == 1)[:, None] & (blk[None, :] == blk[:, None] - 1))
        w //= 2
    masks.append(np.eye(L, dtype=bool))
    m_f = np.stack(masks).astype(np.float32)
    return m_f, np.ascontiguousarray(m_f[:, ::-1, ::-1]), len(masks) - 1


def _pair_ref(b, w, reverse):
    L, d = b.shape
    off = w if reverse else w - 1
    if 2 * w > 8:
        pieces = [jnp.broadcast_to(b[p * 2 * w + off:p * 2 * w + off + 1, :], (2 * w, d))
                  for p in range(L // (2 * w))]
        return pieces[0] if len(pieces) == 1 else jnp.concatenate(pieces, axis=0)
    b3 = b.reshape(L // 8, 8, d)
    sub = lax.broadcasted_iota(jnp.int32, b3.shape, 1)
    ref = None
    for p in range(8 // (2 * w)):
        row = jnp.broadcast_to(b3[:, p * 2 * w + off:p * 2 * w + off + 1, :], b3.shape)
        ref = row if ref is None else jnp.where(sub >= p * 2 * w, row, ref)
    return ref.reshape(L, d)


def _hgrn2_kernel(qp_ref, ffp_ref, fbp_ref, ip_ref, gp_ref, lb_ref, nw_ref,
                  trif_ref, trib_ref, mf_ref, mb_ref, o_ref,
                  of_scr, ob_scr, stf_scr, stb_scr, *, L, nl):
    S = qp_ref.shape[0]
    nc = S // L
    lb = lb_ref[...]
    lb_floor = jnp.maximum(lb, LB_FLOOR)
    one_m_lb = 1.0 - lb

    def chunk(c, fpre_ref, tri_ref, m_ref, st_scr, reverse):
        r0 = pl.multiple_of(c * L, L)
        q = _silu(qp_ref[pl.ds(r0, L), :])
        v = ip_ref[pl.ds(r0, L), :]
        f = lb_floor + one_m_lb * _sigmoid(fpre_ref[pl.ds(r0, L), :])
        k = 1.0 - f
        hi, mid, lo = _split3(jnp.log2(f))
        tri = tri_ref[...]
        b = _dot(tri, hi) + _dot(tri, mid) + _dot(tri, lo)
        qb = q.astype(BF16)
        kb = k.astype(BF16)
        a = m_ref[nl] * _dot_nt(qb, kb)
        w = L // 2
        for lvl in range(nl):
            e = jnp.exp2(-jnp.abs(b - _pair_ref(b, w, reverse))).astype(BF16)
            a = a + m_ref[lvl] * _dot_nt(qb * e, kb * e)
            w //= 2
        tot_row = b[0:1] if reverse else b[L - 1:L]
        q_in = (q * jnp.exp2(b)).astype(BF16)
        k_out = (k * jnp.exp2(tot_row - b)).astype(BF16)
        st = st_scr[...]
        o = _dot(a.astype(BF16), v.astype(BF16)) + _dot_nt(q_in, st.astype(BF16))
        st_scr[...] = st * jnp.exp2(tot_row) + _dot(v.T.astype(BF16), k_out)
        return r0, o

    stf_scr[...] = jnp.zeros_like(stf_scr)
    stb_scr[...] = jnp.zeros_like(stb_scr)

    def body(ci, carry):
        r0, o = chunk(ci, ffp_ref, trif_ref, mf_ref, stf_scr, False)
        of_scr[pl.ds(r0, L), :] = o
        r0, o = chunk(nc - 1 - ci, fbp_ref, trib_ref, mb_ref, stb_scr, True)
        ob_scr[pl.ds(r0, L), :] = o
        return carry

    lax.fori_loop(0, nc, body, 0, unroll=4)
    nw = nw_ref[...]
    R = 4 * L if S % (4 * L) == 0 else L

    def finish(i, carry):
        r0 = pl.multiple_of(i * R, R)
        o = of_scr[pl.ds(r0, R), :] + ob_scr[pl.ds(r0, R), :]
        o = o * lax.rsqrt(jnp.mean(o * o, axis=-1, keepdims=True) + RMS_EPS)
        o = o * nw * _sigmoid(gp_ref[pl.ds(r0, R), :])
        o_ref[pl.ds(r0, R), :] = o.astype(o_ref.dtype)
        return carry

    lax.fori_loop(0, S // R, finish, 0)


def hgrn2(proj, lb, norm_w, L=HG_CHUNK):
    B, S, _ = proj.shape
    m_f, m_b, nl = _hgrn2_masks(L)
    t = np.arange(L)
    tri_f = jnp.asarray((t[None, :] <= t[:, None]).astype(np.float32), BF16)
    tri_b = jnp.asarray((t[None, :] >= t[:, None]).astype(np.float32), BF16)
    d = HG_DIM
    nb = HG_WIDTH // d

    def col(off):
        return pl.BlockSpec((None, S, d), lambda b, h, off=off: (b, 0, off // d + h))

    vec = pl.BlockSpec((1, d), lambda b, h: (0, h))
    const2 = lambda a: pl.BlockSpec(a.shape, lambda b, h: (0, 0))
    const3 = lambda a: pl.BlockSpec(a.shape, lambda b, h: (0, 0, 0))
    return pl.pallas_call(
        functools.partial(_hgrn2_kernel, L=L, nl=nl),
        grid=(B, nb),
        in_specs=[col(OFF_Q), col(OFF_FF), col(OFF_FB), col(OFF_I), col(OFF_G), vec, vec,
                  const2(tri_f), const2(tri_b), const3(m_f), const3(m_b)],
        out_specs=pl.BlockSpec((None, S, d), lambda b, h: (b, 0, h)),
        out_shape=jax.ShapeDtypeStruct((B, S, HG_WIDTH), BF16),
        scratch_shapes=[pltpu.VMEM((S, d), F32), pltpu.VMEM((S, d), F32),
                        pltpu.VMEM((d, d), F32), pltpu.VMEM((d, d), F32)],
        compiler_params=_cparams("parallel", "parallel"),
        name="hgrn2",
    )(proj, proj, proj, proj, proj, lb.reshape(1, HG_WIDTH), norm_w.reshape(1, HG_WIDTH),
      tri_f, tri_b, jnp.asarray(m_f), jnp.asarray(m_b))


def _shift_rows(x, k):
    if k == 0:
        return x
    n = x.shape[0]
    rolled = pltpu.roll(x, (-k) % n, 0)
    t = lax.broadcasted_iota(jnp.int32, x.shape, 0)
    ok = (t + k >= 0) & (t + k < n)
    return jnp.where(ok, rolled, 0.0)


def _conv_silu(x, w, b):
    half = w.shape[0] // 2
    acc = b
    for j in range(w.shape[0]):
        acc = acc + w[j:j + 1, :] * _shift_rows(x, j - half)
    return _silu(acc)


def _ssd_kernel(x_ref, b_ref, c_ref, z_ref, dtc_ref, dtr_ref,
                cwx_ref, cwb_ref, cwc_ref, cbx_ref, cbb_ref, cbc_ref,
                dtbc_ref, dtbr_ref, alr_ref, alc_ref, dsk_ref, nw_ref,
                tril_ref, triu_ref,
                o_ref, xs_scr, bs_scr, cs_scr, yf_scr, yb_scr, stf_scr, stb_scr, *, L, hg, P):
    S = x_ref.shape[0]
    nc = S // L
    nh = 2 * hg
    gw = hg * P
    N = bs_scr.shape[1]

    a_row = jnp.exp(alr_ref[...]) * NEG_LOG2E
    a_col = jnp.exp(alc_ref[...]) * NEG_LOG2E
    tril = tril_ref[...]
    triu = triu_ref[...]
    ti = lax.broadcasted_iota(jnp.int32, (L, L), 0)
    si = lax.broadcasted_iota(jnp.int32, (L, L), 1)
    lane2 = lax.broadcasted_iota(jnp.int32, (L, 2 * P), 1)

    def softplus(v):
        return jnp.maximum(v, 0.0) + jnp.log(1.0 + jnp.exp(-jnp.abs(v)))

    def per_head(cols):
        tiles = []
        for j in range(0, hg, 2):
            lo = jnp.broadcast_to(cols[j], (L, 2 * P))
            hi = jnp.broadcast_to(cols[j + 1], (L, 2 * P))
            tiles.append(jnp.where(lane2 < P, lo, hi))
        return jnp.concatenate(tiles, axis=1)

    def chunk(c, d, g, st_scr):
        cum_c = tril if d == 0 else triu
        cum_r = triu if d == 0 else tril
        r0 = pl.multiple_of(c * L, L)
        x = xs_scr[pl.ds(r0, L), :]
        bm = bs_scr[pl.ds(r0, L), :]
        cm = cs_scr[pl.ds(r0, L), :]
        dt_c = softplus(dtc_ref[pl.ds(r0, L), :] + dtbc_ref[...])
        dt_r = softplus(dtr_ref[:, pl.ds(r0, L)] + dtbr_ref[...])
        h1, h2, h3 = _split3(dt_c * a_row)
        acum_c = _dot(cum_c, h1) + _dot(cum_c, h2) + _dot(cum_c, h3)
        g1, g2, g3 = _split3(dt_r * a_col)
        acum_r = _dot(g1, cum_r) + _dot(g2, cum_r) + _dot(g3, cum_r)
        ones_rows = jnp.ones((8, L), BF16)
        tot = (_dot(ones_rows, h1) + _dot(ones_rows, h2) + _dot(ones_rows, h3))[0:1]
        cb = _dot_nt(cm.astype(BF16), bm.astype(BF16))
        keep = (si <= ti) if d == 0 else (si >= ti)
        ys, in_cols, out_cols, dec_cols = [], [], [], []
        for hh in range(hg):
            j = d * nh + g * hg + hh
            ac = acum_c[:, j:j + 1]
            dc = dt_c[:, j:j + 1]
            te = tot[:, j:j + 1]
            ar = acum_r[j:j + 1, :]
            dr = dt_r[j:j + 1, :]
            seg = ac - ar
            decay = jnp.where(keep, jnp.exp2(jnp.where(keep, seg, 0.0)), 0.0)
            w = (cb * decay * dr).astype(BF16)
            ys.append(_dot(w, x[:, hh * P:(hh + 1) * P].astype(BF16)))
            in_cols.append(jnp.exp2(te - ac) * dc)
            out_cols.append(jnp.exp2(ac))
            dec_cols.append(jnp.broadcast_to(jnp.exp2(te), (1, P)))
        y_diag = jnp.concatenate(ys, axis=1)
        st = st_scr[...]
        y_off = _dot(cm.astype(BF16), st.astype(BF16)) * per_head(out_cols)
        x_in = (x * per_head(in_cols)).astype(BF16)
        st_scr[...] = st * jnp.concatenate(dec_cols, axis=1) + _dot(bm.T.astype(BF16), x_in)
        return r0, y_diag + y_off

    R = 2 * L
    for g in range(x_ref.shape[1] // gw):
        cols = slice(g * gw, (g + 1) * gw)
        ncols = slice(g * N, (g + 1) * N)
        xs_scr[...] = _conv_silu(x_ref[:, cols], cwx_ref[:, cols], cbx_ref[:, cols])
        bs_scr[...] = _conv_silu(b_ref[:, ncols], cwb_ref[:, ncols], cbb_ref[:, ncols])
        cs_scr[...] = _conv_silu(c_ref[:, ncols], cwc_ref[:, ncols], cbc_ref[:, ncols])
        stf_scr[...] = jnp.zeros_like(stf_scr)
        stb_scr[...] = jnp.zeros_like(stb_scr)

        def body(ci, carry, g=g):
            r0, y = chunk(ci, 0, g, stf_scr)
            yf_scr[pl.ds(r0, L), :] = y
            r0, y = chunk(nc - 1 - ci, 1, g, stb_scr)
            yb_scr[pl.ds(r0, L), :] = y
            return carry

        lax.fori_loop(0, nc, body, 0, unroll=2)

        def finish(i, carry, cols=cols):
            r0 = pl.multiple_of(i * R, R)
            y = (yf_scr[pl.ds(r0, R), :] + yb_scr[pl.ds(r0, R), :]
                 + dsk_ref[:, cols] * xs_scr[pl.ds(r0, R), :])
            y = y * _silu(z_ref[pl.ds(r0, R), cols])
            y = y * lax.rsqrt(jnp.mean(y * y, axis=-1, keepdims=True) + RMS_EPS)
            o_ref[pl.ds(r0, R), cols] = (y * nw_ref[:, cols]).astype(o_ref.dtype)
            return carry

        lax.fori_loop(0, S // R, finish, 0)


def ssd(proj, dt_rows, conv_w, conv_b, dt_bias, a_log, d_skip, norm_w, L=M2_CHUNK):
    B, S, _ = proj.shape
    hg = M2_HEADS // M2_GROUPS
    gw = hg * M2_P
    t = np.arange(L)
    tril = jnp.asarray((t[None, :] <= t[:, None]).astype(np.float32), BF16)
    triu = jnp.asarray((t[None, :] >= t[:, None]).astype(np.float32), BF16)
    nb_c = OFF_XBC + M2_INNER
    nc_c = nb_c + M2_GROUPS * M2_N
    xw = conv_w[:, :M2_INNER]
    bw = conv_w[:, M2_INNER:M2_INNER + M2_GROUPS * M2_N]
    cw = conv_w[:, M2_INNER + M2_GROUPS * M2_N:]
    cb2 = conv_b.reshape(1, -1)
    xb = cb2[:, :M2_INNER]
    bb = cb2[:, M2_INNER:M2_INNER + M2_GROUPS * M2_N]
    cbb = cb2[:, M2_INNER + M2_GROUPS * M2_N:]
    nh2 = 2 * M2_HEADS
    dtb_row = jnp.zeros((1, LANE), F32).at[0, :nh2].set(dt_bias.reshape(-1))
    al_row = jnp.zeros((1, LANE), F32).at[0, :nh2].set(a_log.reshape(-1))
    dtb_col = dt_bias.reshape(nh2, 1)
    al_col = a_log.reshape(nh2, 1)
    dsk = jnp.repeat(d_skip, M2_P).reshape(1, M2_INNER)
    nw = norm_w.reshape(1, M2_INNER)

    gn = M2_GROUPS * M2_N
    full2 = lambda a: pl.BlockSpec(a.shape, lambda b: (0, 0))
    return pl.pallas_call(
        functools.partial(_ssd_kernel, L=L, hg=hg, P=M2_P),
        grid=(B,),
        in_specs=[
            pl.BlockSpec((None, S, M2_INNER), lambda b: (b, 0, OFF_XBC // M2_INNER)),
            pl.BlockSpec((None, S, gn), lambda b: (b, 0, nb_c // gn)),
            pl.BlockSpec((None, S, gn), lambda b: (b, 0, nc_c // gn)),
            pl.BlockSpec((None, S, M2_INNER), lambda b: (b, 0, OFF_Z // M2_INNER)),
            pl.BlockSpec((None, S, LANE), lambda b: (b, 0, OFF_DT // LANE)),
            pl.BlockSpec((nh2, S), lambda b: (0, b)),
            full2(xw), full2(bw), full2(cw), full2(xb), full2(bb), full2(cbb),
            full2(dtb_row), full2(dtb_col), full2(al_row), full2(al_col),
            full2(dsk), full2(nw), full2(tril), full2(triu),
        ],
        out_specs=pl.BlockSpec((None, S, M2_INNER), lambda b: (b, 0, 0)),
        out_shape=jax.ShapeDtypeStruct((B, S, M2_INNER), BF16),
        scratch_shapes=[pltpu.VMEM((S, gw), F32), pltpu.VMEM((S, M2_N), F32),
                        pltpu.VMEM((S, M2_N), F32), pltpu.VMEM((S, gw), F32),
                        pltpu.VMEM((S, gw), F32), pltpu.VMEM((M2_N, gw), F32),
                        pltpu.VMEM((M2_N, gw), F32)],
        compiler_params=_cparams("parallel"),
        name="ssd",
    )(proj, proj, proj, proj, proj, dt_rows, xw, bw, cw, xb, bb, cbb,
      dtb_row, dtb_col, al_row, al_col, dsk, nw, tril, triu)


def _mix_out_kernel(h_ref, a_ref, b_ref, wa_ref, wb_ref, o_ref):
    o_ref[...] = h_ref[...] + _dot(a_ref[...], wa_ref[...]) + _dot(b_ref[...], wb_ref[...])


def mix_out(h, oa, ob, w, layer, tm=512):
    m, n = h.shape
    ka, kb = oa.shape[1], ob.shape[1]
    assert ka == kb
    return pl.pallas_call(
        _mix_out_kernel,
        grid=(m // tm,),
        in_specs=[pl.BlockSpec((tm, n), lambda i: (i, 0)),
                  pl.BlockSpec((tm, ka), lambda i: (i, 0)),
                  pl.BlockSpec((tm, kb), lambda i: (i, 0)),
                  pl.BlockSpec((None, ka, n), lambda i: (layer, 0, 0)),
                  pl.BlockSpec((None, kb, n), lambda i: (layer, 1, 0))],
        out_specs=pl.BlockSpec((tm, n), lambda i: (i, 0)),
        out_shape=jax.ShapeDtypeStruct((m, n), F32),
        compiler_params=_cparams("parallel"),
        name="mix_out",
    )(h, oa, ob, w, w)


def _ffn_kernel(h_ref, nw_ref, wg_ref, wu_ref, wd_ref, o_ref, xn_scr, acc_scr):
    f = pl.program_id(1)

    @pl.when(f == 0)
    def _():
        x = h_ref[...]
        xn_scr[...] = _rms(x, nw_ref[...]).astype(BF16)
        acc_scr[...] = x

    xn = xn_scr[...]
    act = _silu(_dot(xn, wg_ref[...])) * _dot(xn, wu_ref[...])
    acc_scr[...] += _dot(act.astype(BF16), wd_ref[...])

    @pl.when(f == pl.num_programs(1) - 1)
    def _():
        o_ref[...] = acc_scr[...]


def ffn(h, nw, w_gu, w_down, layer, tm=1024, tf=512):
    m, d = h.shape
    dff = w_down.shape[1]
    nf = dff // tf
    return pl.pallas_call(
        _ffn_kernel,
        grid=(m // tm, nf),
        in_specs=[pl.BlockSpec((tm, d), lambda i, f: (i, 0)),
                  pl.BlockSpec((1, d), lambda i, f: (0, 0)),
                  pl.BlockSpec((None, d, tf), lambda i, f: (layer, 0, f)),
                  pl.BlockSpec((None, d, tf), lambda i, f: (layer, 0, nf + f)),
                  pl.BlockSpec((None, tf, d), lambda i, f: (layer, f, 0))],
        out_specs=pl.BlockSpec((tm, d), lambda i, f: (i, 0)),
        out_shape=jax.ShapeDtypeStruct((m, d), F32),
        scratch_shapes=[pltpu.VMEM((tm, d), BF16), pltpu.VMEM((tm, d), F32)],
        compiler_params=_cparams("parallel", "arbitrary"),
        name="ffn",
    )(h, nw.reshape(1, d), w_gu, w_gu, w_down)


def _ple_kernel(*refs, moe, final):
    if moe:
        h_ref, y2_ref, meta_ref, p_ref, nw_ref, wg_ref, wp_ref, fw_ref, o_ref = refs
        meta = meta_ref[...]
        d = h_ref.shape[1]
        h = h_ref[...] + (meta[:, 2:3] * y2_ref[:, 0:d] + meta[:, 3:4] * y2_ref[:, d:2 * d])
    else:
        h_ref, p_ref, nw_ref, wg_ref, wp_ref, fw_ref, o_ref = refs
        h = h_ref[...]
    gate = _sigmoid(_dot(_rms(h, nw_ref[...]).astype(BF16), wg_ref[...]))
    h = h + gate * _dot(p_ref[...].astype(BF16), wp_ref[...])
    if final:
        h = _rms(h, fw_ref[...])
    o_ref[...] = h


def ple(h, p, nw, wg, wp, fw, layer, y2=None, meta=None, final=False, tm=512):
    m, d = h.shape
    moe = y2 is not None
    row = lambda w: pl.BlockSpec((tm, w), lambda i: (i, 0))
    slab = lambda a: pl.BlockSpec((None,) + a.shape[1:], lambda i: (layer, 0, 0))
    vec = pl.BlockSpec((1, d), lambda i: (0, 0))
    in_specs = [row(d)]
    args = [h]
    if moe:
        in_specs += [row(2 * d), row(meta.shape[1])]
        args += [y2, meta]
    in_specs += [pl.BlockSpec((None, tm, p.shape[2]), lambda i: (layer, i, 0)), vec, slab(wg), slab(wp), vec]
    args += [p, nw.reshape(1, d), wg, wp, fw.reshape(1, d)]
    return pl.pallas_call(
        functools.partial(_ple_kernel, moe=moe, final=final),
        grid=(m // tm,),
        in_specs=in_specs,
        out_specs=row(d),
        out_shape=jax.ShapeDtypeStruct((m, d), F32),
        compiler_params=_cparams("parallel"),
        name="ple",
    )(*args)


def _s5_tables(a_re, a_im, log_step, b_re, b_im, c_re, c_im):
    G, N = a_re.shape[1:]
    C = S5_GROUP_SIZE
    T = S5_STEPS
    gt = LANE // C
    Z = G // gt
    tau = jnp.arange(T + 1, dtype=F32)
    steps = jnp.arange(T)

    def cmul(xr, xi, yr, yi):
        return xr * yr - xi * yi, xr * yi + xi * yr

    st_parts, rd_parts, k_parts, lam_rows = [], [], [], []
    for d in range(2):
        delta = jnp.exp(log_step[d])[:, None]
        ar, ai = a_re[d], a_im[d]
        mag = jnp.exp(ar * delta)
        lam_re, lam_im = mag * jnp.cos(ai * delta), mag * jnp.sin(ai * delta)
        den = ar * ar + ai * ai
        num_re = lam_re - 1.0
        coef_re = (num_re * ar + lam_im * ai) / den
        coef_im = (lam_im * ar - num_re * ai) / den
        br = coef_re[..., None] * b_re - coef_im[..., None] * b_im
        bi = coef_re[..., None] * b_im + coef_im[..., None] * b_re
        cr, ci = c_re[d], c_im[d]
        pm = jnp.exp((ar * delta)[None] * tau[:, None, None])
        ang = (ai * delta)[None] * tau[:, None, None]
        pr, pi = pm * jnp.cos(ang), pm * jnp.sin(ang)
        e_in = (T - 1 - steps) if d == 0 else steps
        sr, si = cmul(pr[e_in][..., None], pi[e_in][..., None], br[None], bi[None])
        e_out = (steps + 1) if d == 0 else (T - steps)
        cpr, cpi = cmul(cr[None], ci[None], pr[e_out][:, :, None, :], pi[e_out][:, :, None, :])
        lbr, lbi = cmul(pr[:T][..., None], pi[:T][..., None], br[None], bi[None])
        ktau = jnp.einsum('gon,tgni->tgoi', cr, lbr) - jnp.einsum('gon,tgni->tgoi', ci, lbi)
        st_parts += [sr, si]
        rd_parts += [cpr, -cpi]
        k_parts.append(ktau)
        lam_rows += [pr[T], pi[T]]

    NS = gt * N
    st = jnp.stack(st_parts).reshape(4, T, Z, gt, N, C).transpose(2, 1, 5, 0, 3, 4)
    wst = block_diag_rows(st.reshape(Z, T * C, 4 * NS), C, N)
    rd = jnp.stack(rd_parts).reshape(4, T, Z, gt, C, N).transpose(2, 1, 4, 0, 3, 5)
    wc = block_diag_rows(rd.reshape(Z, T * C, 4 * NS), C, N)
    lag = steps[None, :] - steps[:, None]
    sel_f = (lag[:, :, None] == steps[None, None, :]).astype(F32)
    sel_b = (-lag[:, :, None] == steps[None, None, :]).astype(F32)
    toe = (jnp.einsum('stk,kgoi->stgoi', sel_f, k_parts[0])
           + jnp.einsum('stk,kgoi->stgoi', sel_b, k_parts[1]))
    toe = toe.reshape(T, T, Z, gt, C, C).transpose(2, 0, 5, 1, 3, 4)
    kin = block_diag_rows(toe.reshape(Z, T * C, T * LANE), C, C)
    lam = jnp.stack([r.reshape(Z, NS) for r in lam_rows], axis=1)
    lam = jnp.concatenate([lam, jnp.zeros_like(lam)], axis=1)
    return wst, kin, wc, lam


def _block_diag_kernel(s_ref, o_ref, *, C, gcol):
    n = o_ref.shape[1]
    gt = LANE // C
    row_g = lax.broadcasted_iota(jnp.int32, (LANE, n), 0) // C
    col_g = (lax.broadcasted_iota(jnp.int32, (LANE, n), 1) // gcol) % gt
    keep = row_g == col_g
    for t in range(o_ref.shape[0] // LANE):
        strip = s_ref[t * C:(t + 1) * C, :]
        tiled = jnp.broadcast_to(strip[None], (gt, C, n)).reshape(LANE, n)
        o_ref[t * LANE:(t + 1) * LANE, :] = jnp.where(keep, tiled, 0.0).astype(o_ref.dtype)


def block_diag_rows(strips, C, gcol):
    Z, rows, n = strips.shape
    gt = LANE // C
    return pl.pallas_call(
        functools.partial(_block_diag_kernel, C=C, gcol=gcol),
        grid=(Z,),
        in_specs=[pl.BlockSpec((None, rows, n), lambda z: (z, 0, 0))],
        out_specs=pl.BlockSpec((None, rows * gt, n), lambda z: (z, 0, 0)),
        out_shape=jax.ShapeDtypeStruct((Z, rows * gt, n), BF16),
        compiler_params=_cparams("parallel"),
        name="block_diag_rows",
    )(strips)


def _s5_pre_kernel(x_ref, nw_ref, u_ref, xn_scr, tmp_scr):
    nb, R, D = x_ref.shape
    T = S5_STEPS
    rc = R // T
    xn = _rms(x_ref[...], nw_ref[...])
    for z in range(D // LANE):
        xn_scr[z] = xn[:, :, z * LANE:(z + 1) * LANE]
    for z in range(D // LANE):
        for t in range(T):
            for b in range(nb):
                tmp_scr[z, t, pl.ds(b, rc, stride=nb), :] = xn_scr[z, b, pl.ds(t, rc, stride=T), :]
            u_ref[z, :, t * LANE:(t + 1) * LANE] = tmp_scr[z, t].astype(u_ref.dtype)


def s5_pre(h3, nw, R=64):
    B, S, D = h3.shape
    T = S5_STEPS
    Z = D // LANE
    rc = R // T
    return pl.pallas_call(
        _s5_pre_kernel,
        grid=(S // R,),
        in_specs=[pl.BlockSpec((B, R, D), lambda i: (0, i, 0)),
                  pl.BlockSpec((1, D), lambda i: (0, 0))],
        out_specs=pl.BlockSpec((Z, rc * B, T * LANE), lambda i: (0, i, 0)),
        out_shape=jax.ShapeDtypeStruct((Z, S // T * B, T * LANE), BF16),
        scratch_shapes=[pltpu.VMEM((Z, B, R, LANE), F32), pltpu.VMEM((Z, T, rc * B, LANE), F32)],
        compiler_params=_cparams("parallel"),
        name="s5_pre",
    )(h3, nw.reshape(1, D))


def _s5_kernel(u_ref, wst_ref, kin_ref, wc_ref, lam_ref, y_ref, s_scr, *, nb, rblk):
    rows = u_ref.shape[0]
    nc = rows // nb
    ns = lam_ref.shape[1]
    for r0 in range(0, rows, rblk):
        s_scr[r0:r0 + rblk, :] = _dot(u_ref[r0:r0 + rblk, :], wst_ref[...])
    lam = lam_ref[...]
    lfr, lfi, lbr, lbi = lam[0:1], lam[1:2], lam[2:3], lam[3:4]

    def body(ci, carry):
        hfr, hfi, hbr, hbi = carry
        rf = pl.multiple_of(ci * nb, nb)
        rb = pl.multiple_of((nc - 1 - ci) * nb, nb)
        sfr = s_scr[pl.ds(rf, nb), 0:ns]
        sfi = s_scr[pl.ds(rf, nb), ns:2 * ns]
        sbr = s_scr[pl.ds(rb, nb), 2 * ns:3 * ns]
        sbi = s_scr[pl.ds(rb, nb), 3 * ns:4 * ns]
        s_scr[pl.ds(rf, nb), 0:ns] = hfr
        s_scr[pl.ds(rf, nb), ns:2 * ns] = hfi
        s_scr[pl.ds(rb, nb), 2 * ns:3 * ns] = hbr
        s_scr[pl.ds(rb, nb), 3 * ns:4 * ns] = hbi
        return (lfr * hfr - lfi * hfi + sfr, lfr * hfi + lfi * hfr + sfi,
                lbr * hbr - lbi * hbi + sbr, lbr * hbi + lbi * hbr + sbi)

    z = jnp.zeros((nb, ns), F32)
    lax.fori_loop(0, nc, body, (z, z, z, z))
    for r0 in range(0, rows, rblk):
        y = (_dot(u_ref[r0:r0 + rblk, :], kin_ref[...])
             + _dot_nt(s_scr[r0:r0 + rblk, :].astype(BF16), wc_ref[...]))
        y_ref[r0:r0 + rblk, :] = y.astype(y_ref.dtype)


def s5_scan(u, wst, kin, wc, lam, nb, rblk=512):
    Z, rows, K = u.shape
    per = lambda a: pl.BlockSpec((None,) + a.shape[1:], lambda s: (s, 0, 0))
    once = lambda a: pl.BlockSpec((None,) + a.shape[1:], lambda s: (s, 0, 0),
                                  pipeline_mode=pl.Buffered(1))
    return pl.pallas_call(
        functools.partial(_s5_kernel, nb=nb, rblk=rblk),
        grid=(Z,),
        in_specs=[per(u), once(wst), once(kin), once(wc), per(lam)],
        out_specs=pl.BlockSpec((None, rows, K), lambda s: (s, 0, 0)),
        out_shape=jax.ShapeDtypeStruct((Z, rows, K), BF16),
        scratch_shapes=[pltpu.VMEM((rows, wst.shape[2]), F32)],
        compiler_params=_cparams("parallel"),
        name="s5_scan",
    )(u, wst, kin, wc, lam)


def _s5_post_kernel(h_ref, y_ref, nw_ref, d_ref, wo_ref, wg_ref, o_ref, yf_scr, yt_scr):
    nb, R, D = h_ref.shape
    T = S5_STEPS
    rc = R // T
    for z in range(D // LANE):
        for t in range(T):
            yf_scr[z, t] = y_ref[z, :, t * LANE:(t + 1) * LANE].astype(F32)
            for b in range(nb):
                yt_scr[z, b, pl.ds(t, rc, stride=T), :] = yf_scr[z, t, pl.ds(b, rc, stride=nb), :]
    yt = jnp.concatenate([yt_scr[z] for z in range(D // LANE)], axis=-1)
    h = h_ref[...].reshape(nb * R, D)
    y = yt.reshape(nb * R, D) + d_ref[...] * _rms(h, nw_ref[...])
    act = jax.nn.gelu(y).astype(BF16)
    out = h + _dot(act, wo_ref[...]) * _sigmoid(_dot(act, wg_ref[...]))
    o_ref[...] = out.reshape(nb, R, D)


def s5_post(h3, y, nw, d_skip, glu_w, layer, R=64):
    B, S, D = h3.shape
    T = S5_STEPS
    Z = D // LANE
    rc = R // T
    row = pl.BlockSpec((B, R, D), lambda i: (0, i, 0))
    vec = pl.BlockSpec((1, D), lambda i: (0, 0))
    return pl.pallas_call(
        _s5_post_kernel,
        grid=(S // R,),
        in_specs=[row, pl.BlockSpec((Z, rc * B, T * LANE), lambda i: (0, i, 0)), vec, vec,
                  pl.BlockSpec((None, D, D), lambda i: (layer, 0, 0)),
                  pl.BlockSpec((None, D, D), lambda i: (layer, 0, 1))],
        out_specs=row,
        out_shape=jax.ShapeDtypeStruct((B, S, D), F32),
        scratch_shapes=[pltpu.VMEM((Z, T, rc * B, LANE), F32), pltpu.VMEM((Z, B, R, LANE), F32)],
        compiler_params=_cparams("parallel"),
        name="s5_post",
    )(h3, y, nw.reshape(1, D), d_skip.reshape(1, D), glu_w, glu_w)


def _router_kernel(h_ref, nw_ref, wr_ref, tri_ref, meta_ref, cnt_ref, carry_scr):
    @pl.when(pl.program_id(0) == 0)
    def _():
        carry_scr[...] = jnp.zeros_like(carry_scr)

    xn = _rms(h_ref[...], nw_ref[...])
    logits = jnp.dot(xn, wr_ref[...], preferred_element_type=F32, precision=lax.Precision.HIGHEST)
    lane = lax.broadcasted_iota(jnp.int32, logits.shape, 1)
    neg = jnp.float32(-jnp.inf)
    lg = jnp.where(lane < N_EXPERTS, logits, neg)
    t1 = jnp.max(lg, axis=1, keepdims=True)
    i1 = jnp.min(jnp.where(lg == t1, lane, LANE), axis=1, keepdims=True)
    lg2 = jnp.where(lane == i1, neg, lg)
    t2 = jnp.max(lg2, axis=1, keepdims=True)
    i2 = jnp.min(jnp.where(lg2 == t2, lane, LANE), axis=1, keepdims=True)
    ex = jnp.exp(t2 - t1)
    g1 = 1.0 / (1.0 + ex)
    g2 = ex / (1.0 + ex)
    oh1 = jnp.where(lane == i1, 1.0, 0.0)
    oh2 = jnp.where(lane == i2, 1.0, 0.0)
    tri = tri_ref[...]
    before1 = _dot(tri, oh1.astype(BF16))
    before2 = _dot(tri, oh2.astype(BF16))
    tot1 = jnp.sum(oh1, axis=0, keepdims=True)
    tot2 = jnp.sum(oh2, axis=0, keepdims=True)
    carry = carry_scr[0:1]
    rank1 = jnp.sum(oh1 * (carry + before1), axis=1, keepdims=True)
    rank2 = jnp.sum(oh2 * (carry + tot1 + before2), axis=1, keepdims=True)
    counts = jnp.broadcast_to(carry + tot1 + tot2, carry_scr.shape)
    carry_scr[...] = counts
    cnt_ref[...] = counts
    meta = jnp.where(lane == 0, i1.astype(F32), 0.0)
    meta = jnp.where(lane == 1, i2.astype(F32), meta)
    meta = jnp.where(lane == 2, g1, meta)
    meta = jnp.where(lane == 3, g2, meta)
    meta = jnp.where(lane == 4, rank1, meta)
    meta = jnp.where(lane == 5, rank2, meta)
    meta_ref[...] = meta


def router(h, nw, w_router, tm=512):
    m, d = h.shape
    wr = jnp.zeros((d, LANE), F32).at[:, :N_EXPERTS].set(w_router)
    t = np.arange(tm)
    tri = jnp.asarray((t[None, :] < t[:, None]).astype(np.float32), BF16)
    return pl.pallas_call(
        _router_kernel,
        grid=(m // tm,),
        in_specs=[pl.BlockSpec((tm, d), lambda i: (i, 0)),
                  pl.BlockSpec((1, d), lambda i: (0, 0)),
                  pl.BlockSpec((d, LANE), lambda i: (0, 0)),
                  pl.BlockSpec((tm, tm), lambda i: (0, 0))],
        out_specs=[pl.BlockSpec((tm, LANE), lambda i: (i, 0)),
                   pl.BlockSpec((8, LANE), lambda i: (0, 0))],
        out_shape=[jax.ShapeDtypeStruct((m, LANE), F32), jax.ShapeDtypeStruct((8, LANE), F32)],
        scratch_shapes=[pltpu.VMEM((8, LANE), F32)],
        compiler_params=_cparams("arbitrary"),
        name="router",
    )(h, nw.reshape(1, d), wr, tri)


def _moe_kernel(arow_ref, blk_e_ref, nvalid_ref, h_hbm, nw_ref, wg_ref, wu_ref, wd_ref,
                y2_hbm, xg_scr, xn_scr, acc_scr, sem_in, sem_out, *, bm, n_tok):
    g = pl.program_id(0)
    f = pl.program_id(1)
    nf = pl.num_programs(1)
    slot = g % 2
    other = 1 - slot
    d = acc_scr.shape[2]
    rows_per_step = bm // (nf - 1)
    prv, cur, nxt = g * bm, (g + 1) * bm, (g + 2) * bm
    nv = nvalid_ref[g]

    def in_copy(off, r, s):
        tok = jnp.minimum(arow_ref[off + r] >> 1, n_tok - 1)
        return pltpu.make_async_copy(h_hbm.at[pl.ds(tok, 1)], xg_scr.at[s, pl.ds(r, 1)], sem_in.at[s])

    def out_copy(off, r, s):
        a = arow_ref[off + r]
        col = pl.multiple_of((a & 1) * d, d)
        return pltpu.make_async_copy(acc_scr.at[s, pl.ds(r, 1)],
                                     y2_hbm.at[pl.ds(a >> 1, 1), pl.ds(col, d)], sem_out.at[s])

    def wait_in(s):
        pltpu.make_async_copy(h_hbm.at[pl.ds(0, bm)], xg_scr.at[s], sem_in.at[s]).wait()

    def wait_out(s):
        pltpu.make_async_copy(acc_scr.at[s], y2_hbm.at[pl.ds(0, bm), pl.ds(0, d)], sem_out.at[s]).wait()

    @pl.when((g == 0) & (f == 0))
    def _():
        acc_scr[...] = jnp.zeros_like(acc_scr)

        def start(r, c):
            in_copy(cur, r, slot).start()
            return c

        lax.fori_loop(0, bm, start, 0)

    @pl.when(f == 0)
    def _():
        wait_in(slot)
        xn_scr[...] = _rms(xg_scr[slot], nw_ref[...]).astype(BF16)
        acc_scr[slot] = jnp.zeros((bm, d), F32)

    def move_rows(part, parts):
        base = pl.multiple_of(f * rows_per_step, 8)
        per = rows_per_step // parts
        for j in range(part * per, (part + 1) * per):
            in_copy(nxt, base + j, other).start()
            out_copy(prv, base + j, other).start()

    def experts(with_moves):
        parts = 2
        tc = wg_ref.shape[2] // parts
        xn = xn_scr[...]
        upd = None
        for c in range(parts):
            cs = slice(c * tc, (c + 1) * tc)
            act = _silu(_dot(xn, wg_ref[0, :, cs])) * _dot(xn, wu_ref[0, :, cs])
            part = _dot(act.astype(BF16), wd_ref[0, cs, :])
            upd = part if upd is None else upd + part
            if with_moves:
                move_rows(c, parts)
        acc_scr[slot] += upd

    @pl.when((nv > 0) & (f < nf - 1))
    def _():
        experts(True)

    @pl.when((nv > 0) & (f == nf - 1))
    def _():
        experts(False)

    @pl.when((nv == 0) & (f < nf - 1))
    def _():
        move_rows(0, 1)

    @pl.when(f == nf - 1)
    def _():
        wait_out(other)

    @pl.when((f == nf - 1) & (g == pl.num_programs(0) - 1))
    def _():
        wait_in(other)


def moe_experts(h, nw, w_gu, w_down, arow, blk_e, nvalid, layer, bm=MOE_BM, tf=512):
    t, d = h.shape
    dff = w_down.shape[1]
    nf = dff // tf
    assert bm % (2 * (nf - 1)) == 0 and bm % 16 == 0
    n_steps = blk_e.shape[0]
    e0 = layer * N_EXPERTS

    def wmap(col0):
        def index(g, f, ar, be, nv):
            return (e0 + be[g], 0, col0 + jnp.where(nv[g] > 0, f, nf - 1))
        return index

    def dmap(g, f, ar, be, nv):
        return (e0 + be[g], jnp.where(nv[g] > 0, f, nf - 1), 0)

    grid_spec = pltpu.PrefetchScalarGridSpec(
        num_scalar_prefetch=3,
        grid=(n_steps, nf),
        in_specs=[pl.BlockSpec(memory_space=pl.ANY),
                  pl.BlockSpec((1, d), lambda g, f, ar, be, nv: (0, 0)),
                  pl.BlockSpec((1, d, tf), wmap(0)),
                  pl.BlockSpec((1, d, tf), wmap(nf)),
                  pl.BlockSpec((1, tf, d), dmap)],
        out_specs=pl.BlockSpec(memory_space=pl.ANY),
        scratch_shapes=[pltpu.VMEM((2, bm, d), F32), pltpu.VMEM((bm, d), BF16),
                        pltpu.VMEM((2, bm, d), F32),
                        pltpu.SemaphoreType.DMA((2,)), pltpu.SemaphoreType.DMA((2,))],
    )
    return pl.pallas_call(
        functools.partial(_moe_kernel, bm=bm, n_tok=t),
        grid_spec=grid_spec,
        out_shape=jax.ShapeDtypeStruct((t + bm // 2, 2 * d), F32),
        compiler_params=_cparams("arbitrary", "arbitrary"),
        name="moe_experts",
    )(arow, blk_e, nvalid, h, nw.reshape(1, d), w_gu, w_gu, w_down)


def _moe_plan(meta, counts, bm):
    t = meta.shape[0]
    n_assign = 2 * t
    experts = jnp.arange(N_EXPERTS, dtype=jnp.int32)
    cnt = counts[0, :N_EXPERTS].astype(jnp.int32)
    padded = (cnt + bm - 1) // bm * bm
    pend = jnp.cumsum(padded)
    pstart = pend - padded
    e = meta[:, 0:2].astype(jnp.int32)
    rank = meta[:, 4:6].astype(jnp.int32)
    dest = jnp.sum(jnp.where(e[..., None] == experts, pstart, 0), axis=-1) + rank
    n_blocks = -(-n_assign // bm) + N_EXPERTS
    spare = 2 * t + jnp.arange(bm, dtype=jnp.int32)
    arow = jnp.tile(spare, n_blocks).at[dest.reshape(-1)].set(jnp.arange(n_assign, dtype=jnp.int32))
    arow = jnp.concatenate([spare, arow, spare, spare])
    blk_start = jnp.arange(n_blocks, dtype=jnp.int32) * bm
    blk_e = jnp.minimum(jnp.sum((pend[None, :] <= blk_start[:, None]).astype(jnp.int32), axis=1),
                        N_EXPERTS - 1)
    nvalid = jnp.clip((pstart + cnt)[blk_e] - blk_start, 0, bm)
    nvalid = jnp.where(blk_start < pend[-1], nvalid, 0).astype(jnp.int32)
    blk_e = jnp.concatenate([blk_e, blk_e[-1:]]).astype(jnp.int32)
    nvalid = jnp.concatenate([nvalid, jnp.zeros((1,), jnp.int32)])
    return arow, blk_e, nvalid


def kernel(x, p, norm_mix, norm_ffn, norm_ple, final_norm, ple_gate, ple_proj, ev_w_in, ev_w_out,
           hg_lb_logits, hg_norm_w, m2_conv_w, m2_conv_b, m2_dt_bias, m2_a_log, m2_d, m2_norm_w,
           s5_a_re, s5_a_im, s5_log_step, s5_b_re, s5_b_im, s5_c_re, s5_c_im, s5_d, s5_glu_w,
           ffn_w_gu, ffn_w_down, moe_router, moe_w_gu, moe_w_down):
    B, S, D = x.shape
    T = B * S
    depth = norm_mix.shape[0]
    lb_soft = jax.nn.softmax(hg_lb_logits.astype(F32), axis=0)
    hg_lb = jnp.cumsum(lb_soft, axis=0) - lb_soft[0]
    w_in = jnp.pad(ev_w_in, ((0, 0), (0, 0), (0, EVEN_IN_PAD - ev_w_in.shape[2]))).astype(BF16)
    w_dt_t = jnp.swapaxes(ev_w_in[:, :, OFF_DT:], 1, 2).astype(BF16)
    w_out = ev_w_out.astype(BF16)
    w_ffn_gu, w_ffn_down = ffn_w_gu.astype(BF16), ffn_w_down.astype(BF16)
    w_moe_gu = moe_w_gu.astype(BF16).reshape((-1,) + moe_w_gu.shape[2:])
    w_moe_down = moe_w_down.astype(BF16).reshape((-1,) + moe_w_down.shape[2:])
    w_glu = s5_glu_w.astype(BF16)
    w_ple_gate, w_ple_proj = ple_gate.astype(BF16), ple_proj.astype(BF16)
    p_rows = p.reshape(depth, T, p.shape[-1])
    h = x.reshape(T, D)
    for layer in range(depth):
        j = layer // 2
        if layer % 2 == 0:
            proj, dt_rows = rms_matmul(h, norm_mix[layer], w_in, w_dt_t, j)
            proj = proj.reshape(B, S, EVEN_IN_PAD)
            o_a = hgrn2(proj, hg_lb[j], hg_norm_w[j])
            o_b = ssd(proj, dt_rows, m2_conv_w[j], m2_conv_b[j], m2_dt_bias[j], m2_a_log[j],
                      m2_d[j], m2_norm_w[j])
            h = mix_out(h, o_a.reshape(T, HG_WIDTH), o_b.reshape(T, M2_INNER), w_out, j)
            h = ffn(h, norm_ffn[layer], w_ffn_gu, w_ffn_down, j)
            y2 = meta = None
        else:
            h3 = h.reshape(B, S, D)
            wst, kin, wc, lam = _s5_tables(s5_a_re[j], s5_a_im[j], s5_log_step[j], s5_b_re[j],
                                           s5_b_im[j], s5_c_re[j], s5_c_im[j])
            y = s5_scan(s5_pre(h3, norm_mix[layer]), wst, kin, wc, lam, B)
            h = s5_post(h3, y, norm_mix[layer], s5_d[j], w_glu, j).reshape(T, D)
            meta, counts = router(h, norm_ffn[layer], moe_router[j])
            arow, blk_e, nvalid = _moe_plan(meta, counts, MOE_BM)
            y2 = moe_experts(h, norm_ffn[layer], w_moe_gu, w_moe_down, arow, blk_e, nvalid, j)
        h = ple(h, p_rows, norm_ple[layer], w_ple_gate, w_ple_proj, final_norm, layer,
                y2=y2, meta=meta, final=(layer == depth - 1))
    return h.reshape(B, S, D)
```

```python
import functools

import numpy as np
import jax
import jax.numpy as jnp
from jax import lax
from jax.experimental import pallas as pl
from jax.experimental.pallas import tpu as pltpu

F32 = jnp.float32
BF16 = jnp.bfloat16

RMS_EPS = 1e-6
LB_FLOOR = 1e-30
NEG_LOG2E = -1.4426950408889634
LANE = 128
VMEM_LIMIT = 56 * 1024 * 1024

HG_HEADS = 4
HG_DIM = 128
HG_WIDTH = HG_HEADS * HG_DIM
HG_CHUNK = 128
M2_HEADS = 8
M2_P = 64
M2_INNER = M2_HEADS * M2_P
M2_GROUPS = 2
M2_N = 128
M2_CHUNK = 128
M2_XBC = M2_INNER + 2 * M2_GROUPS * M2_N
S5_GROUP_SIZE = 16
S5_STATE = 64
S5_STEPS = 8
N_EXPERTS = 8
MOE_BM = 576

OFF_Q, OFF_FF, OFF_FB, OFF_I, OFF_G = (k * HG_WIDTH for k in range(5))
OFF_Z = 5 * HG_WIDTH
OFF_XBC = OFF_Z + M2_INNER
OFF_DT = OFF_XBC + M2_XBC
EVEN_IN_PAD = OFF_DT + LANE


def _cparams(*sem):
    return pltpu.CompilerParams(dimension_semantics=sem, vmem_limit_bytes=VMEM_LIMIT)


def _rms(x, w):
    return x * lax.rsqrt(jnp.mean(x * x, axis=-1, keepdims=True) + RMS_EPS) * w


def _sigmoid(x):
    return 1.0 / (1.0 + jnp.exp2(x * NEG_LOG2E))


def _silu(x):
    return x * _sigmoid(x)


def _dot(a, b):
    return jnp.dot(a, b, preferred_element_type=F32)


def _dot_nt(a, b):
    return lax.dot_general(a, b, (((1,), (1,)), ((), ())), preferred_element_type=F32)


def _split3(x):
    hi = x.astype(BF16)
    r1 = x - hi.astype(F32)
    mid = r1.astype(BF16)
    lo = (r1 - mid.astype(F32)).astype(BF16)
    return hi, mid, lo


def _rms_matmul_kernel(x_ref, nw_ref, w_ref, wdt_ref, o_ref, dtr_ref):
    xn = _rms(x_ref[...], nw_ref[...]).astype(BF16)
    o_ref[...] = _dot(xn, w_ref[...])
    dtr_ref[...] = _dot_nt(wdt_ref[...], xn)


def rms_matmul(x, nw, w, w_dt_t, layer, tm=256):
    m, k = x.shape
    n = w.shape[2]
    nr = w_dt_t.shape[1]
    return pl.pallas_call(
        _rms_matmul_kernel,
        grid=(m // tm,),
        in_specs=[pl.BlockSpec((tm, k), lambda i: (i, 0)),
                  pl.BlockSpec((1, k), lambda i: (0, 0)),
                  pl.BlockSpec((None, k, n), lambda i: (layer, 0, 0)),
                  pl.BlockSpec((None, nr, k), lambda i: (layer, 0, 0))],
        out_specs=[pl.BlockSpec((tm, n), lambda i: (i, 0)),
                   pl.BlockSpec((nr, tm), lambda i: (0, i))],
        out_shape=[jax.ShapeDtypeStruct((m, n), F32), jax.ShapeDtypeStruct((nr, m), F32)],
        compiler_params=_cparams("parallel"),
        name="rms_matmul",
    )(x, nw.reshape(1, k), w, w_dt_t)


def _hgrn2_masks(L):
    t = np.arange(L)
    masks = []
    w = L // 2
    while w >= 1:
        blk = t // w
        masks.append((blk % 2 == 1)[:, None] & (blk[None, :] == blk[:, None] - 1))
        w //= 2
    masks.append(np.eye(L, dtype=bool))
    m_f = np.stack(masks).astype(np.float32)
    return m_f, np.ascontiguousarray(m_f[:, ::-1, ::-1]), len(masks) - 1


def _pair_ref(b, w, reverse):
    L, d = b.shape
    off = w if reverse else w - 1
    if 2 * w > 8:
        pieces = [jnp.broadcast_to(b[p * 2 * w + off:p * 2 * w + off + 1, :], (2 * w, d))
                  for p in range(L // (2 * w))]
        return pieces[0] if len(pieces) == 1 else jnp.concatenate(pieces, axis=0)
    b3 = b.reshape(L // 8, 8, d)
    sub = lax.broadcasted_iota(jnp.int32, b3.shape, 1)
    ref = None
    for p in range(8 // (2 * w)):
        row = jnp.broadcast_to(b3[:, p * 2 * w + off:p * 2 * w + off + 1, :], b3.shape)
        ref = row if ref is None else jnp.where(sub >= p * 2 * w, row, ref)
    return ref.reshape(L, d)


def _hgrn2_kernel(qp_ref, ffp_ref, fbp_ref, ip_ref, gp_ref, lb_ref, nw_ref,
                  trif_ref, trib_ref, mf_ref, mb_ref, o_ref,
                  of_scr, ob_scr, stf_scr, stb_scr, *, L, nl):
    S = qp_ref.shape[0]
    nc = S // L
    lb = lb_ref[...]
    lb_floor = jnp.maximum(lb, LB_FLOOR)
    one_m_lb = 1.0 - lb

    def chunk(c, fpre_ref, tri_ref, m_ref, st_scr, reverse):
        r0 = pl.multiple_of(c * L, L)
        q = _silu(qp_ref[pl.ds(r0, L), :])
        v = ip_ref[pl.ds(r0, L), :]
        f = lb_floor + one_m_lb * _sigmoid(fpre_ref[pl.ds(r0, L), :])
        k = 1.0 - f
        hi, mid, lo = _split3(jnp.log2(f))
        tri = tri_ref[...]
        b = _dot(tri, hi) + _dot(tri, mid) + _dot(tri, lo)
        qb = q.astype(BF16)
        kb = k.astype(BF16)
        a = m_ref[nl] * _dot_nt(qb, kb)
        w = L // 2
        for lvl in range(nl):
            e = jnp.exp2(-jnp.abs(b - _pair_ref(b, w, reverse))).astype(BF16)
            a = a + m_ref[lvl] * _dot_nt(qb * e, kb * e)
            w //= 2
        tot_row = b[0:1] if reverse else b[L - 1:L]
        q_in = (q * jnp.exp2(b)).astype(BF16)
        k_out = (k * jnp.exp2(tot_row - b)).astype(BF16)
        st = st_scr[...]
        o = _dot(a.astype(BF16), v.astype(BF16)) + _dot_nt(q_in, st.astype(BF16))
        st_scr[...] = st * jnp.exp2(tot_row) + _dot(v.T.astype(BF16), k_out)
        return r0, o

    stf_scr[...] = jnp.zeros_like(stf_scr)
    stb_scr[...] = jnp.zeros_like(stb_scr)

    def body(ci, carry):
        r0, o = chunk(ci, ffp_ref, trif_ref, mf_ref, stf_scr, False)
        of_scr[pl.ds(r0, L), :] = o
        r0, o = chunk(nc - 1 - ci, fbp_ref, trib_ref, mb_ref, stb_scr, True)
        ob_scr[pl.ds(r0, L), :] = o
        return carry

    lax.fori_loop(0, nc, body, 0, unroll=4)
    nw = nw_ref[...]
    R = 4 * L if S % (4 * L) == 0 else L

    def finish(i, carry):
        r0 = pl.multiple_of(i * R, R)
        o = of_scr[pl.ds(r0, R), :] + ob_scr[pl.ds(r0, R), :]
        o = o * lax.rsqrt(jnp.mean(o * o, axis=-1, keepdims=True) + RMS_EPS)
        o = o * nw * _sigmoid(gp_ref[pl.ds(r0, R), :])
        o_ref[pl.ds(r0, R), :] = o.astype(o_ref.dtype)
        return carry

    lax.fori_loop(0, S // R, finish, 0)


def hgrn2(proj, lb, norm_w, L=HG_CHUNK):
    B, S, _ = proj.shape
    m_f, m_b, nl = _hgrn2_masks(L)
    t = np.arange(L)
    tri_f = jnp.asarray((t[None, :] <= t[:, None]).astype(np.float32), BF16)
    tri_b = jnp.asarray((t[None, :] >= t[:, None]).astype(np.float32), BF16)
    d = HG_DIM
    nb = HG_WIDTH // d

    def col(off):
        return pl.BlockSpec((None, S, d), lambda b, h, off=off: (b, 0, off // d + h))

    vec = pl.BlockSpec((1, d), lambda b, h: (0, h))
    const2 = lambda a: pl.BlockSpec(a.shape, lambda b, h: (0, 0))
    const3 = lambda a: pl.BlockSpec(a.shape, lambda b, h: (0, 0, 0))
    return pl.pallas_call(
        functools.partial(_hgrn2_kernel, L=L, nl=nl),
        grid=(B, nb),
        in_specs=[col(OFF_Q), col(OFF_FF), col(OFF_FB), col(OFF_I), col(OFF_G), vec, vec,
                  const2(tri_f), const2(tri_b), const3(m_f), const3(m_b)],
        out_specs=pl.BlockSpec((None, S, d), lambda b, h: (b, 0, h)),
        out_shape=jax.ShapeDtypeStruct((B, S, HG_WIDTH), BF16),
        scratch_shapes=[pltpu.VMEM((S, d), F32), pltpu.VMEM((S, d), F32),
                        pltpu.VMEM((d, d), F32), pltpu.VMEM((d, d), F32)],
        compiler_params=_cparams("parallel", "parallel"),
        name="hgrn2",
    )(proj, proj, proj, proj, proj, lb.reshape(1, HG_WIDTH), norm_w.reshape(1, HG_WIDTH),
      tri_f, tri_b, jnp.asarray(m_f), jnp.asarray(m_b))


def _shift_rows(x, k):
    if k == 0:
        return x
    n = x.shape[0]
    rolled = pltpu.roll(x, (-k) % n, 0)
    t = lax.broadcasted_iota(jnp.int32, x.shape, 0)
    ok = (t + k >= 0) & (t + k < n)
    return jnp.where(ok, rolled, 0.0)


def _conv_silu(x, w, b):
    half = w.shape[0] // 2
    acc = b
    for j in range(w.shape[0]):
        acc = acc + w[j:j + 1, :] * _shift_rows(x, j - half)
    return _silu(acc)


def _ssd_kernel(x_ref, b_ref, c_ref, z_ref, dtc_ref, dtr_ref,
                cwx_ref, cwb_ref, cwc_ref, cbx_ref, cbb_ref, cbc_ref,
                dtbc_ref, dtbr_ref, alr_ref, alc_ref, dsk_ref, nw_ref,
                tril_ref, triu_ref,
                o_ref, xs_scr, bs_scr, cs_scr, yf_scr, yb_scr, stf_scr, stb_scr, *, L, hg, P):
    S = x_ref.shape[0]
    nc = S // L
    nh = 2 * hg
    gw = hg * P
    N = bs_scr.shape[1]

    a_row = jnp.exp(alr_ref[...]) * NEG_LOG2E
    a_col = jnp.exp(alc_ref[...]) * NEG_LOG2E
    tril = tril_ref[...]
    triu = triu_ref[...]
    ti = lax.broadcasted_iota(jnp.int32, (L, L), 0)
    si = lax.broadcasted_iota(jnp.int32, (L, L), 1)
    lane2 = lax.broadcasted_iota(jnp.int32, (L, 2 * P), 1)

    def softplus(v):
        return jnp.maximum(v, 0.0) + jnp.log(1.0 + jnp.exp(-jnp.abs(v)))

    def per_head(cols):
        tiles = []
        for j in range(0, hg, 2):
            lo = jnp.broadcast_to(cols[j], (L, 2 * P))
            hi = jnp.broadcast_to(cols[j + 1], (L, 2 * P))
            tiles.append(jnp.where(lane2 < P, lo, hi))
        return jnp.concatenate(tiles, axis=1)

    def chunk(c, d, g, st_scr):
        cum_c = tril if d == 0 else triu
        cum_r = triu if d == 0 else tril
        r0 = pl.multiple_of(c * L, L)
        x = xs_scr[pl.ds(r0, L), :]
        bm = bs_scr[pl.ds(r0, L), :]
        cm = cs_scr[pl.ds(r0, L), :]
        dt_c = softplus(dtc_ref[pl.ds(r0, L), :] + dtbc_ref[...])
        dt_r = softplus(dtr_ref[:, pl.ds(r0, L)] + dtbr_ref[...])
        h1, h2, h3 = _split3(dt_c * a_row)
        acum_c = _dot(cum_c, h1) + _dot(cum_c, h2) + _dot(cum_c, h3)
        g1, g2, g3 = _split3(dt_r * a_col)
        acum_r = _dot(g1, cum_r) + _dot(g2, cum_r) + _dot(g3, cum_r)
        ones_rows = jnp.ones((8, L), BF16)
        tot = (_dot(ones_rows, h1) + _dot(ones_rows, h2) + _dot(ones_rows, h3))[0:1]
        cb = _dot_nt(cm.astype(BF16), bm.astype(BF16))
        keep = (si <= ti) if d == 0 else (si >= ti)
        ys, in_cols, out_cols, dec_cols = [], [], [], []
        for hh in range(hg):
            j = d * nh + g * hg + hh
            ac = acum_c[:, j:j + 1]
            dc = dt_c[:, j:j + 1]
            te = tot[:, j:j + 1]
            ar = acum_r[j:j + 1, :]
            dr = dt_r[j:j + 1, :]
            seg = ac - ar
            decay = jnp.where(keep, jnp.exp2(jnp.where(keep, seg, 0.0)), 0.0)
            w = (cb * decay * dr).astype(BF16)
            ys.append(_dot(w, x[:, hh * P:(hh + 1) * P].astype(BF16)))
            in_cols.append(jnp.exp2(te - ac) * dc)
            out_cols.append(jnp.exp2(ac))
            dec_cols.append(jnp.broadcast_to(jnp.exp2(te), (1, P)))
        y_diag = jnp.concatenate(ys, axis=1)
        st = st_scr[...]
        y_off = _dot(cm.astype(BF16), st.astype(BF16)) * per_head(out_cols)
        x_in = (x * per_head(in_cols)).astype(BF16)
        st_scr[...] = st * jnp.concatenate(dec_cols, axis=1) + _dot(bm.T.astype(BF16), x_in)
        return r0, y_diag + y_off

    R = 2 * L
    for g in range(x_ref.shape[1] // gw):
        cols = slice(g * gw, (g + 1) * gw)
        ncols = slice(g * N, (g + 1) * N)
        xs_scr[...] = _conv_silu(x_ref[:, cols], cwx_ref[:, cols], cbx_ref[:, cols])
        bs_scr[...] = _conv_silu(b_ref[:, ncols], cwb_ref[:, ncols], cbb_ref[:, ncols])
        cs_scr[...] = _conv_silu(c_ref[:, ncols], cwc_ref[:, ncols], cbc_ref[:, ncols])
        stf_scr[...] = jnp.zeros_like(stf_scr)
        stb_scr[...] = jnp.zeros_like(stb_scr)

        def body(ci, carry, g=g):
            r0, y = chunk(ci, 0, g, stf_scr)
            yf_scr[pl.ds(r0, L), :] = y
            r0, y = chunk(nc - 1 - ci, 1, g, stb_scr)
            yb_scr[pl.ds(r0, L), :] = y
            return carry

        lax.fori_loop(0, nc, body, 0, unroll=2)

        def finish(i, carry, cols=cols):
            r0 = pl.multiple_of(i * R, R)
            y = (yf_scr[pl.ds(r0, R), :] + yb_scr[pl.ds(r0, R), :]
                 + dsk_ref[:, cols] * xs_scr[pl.ds(r0, R), :])
            y = y * _silu(z_ref[pl.ds(r0, R), cols])
            y = y * lax.rsqrt(jnp.mean(y * y, axis=-1, keepdims=True) + RMS_EPS)
            o_ref[pl.ds(r0, R), cols] = (y * nw_ref[:, cols]).astype(o_ref.dtype)
            return carry

        lax.fori_loop(0, S // R, finish, 0)


def ssd(proj, dt_rows, conv_w, conv_b, dt_bias, a_log, d_skip, norm_w, L=M2_CHUNK):
    B, S, _ = proj.shape
    hg = M2_HEADS // M2_GROUPS
    gw = hg * M2_P
    t = np.arange(L)
    tril = jnp.asarray((t[None, :] <= t[:, None]).astype(np.float32), BF16)
    triu = jnp.asarray((t[None, :] >= t[:, None]).astype(np.float32), BF16)
    nb_c = OFF_XBC + M2_INNER
    nc_c = nb_c + M2_GROUPS * M2_N
    xw = conv_w[:, :M2_INNER]
    bw = conv_w[:, M2_INNER:M2_INNER + M2_GROUPS * M2_N]
    cw = conv_w[:, M2_INNER + M2_GROUPS * M2_N:]
    cb2 = conv_b.reshape(1, -1)
    xb = cb2[:, :M2_INNER]
    bb = cb2[:, M2_INNER:M2_INNER + M2_GROUPS * M2_N]
    cbb = cb2[:, M2_INNER + M2_GROUPS * M2_N:]
    nh2 = 2 * M2_HEADS
    dtb_row = jnp.zeros((1, LANE), F32).at[0, :nh2].set(dt_bias.reshape(-1))
    al_row = jnp.zeros((1, LANE), F32).at[0, :nh2].set(a_log.reshape(-1))
    dtb_col = dt_bias.reshape(nh2, 1)
    al_col = a_log.reshape(nh2, 1)
    dsk = jnp.repeat(d_skip, M2_P).reshape(1, M2_INNER)
    nw = norm_w.reshape(1, M2_INNER)

    gn = M2_GROUPS * M2_N
    full2 = lambda a: pl.BlockSpec(a.shape, lambda b: (0, 0))
    return pl.pallas_call(
        functools.partial(_ssd_kernel, L=L, hg=hg, P=M2_P),
        grid=(B,),
        in_specs=[
            pl.BlockSpec((None, S, M2_INNER), lambda b: (b, 0, OFF_XBC // M2_INNER)),
            pl.BlockSpec((None, S, gn), lambda b: (b, 0, nb_c // gn)),
            pl.BlockSpec((None, S, gn), lambda b: (b, 0, nc_c // gn)),
            pl.BlockSpec((None, S, M2_INNER), lambda b: (b, 0, OFF_Z // M2_INNER)),
            pl.BlockSpec((None, S, LANE), lambda b: (b, 0, OFF_DT // LANE)),
            pl.BlockSpec((nh2, S), lambda b: (0, b)),
            full2(xw), full2(bw), full2(cw), full2(xb), full2(bb), full2(cbb),
            full2(dtb_row), full2(dtb_col), full2(al_row), full2(al_col),
            full2(dsk), full2(nw), full2(tril), full2(triu),
        ],
        out_specs=pl.BlockSpec((None, S, M2_INNER), lambda b: (b, 0, 0)),
        out_shape=jax.ShapeDtypeStruct((B, S, M2_INNER), BF16),
        scratch_shapes=[pltpu.VMEM((S, gw), F32), pltpu.VMEM((S, M2_N), F32),
                        pltpu.VMEM((S, M2_N), F32), pltpu.VMEM((S, gw), F32),
                        pltpu.VMEM((S, gw), F32), pltpu.VMEM((M2_N, gw), F32),
                        pltpu.VMEM((M2_N, gw), F32)],
        compiler_params=_cparams("parallel"),
        name="ssd",
    )(proj, proj, proj, proj, proj, dt_rows, xw, bw, cw, xb, bb, cbb,
      dtb_row, dtb_col, al_row, al_col, dsk, nw, tril, triu)


def _ffn_kernel(h_ref, oa_ref, ob_ref, wa_ref, wb_ref, nw_ref, wg_ref, wu_ref, wd_ref, o_ref,
                xn_scr, acc_scr):
    f = pl.program_id(1)

    @pl.when(f == 0)
    def _():
        x = h_ref[...] + _dot(oa_ref[...], wa_ref[...]) + _dot(ob_ref[...], wb_ref[...])
        xn_scr[...] = _rms(x, nw_ref[...]).astype(BF16)
        acc_scr[...] = x

    xn = xn_scr[...]
    act = _silu(_dot(xn, wg_ref[...].astype(BF16))) * _dot(xn, wu_ref[...].astype(BF16))
    acc_scr[...] += _dot(act.astype(BF16), wd_ref[...].astype(BF16))

    @pl.when(f == pl.num_programs(1) - 1)
    def _():
        o_ref[...] = acc_scr[...]


def ffn(h, oa, ob, w_out, nw, w_gu, w_down, layer, tm=1024, tf=512):
    m, d = h.shape
    dff = w_down.shape[1]
    nf = dff // tf
    ka, kb = oa.shape[1], ob.shape[1]
    assert ka == kb
    return pl.pallas_call(
        _ffn_kernel,
        grid=(m // tm, nf),
        in_specs=[pl.BlockSpec((tm, d), lambda i, f: (i, 0)),
                  pl.BlockSpec((tm, ka), lambda i, f: (i, 0)),
                  pl.BlockSpec((tm, kb), lambda i, f: (i, 0)),
                  pl.BlockSpec((None, ka, d), lambda i, f: (layer, 0, 0)),
                  pl.BlockSpec((None, kb, d), lambda i, f: (layer, 1, 0)),
                  pl.BlockSpec((1, d), lambda i, f: (0, 0)),
                  pl.BlockSpec((None, d, tf), lambda i, f: (layer, 0, f)),
                  pl.BlockSpec((None, d, tf), lambda i, f: (layer, 0, nf + f)),
                  pl.BlockSpec((None, tf, d), lambda i, f: (layer, f, 0))],
        out_specs=pl.BlockSpec((tm, d), lambda i, f: (i, 0)),
        out_shape=jax.ShapeDtypeStruct((m, d), F32),
        scratch_shapes=[pltpu.VMEM((tm, d), BF16), pltpu.VMEM((tm, d), F32)],
        compiler_params=_cparams("parallel", "arbitrary"),
        name="ffn",
    )(h, oa, ob, w_out, w_out, nw.reshape(1, d), w_gu, w_gu, w_down)


def _ple_kernel(*refs, moe, final):
    if moe:
        h_ref, y2_ref, meta_ref, p_ref, nw_ref, wg_ref, wp_ref, fw_ref, o_ref = refs
        meta = meta_ref[...]
        d = h_ref.shape[1]
        h = h_ref[...] + (meta[:, 2:3] * y2_ref[:, 0:d] + meta[:, 3:4] * y2_ref[:, d:2 * d])
    else:
        h_ref, p_ref, nw_ref, wg_ref, wp_ref, fw_ref, o_ref = refs
        h = h_ref[...]
    gate = _sigmoid(_dot(_rms(h, nw_ref[...]).astype(BF16), wg_ref[...]))
    h = h + gate * _dot(p_ref[...].astype(BF16), wp_ref[...])
    if final:
        h = _rms(h, fw_ref[...])
    o_ref[...] = h


def ple(h, p, nw, wg, wp, fw, layer, y2=None, meta=None, final=False, tm=512):
    m, d = h.shape
    moe = y2 is not None
    row = lambda w: pl.BlockSpec((tm, w), lambda i: (i, 0))
    slab = lambda a: pl.BlockSpec((None,) + a.shape[1:], lambda i: (layer, 0, 0))
    vec = pl.BlockSpec((1, d), lambda i: (0, 0))
    in_specs = [row(d)]
    args = [h]
    if moe:
        in_specs += [row(2 * d), row(meta.shape[1])]
        args += [y2, meta]
    in_specs += [pl.BlockSpec((None, tm, p.shape[2]), lambda i: (layer, i, 0)), vec, slab(wg), slab(wp), vec]
    args += [p, nw.reshape(1, d), wg, wp, fw.reshape(1, d)]
    return pl.pallas_call(
        functools.partial(_ple_kernel, moe=moe, final=final),
        grid=(m // tm,),
        in_specs=in_specs,
        out_specs=row(d),
        out_shape=jax.ShapeDtypeStruct((m, d), F32),
        compiler_params=_cparams("parallel"),
        name="ple",
    )(*args)


def _s5_tables(a_re, a_im, log_step, b_re, b_im, c_re, c_im):
    G, N = a_re.shape[1:]
    C = S5_GROUP_SIZE
    T = S5_STEPS
    gt = LANE // C
    Z = G // gt
    tau = jnp.arange(T + 1, dtype=F32)
    steps = jnp.arange(T)

    def cmul(xr, xi, yr, yi):
        return xr * yr - xi * yi, xr * yi + xi * yr

    st_parts, rd_parts, k_parts, lam_rows = [], [], [], []
    for d in range(2):
        delta = jnp.exp(log_step[d])[:, None]
        ar, ai = a_re[d], a_im[d]
        mag = jnp.exp(ar * delta)
        lam_re, lam_im = mag * jnp.cos(ai * delta), mag * jnp.sin(ai * delta)
        den = ar * ar + ai * ai
        num_re = lam_re - 1.0
        coef_re = (num_re * ar + lam_im * ai) / den
        coef_im = (lam_im * ar - num_re * ai) / den
        br = coef_re[..., None] * b_re - coef_im[..., None] * b_im
        bi = coef_re[..., None] * b_im + coef_im[..., None] * b_re
        cr, ci = c_re[d], c_im[d]
        pm = jnp.exp((ar * delta)[None] * tau[:, None, None])
        ang = (ai * delta)[None] * tau[:, None, None]
        pr, pi = pm * jnp.cos(ang), pm * jnp.sin(ang)
        e_in = (T - 1 - steps) if d == 0 else steps
        sr, si = cmul(pr[e_in][..., None], pi[e_in][..., None], br[None], bi[None])
        e_out = (steps + 1) if d == 0 else (T - steps)
        cpr, cpi = cmul(cr[None], ci[None], pr[e_out][:, :, None, :], pi[e_out][:, :, None, :])
        lbr, lbi = cmul(pr[:T][..., None], pi[:T][..., None], br[None], bi[None])
        ktau = jnp.einsum('gon,tgni->tgoi', cr, lbr) - jnp.einsum('gon,tgni->tgoi', ci, lbi)
        st_parts += [sr, si]
        rd_parts += [cpr, -cpi]
        k_parts.append(ktau)
        lam_rows += [pr[T], pi[T]]

    NS = gt * N
    st = jnp.stack(st_parts).reshape(4, T, Z, gt, N, C).transpose(2, 1, 5, 0, 3, 4)
    wst = block_diag_rows(st.reshape(Z, T * C, 4 * NS), C, N)
    rd = jnp.stack(rd_parts).reshape(4, T, Z, gt, C, N).transpose(2, 1, 4, 0, 3, 5)
    wc = block_diag_rows(rd.reshape(Z, T * C, 4 * NS), C, N)
    lag = steps[None, :] - steps[:, None]
    sel_f = (lag[:, :, None] == steps[None, None, :]).astype(F32)
    sel_b = (-lag[:, :, None] == steps[None, None, :]).astype(F32)
    toe = (jnp.einsum('stk,kgoi->stgoi', sel_f, k_parts[0])
           + jnp.einsum('stk,kgoi->stgoi', sel_b, k_parts[1]))
    toe = toe.reshape(T, T, Z, gt, C, C).transpose(2, 0, 5, 1, 3, 4)
    kin = block_diag_rows(toe.reshape(Z, T * C, T * LANE), C, C)
    lam = jnp.stack([r.reshape(Z, NS) for r in lam_rows], axis=1)
    lam = jnp.concatenate([lam, jnp.zeros_like(lam)], axis=1)
    return wst, kin, wc, lam


def _block_diag_kernel(s_ref, o_ref, *, C, gcol):
    n = o_ref.shape[1]
    gt = LANE // C
    row_g = lax.broadcasted_iota(jnp.int32, (LANE, n), 0) // C
    col_g = (lax.broadcasted_iota(jnp.int32, (LANE, n), 1) // gcol) % gt
    keep = row_g == col_g
    for t in range(o_ref.shape[0] // LANE):
        strip = s_ref[t * C:(t + 1) * C, :]
        tiled = jnp.broadcast_to(strip[None], (gt, C, n)).reshape(LANE, n)
        o_ref[t * LANE:(t + 1) * LANE, :] = jnp.where(keep, tiled, 0.0).astype(o_ref.dtype)


def block_diag_rows(strips, C, gcol):
    Z, rows, n = strips.shape
    gt = LANE // C
    return pl.pallas_call(
        functools.partial(_block_diag_kernel, C=C, gcol=gcol),
        grid=(Z,),
        in_specs=[pl.BlockSpec((None, rows, n), lambda z: (z, 0, 0))],
        out_specs=pl.BlockSpec((None, rows * gt, n), lambda z: (z, 0, 0)),
        out_shape=jax.ShapeDtypeStruct((Z, rows * gt, n), BF16),
        compiler_params=_cparams("parallel"),
        name="block_diag_rows",
    )(strips)


def _s5_pre_kernel(x_ref, nw_ref, u_ref, xn_scr, tmp_scr):
    nb, R, D = x_ref.shape
    T = S5_STEPS
    rc = R // T
    xn = _rms(x_ref[...], nw_ref[...])
    for z in range(D // LANE):
        xn_scr[z] = xn[:, :, z * LANE:(z + 1) * LANE]
    for z in range(D // LANE):
        for t in range(T):
            for b in range(nb):
                tmp_scr[z, t, pl.ds(b, rc, stride=nb), :] = xn_scr[z, b, pl.ds(t, rc, stride=T), :]
            u_ref[z, :, t * LANE:(t + 1) * LANE] = tmp_scr[z, t].astype(u_ref.dtype)


def s5_pre(h3, nw, R=64):
    B, S, D = h3.shape
    T = S5_STEPS
    Z = D // LANE
    rc = R // T
    return pl.pallas_call(
        _s5_pre_kernel,
        grid=(S // R,),
        in_specs=[pl.BlockSpec((B, R, D), lambda i: (0, i, 0)),
                  pl.BlockSpec((1, D), lambda i: (0, 0))],
        out_specs=pl.BlockSpec((Z, rc * B, T * LANE), lambda i: (0, i, 0)),
        out_shape=jax.ShapeDtypeStruct((Z, S // T * B, T * LANE), BF16),
        scratch_shapes=[pltpu.VMEM((Z, B, R, LANE), F32), pltpu.VMEM((Z, T, rc * B, LANE), F32)],
        compiler_params=_cparams("parallel"),
        name="s5_pre",
    )(h3, nw.reshape(1, D))


def _s5_kernel(u_ref, wst_ref, kin_ref, wc_ref, lam_ref, y_ref, s_scr, *, nb, rblk):
    rows = u_ref.shape[0]
    nc = rows // nb
    ns = lam_ref.shape[1]
    for r0 in range(0, rows, rblk):
        s_scr[r0:r0 + rblk, :] = _dot(u_ref[r0:r0 + rblk, :], wst_ref[...])
    lam = lam_ref[...]
    lfr, lfi, lbr, lbi = lam[0:1], lam[1:2], lam[2:3], lam[3:4]

    def body(ci, carry):
        hfr, hfi, hbr, hbi = carry
        rf = pl.multiple_of(ci * nb, nb)
        rb = pl.multiple_of((nc - 1 - ci) * nb, nb)
        sfr = s_scr[pl.ds(rf, nb), 0:ns]
        sfi = s_scr[pl.ds(rf, nb), ns:2 * ns]
        sbr = s_scr[pl.ds(rb, nb), 2 * ns:3 * ns]
        sbi = s_scr[pl.ds(rb, nb), 3 * ns:4 * ns]
        s_scr[pl.ds(rf, nb), 0:ns] = hfr
        s_scr[pl.ds(rf, nb), ns:2 * ns] = hfi
        s_scr[pl.ds(rb, nb), 2 * ns:3 * ns] = hbr
        s_scr[pl.ds(rb, nb), 3 * ns:4 * ns] = hbi
        return (lfr * hfr - lfi * hfi + sfr, lfr * hfi + lfi * hfr + sfi,
                lbr * hbr - lbi * hbi + sbr, lbr * hbi + lbi * hbr + sbi)

    z = jnp.zeros((nb, ns), F32)
    lax.fori_loop(0, nc, body, (z, z, z, z))
    for r0 in range(0, rows, rblk):
        y = (_dot(u_ref[r0:r0 + rblk, :], kin_ref[...])
             + _dot_nt(s_scr[r0:r0 + rblk, :].astype(BF16), wc_ref[...]))
        y_ref[r0:r0 + rblk, :] = y.astype(y_ref.dtype)


def s5_scan(u, wst, kin, wc, lam, nb, rblk=512):
    Z, rows, K = u.shape
    per = lambda a: pl.BlockSpec((None,) + a.shape[1:], lambda s: (s, 0, 0))
    once = lambda a: pl.BlockSpec((None,) + a.shape[1:], lambda s: (s, 0, 0),
                                  pipeline_mode=pl.Buffered(1))
    return pl.pallas_call(
        functools.partial(_s5_kernel, nb=nb, rblk=rblk),
        grid=(Z,),
        in_specs=[per(u), once(wst), once(kin), once(wc), per(lam)],
        out_specs=pl.BlockSpec((None, rows, K), lambda s: (s, 0, 0)),
        out_shape=jax.ShapeDtypeStruct((Z, rows, K), BF16),
        scratch_shapes=[pltpu.VMEM((rows, wst.shape[2]), F32)],
        compiler_params=_cparams("parallel"),
        name="s5_scan",
    )(u, wst, kin, wc, lam)


def _s5_post_kernel(h_ref, y_ref, nw_ref, d_ref, wo_ref, wg_ref, o_ref, yf_scr, yt_scr):
    nb, R, D = h_ref.shape
    T = S5_STEPS
    rc = R // T
    for z in range(D // LANE):
        for t in range(T):
            yf_scr[z, t] = y_ref[z, :, t * LANE:(t + 1) * LANE].astype(F32)
            for b in range(nb):
                yt_scr[z, b, pl.ds(t, rc, stride=T), :] = yf_scr[z, t, pl.ds(b, rc, stride=nb), :]
    yt = jnp.concatenate([yt_scr[z] for z in range(D // LANE)], axis=-1)
    h = h_ref[...].reshape(nb * R, D)
    y = yt.reshape(nb * R, D) + d_ref[...] * _rms(h, nw_ref[...])
    act = jax.nn.gelu(y).astype(BF16)
    out = h + _dot(act, wo_ref[...]) * _sigmoid(_dot(act, wg_ref[...]))
    o_ref[...] = out.reshape(nb, R, D)


def s5_post(h3, y, nw, d_skip, glu_w, layer, R=64):
    B, S, D = h3.shape
    T = S5_STEPS
    Z = D // LANE
    rc = R // T
    row = pl.BlockSpec((B, R, D), lambda i: (0, i, 0))
    vec = pl.BlockSpec((1, D), lambda i: (0, 0))
    return pl.pallas_call(
        _s5_post_kernel,
        grid=(S // R,),
        in_specs=[row, pl.BlockSpec((Z, rc * B, T * LANE), lambda i: (0, i, 0)), vec, vec,
                  pl.BlockSpec((None, D, D), lambda i: (layer, 0, 0)),
                  pl.BlockSpec((None, D, D), lambda i: (layer, 0, 1))],
        out_specs=row,
        out_shape=jax.ShapeDtypeStruct((B, S, D), F32),
        scratch_shapes=[pltpu.VMEM((Z, T, rc * B, LANE), F32), pltpu.VMEM((Z, B, R, LANE), F32)],
        compiler_params=_cparams("parallel"),
        name="s5_post",
    )(h3, y, nw.reshape(1, D), d_skip.reshape(1, D), glu_w, glu_w)


def _router_kernel(h_ref, nw_ref, whi_ref, wlo_ref, tri_ref, meta_ref, cnt_ref, carry_scr):
    @pl.when(pl.program_id(0) == 0)
    def _():
        carry_scr[...] = jnp.zeros_like(carry_scr)

    xn = _rms(h_ref[...], nw_ref[...])
    x_hi, x_lo, _ = _split3(xn)
    w_hi, w_lo = whi_ref[...], wlo_ref[...]
    logits = _dot(x_hi, w_hi) + (_dot(x_hi, w_lo) + _dot(x_lo, w_hi))
    lane = lax.broadcasted_iota(jnp.int32, logits.shape, 1)
    neg = jnp.float32(-jnp.inf)
    lg = jnp.where(lane < N_EXPERTS, logits, neg)
    t1 = jnp.max(lg, axis=1, keepdims=True)
    i1 = jnp.min(jnp.where(lg == t1, lane, LANE), axis=1, keepdims=True)
    lg2 = jnp.where(lane == i1, neg, lg)
    t2 = jnp.max(lg2, axis=1, keepdims=True)
    i2 = jnp.min(jnp.where(lg2 == t2, lane, LANE), axis=1, keepdims=True)
    ex = jnp.exp(t2 - t1)
    g1 = 1.0 / (1.0 + ex)
    g2 = ex / (1.0 + ex)
    oh1 = jnp.where(lane == i1, 1.0, 0.0)
    oh2 = jnp.where(lane == i2, 1.0, 0.0)
    tri = tri_ref[...]
    before1 = _dot(tri, oh1.astype(BF16))
    before2 = _dot(tri, oh2.astype(BF16))
    tot1 = jnp.sum(oh1, axis=0, keepdims=True)
    tot2 = jnp.sum(oh2, axis=0, keepdims=True)
    carry = carry_scr[0:1]
    rank1 = jnp.sum(oh1 * (carry + before1), axis=1, keepdims=True)
    rank2 = jnp.sum(oh2 * (carry + tot1 + before2), axis=1, keepdims=True)
    counts = jnp.broadcast_to(carry + tot1 + tot2, carry_scr.shape)
    carry_scr[...] = counts
    cnt_ref[...] = counts
    meta = jnp.where(lane == 0, i1.astype(F32), 0.0)
    meta = jnp.where(lane == 1, i2.astype(F32), meta)
    meta = jnp.where(lane == 2, g1, meta)
    meta = jnp.where(lane == 3, g2, meta)
    meta = jnp.where(lane == 4, rank1, meta)
    meta = jnp.where(lane == 5, rank2, meta)
    meta_ref[...] = meta


def router(h, nw, w_router, tm=512):
    m, d = h.shape
    wr = jnp.zeros((d, LANE), F32).at[:, :N_EXPERTS].set(w_router)
    w_hi, w_lo, _ = _split3(wr)
    t = np.arange(tm)
    tri = jnp.asarray((t[None, :] < t[:, None]).astype(np.float32), BF16)
    return pl.pallas_call(
        _router_kernel,
        grid=(m // tm,),
        in_specs=[pl.BlockSpec((tm, d), lambda i: (i, 0)),
                  pl.BlockSpec((1, d), lambda i: (0, 0)),
                  pl.BlockSpec((d, LANE), lambda i: (0, 0)),
                  pl.BlockSpec((d, LANE), lambda i: (0, 0)),
                  pl.BlockSpec((tm, tm), lambda i: (0, 0))],
        out_specs=[pl.BlockSpec((tm, LANE), lambda i: (i, 0)),
                   pl.BlockSpec((8, LANE), lambda i: (0, 0))],
        out_shape=[jax.ShapeDtypeStruct((m, LANE), F32), jax.ShapeDtypeStruct((8, LANE), F32)],
        scratch_shapes=[pltpu.VMEM((8, LANE), F32)],
        compiler_params=_cparams("arbitrary"),
        name="router",
    )(h, nw.reshape(1, d), w_hi, w_lo, tri)


def _moe_kernel(arow_ref, blk_e_ref, nvalid_ref, h_hbm, nw_ref, wg_ref, wu_ref, wd_ref,
                y2_hbm, xg_scr, xn_scr, acc_scr, sem_in, sem_out, *, bm, n_tok):
    g = pl.program_id(0)
    f = pl.program_id(1)
    nf = pl.num_programs(1)
    slot = g % 2
    other = 1 - slot
    d = acc_scr.shape[2]
    rows_per_step = bm // (nf - 1)
    prv, cur, nxt = g * bm, (g + 1) * bm, (g + 2) * bm
    nv = nvalid_ref[g]

    def in_copy(off, r, s):
        tok = jnp.minimum(arow_ref[off + r] >> 1, n_tok - 1)
        return pltpu.make_async_copy(h_hbm.at[pl.ds(tok, 1)], xg_scr.at[s, pl.ds(r, 1)], sem_in.at[s])

    def out_copy(off, r, s):
        a = arow_ref[off + r]
        col = pl.multiple_of((a & 1) * d, d)
        return pltpu.make_async_copy(acc_scr.at[s, pl.ds(r, 1)],
                                     y2_hbm.at[pl.ds(a >> 1, 1), pl.ds(col, d)], sem_out.at[s])

    def wait_in(s):
        pltpu.make_async_copy(h_hbm.at[pl.ds(0, bm)], xg_scr.at[s], sem_in.at[s]).wait()

    def wait_out(s):
        pltpu.make_async_copy(acc_scr.at[s], y2_hbm.at[pl.ds(0, bm), pl.ds(0, d)], sem_out.at[s]).wait()

    @pl.when((g == 0) & (f == 0))
    def _():
        acc_scr[...] = jnp.zeros_like(acc_scr)

        def start(r, c):
            in_copy(cur, r, slot).start()
            return c

        lax.fori_loop(0, bm, start, 0)

    @pl.when(f == 0)
    def _():
        wait_in(slot)
        xn_scr[...] = _rms(xg_scr[slot], nw_ref[...]).astype(BF16)
        acc_scr[slot] = jnp.zeros((bm, d), F32)

    def move_rows(part, parts):
        base = pl.multiple_of(f * rows_per_step, 8)
        per = rows_per_step // parts
        for j in range(part * per, (part + 1) * per):
            in_copy(nxt, base + j, other).start()
            out_copy(prv, base + j, other).start()

    def experts(with_moves):
        parts = 2
        tc = wg_ref.shape[2] // parts
        xn = xn_scr[...]
        upd = None
        for c in range(parts):
            cs = slice(c * tc, (c + 1) * tc)
            act = (_silu(_dot(xn, wg_ref[0, :, cs].astype(BF16)))
                   * _dot(xn, wu_ref[0, :, cs].astype(BF16)))
            part = _dot(act.astype(BF16), wd_ref[0, cs, :].astype(BF16))
            upd = part if upd is None else upd + part
            if with_moves:
                move_rows(c, parts)
        acc_scr[slot] += upd

    @pl.when((nv > 0) & (f < nf - 1))
    def _():
        experts(True)

    @pl.when((nv > 0) & (f == nf - 1))
    def _():
        experts(False)

    @pl.when((nv == 0) & (f < nf - 1))
    def _():
        move_rows(0, 1)

    @pl.when(f == nf - 1)
    def _():
        wait_out(other)

    @pl.when((f == nf - 1) & (g == pl.num_programs(0) - 1))
    def _():
        wait_in(other)


def moe_experts(h, nw, w_gu, w_down, arow, blk_e, nvalid, layer, bm=MOE_BM, tf=512):
    t, d = h.shape
    dff = w_down.shape[1]
    nf = dff // tf
    assert bm % (2 * (nf - 1)) == 0 and bm % 16 == 0
    n_steps = blk_e.shape[0]
    e0 = layer * N_EXPERTS

    def wmap(col0):
        def index(g, f, ar, be, nv):
            return (e0 + be[g], 0, col0 + jnp.where(nv[g] > 0, f, nf - 1))
        return index

    def dmap(g, f, ar, be, nv):
        return (e0 + be[g], jnp.where(nv[g] > 0, f, nf - 1), 0)

    grid_spec = pltpu.PrefetchScalarGridSpec(
        num_scalar_prefetch=3,
        grid=(n_steps, nf),
        in_specs=[pl.BlockSpec(memory_space=pl.ANY),
                  pl.BlockSpec((1, d), lambda g, f, ar, be, nv: (0, 0)),
                  pl.BlockSpec((1, d, tf), wmap(0)),
                  pl.BlockSpec((1, d, tf), wmap(nf)),
                  pl.BlockSpec((1, tf, d), dmap)],
        out_specs=pl.BlockSpec(memory_space=pl.ANY),
        scratch_shapes=[pltpu.VMEM((2, bm, d), F32), pltpu.VMEM((bm, d), BF16),
                        pltpu.VMEM((2, bm, d), F32),
                        pltpu.SemaphoreType.DMA((2,)), pltpu.SemaphoreType.DMA((2,))],
    )
    return pl.pallas_call(
        functools.partial(_moe_kernel, bm=bm, n_tok=t),
        grid_spec=grid_spec,
        out_shape=jax.ShapeDtypeStruct((t + bm // 2, 2 * d), F32),
        compiler_params=_cparams("arbitrary", "arbitrary"),
        name="moe_experts",
    )(arow, blk_e, nvalid, h, nw.reshape(1, d), w_gu, w_gu, w_down)


def _moe_plan(meta, counts, bm):
    t = meta.shape[0]
    n_assign = 2 * t
    experts = jnp.arange(N_EXPERTS, dtype=jnp.int32)
    cnt = counts[0, :N_EXPERTS].astype(jnp.int32)
    padded = (cnt + bm - 1) // bm * bm
    pend = jnp.cumsum(padded)
    pstart = pend - padded
    e = meta[:, 0:2].astype(jnp.int32)
    rank = meta[:, 4:6].astype(jnp.int32)
    dest = jnp.sum(jnp.where(e[..., None] == experts, pstart, 0), axis=-1) + rank
    n_blocks = -(-n_assign // bm) + N_EXPERTS
    spare = 2 * t + jnp.arange(bm, dtype=jnp.int32)
    arow = jnp.tile(spare, n_blocks).at[dest.reshape(-1)].set(jnp.arange(n_assign, dtype=jnp.int32))
    arow = jnp.concatenate([spare, arow, spare, spare])
    blk_start = jnp.arange(n_blocks, dtype=jnp.int32) * bm
    blk_e = jnp.minimum(jnp.sum((pend[None, :] <= blk_start[:, None]).astype(jnp.int32), axis=1),
                        N_EXPERTS - 1)
    nvalid = jnp.clip((pstart + cnt)[blk_e] - blk_start, 0, bm)
    nvalid = jnp.where(blk_start < pend[-1], nvalid, 0).astype(jnp.int32)
    blk_e = jnp.concatenate([blk_e, blk_e[-1:]]).astype(jnp.int32)
    nvalid = jnp.concatenate([nvalid, jnp.zeros((1,), jnp.int32)])
    return arow, blk_e, nvalid


def kernel(x, p, norm_mix, norm_ffn, norm_ple, final_norm, ple_gate, ple_proj, ev_w_in, ev_w_out,
           hg_lb_logits, hg_norm_w, m2_conv_w, m2_conv_b, m2_dt_bias, m2_a_log, m2_d, m2_norm_w,
           s5_a_re, s5_a_im, s5_log_step, s5_b_re, s5_b_im, s5_c_re, s5_c_im, s5_d, s5_glu_w,
           ffn_w_gu, ffn_w_down, moe_router, moe_w_gu, moe_w_down):
    B, S, D = x.shape
    T = B * S
    depth = norm_mix.shape[0]
    lb_soft = jax.nn.softmax(hg_lb_logits.astype(F32), axis=0)
    hg_lb = jnp.cumsum(lb_soft, axis=0) - lb_soft[0]
    w_in = jnp.pad(ev_w_in, ((0, 0), (0, 0), (0, EVEN_IN_PAD - ev_w_in.shape[2]))).astype(BF16)
    w_dt_t = jnp.swapaxes(ev_w_in[:, :, OFF_DT:], 1, 2).astype(BF16)
    w_out = ev_w_out.astype(BF16)
    w_moe_gu = moe_w_gu.reshape((-1,) + moe_w_gu.shape[2:])
    w_moe_down = moe_w_down.reshape((-1,) + moe_w_down.shape[2:])
    w_glu = s5_glu_w.astype(BF16)
    w_ple_gate, w_ple_proj = ple_gate.astype(BF16), ple_proj.astype(BF16)
    p_rows = p.reshape(depth, T, p.shape[-1])
    h = x.reshape(T, D)
    for layer in range(depth):
        j = layer // 2
        if layer % 2 == 0:
            proj, dt_rows = rms_matmul(h, norm_mix[layer], w_in, w_dt_t, j)
            proj = proj.reshape(B, S, EVEN_IN_PAD)
            o_a = hgrn2(proj, hg_lb[j], hg_norm_w[j])
            o_b = ssd(proj, dt_rows, m2_conv_w[j], m2_conv_b[j], m2_dt_bias[j], m2_a_log[j],
                      m2_d[j], m2_norm_w[j])
            h = ffn(h, o_a.reshape(T, HG_WIDTH), o_b.reshape(T, M2_INNER), w_out,
                    norm_ffn[layer], ffn_w_gu, ffn_w_down, j)
            y2 = meta = None
        else:
            h3 = h.reshape(B, S, D)
            wst, kin, wc, lam = _s5_tables(s5_a_re[j], s5_a_im[j], s5_log_step[j], s5_b_re[j],
                                           s5_b_im[j], s5_c_re[j], s5_c_im[j])
            y = s5_scan(s5_pre(h3, norm_mix[layer]), wst, kin, wc, lam, B)
            h = s5_post(h3, y, norm_mix[layer], s5_d[j], w_glu, j).reshape(T, D)
            meta, counts = router(h, norm_ffn[layer], moe_router[j])
            arow, blk_e, nvalid = _moe_plan(meta, counts, MOE_BM)
            y2 = moe_experts(h, norm_ffn[layer], w_moe_gu, w_moe_down, arow, blk_e, nvalid, j)
        h = ple(h, p_rows, norm_ple[layer], w_ple_gate, w_ple_proj, final_norm, layer,
                y2=y2, meta=meta, final=(layer == depth - 1))
    return h.reshape(B, S, D)
```

```python
import functools

import numpy as np
import jax
import jax.numpy as jnp
from jax import lax
from jax.experimental import pallas as pl
from jax.experimental.pallas import tpu as pltpu

F32 = jnp.float32
BF16 = jnp.bfloat16

RMS_EPS = 1e-6
LB_FLOOR = 1e-30
NEG_LOG2E = -1.4426950408889634
LANE = 128
VMEM_LIMIT = 56 * 1024 * 1024

HG_HEADS = 4
HG_DIM = 128
HG_WIDTH = HG_HEADS * HG_DIM
HG_CHUNK = 128
M2_HEADS = 8
M2_P = 64
M2_INNER = M2_HEADS * M2_P
M2_GROUPS = 2
M2_N = 128
M2_CHUNK = 128
M2_XBC = M2_INNER + 2 * M2_GROUPS * M2_N
S5_GROUP_SIZE = 16
S5_STATE = 64
S5_STEPS = 8
N_EXPERTS = 8
MOE_BM = 864

OFF_Q, OFF_FF, OFF_FB, OFF_I, OFF_G = (k * HG_WIDTH for k in range(5))
OFF_Z = 5 * HG_WIDTH
OFF_XBC = OFF_Z + M2_INNER
OFF_DT = OFF_XBC + M2_XBC
EVEN_IN_PAD = OFF_DT + LANE


def _cparams(*sem):
    return pltpu.CompilerParams(dimension_semantics=sem, vmem_limit_bytes=VMEM_LIMIT)


def _rms(x, w):
    return x * lax.rsqrt(jnp.mean(x * x, axis=-1, keepdims=True) + RMS_EPS) * w


def _sigmoid(x):
    return 1.0 / (1.0 + jnp.exp2(x * NEG_LOG2E))


def _silu(x):
    return x * _sigmoid(x)


def _dot(a, b):
    return jnp.dot(a, b, preferred_element_type=F32)


def _dot_nt(a, b):
    return lax.dot_general(a, b, (((1,), (1,)), ((), ())), preferred_element_type=F32)


def _split3(x):
    hi = x.astype(BF16)
    r1 = x - hi.astype(F32)
    mid = r1.astype(BF16)
    lo = (r1 - mid.astype(F32)).astype(BF16)
    return hi, mid, lo


def _rms_matmul_kernel(x_ref, nw_ref, w_ref, wdt_ref, o_ref, dtr_ref):
    xn = _rms(x_ref[...], nw_ref[...]).astype(BF16)
    o_ref[...] = _dot(xn, w_ref[...])
    dtr_ref[...] = _dot_nt(wdt_ref[...], xn)


def rms_matmul(x, nw, w, w_dt_t, layer, tm=256):
    m, k = x.shape
    n = w.shape[2]
    nr = w_dt_t.shape[1]
    return pl.pallas_call(
        _rms_matmul_kernel,
        grid=(m // tm,),
        in_specs=[pl.BlockSpec((tm, k), lambda i: (i, 0)),
                  pl.BlockSpec((1, k), lambda i: (0, 0)),
                  pl.BlockSpec((None, k, n), lambda i: (layer, 0, 0)),
                  pl.BlockSpec((None, nr, k), lambda i: (layer, 0, 0))],
        out_specs=[pl.BlockSpec((tm, n), lambda i: (i, 0)),
                   pl.BlockSpec((nr, tm), lambda i: (0, i))],
        out_shape=[jax.ShapeDtypeStruct((m, n), F32), jax.ShapeDtypeStruct((nr, m), F32)],
        compiler_params=_cparams("parallel"),
        name="rms_matmul",
    )(x, nw.reshape(1, k), w, w_dt_t)


def _hgrn2_masks(L):
    t = np.arange(L)
    masks = []
    w = L // 2
    while w >= 1:
        blk = t // w
        masks.append((blk % 2 == 1)[:, None] & (blk[None, :] == blk[:, None] - 1))
        w //= 2
    masks.append(np.eye(L, dtype=bool))
    m_f = np.stack(masks).astype(np.float32)
    return m_f, np.ascontiguousarray(m_f[:, ::-1, ::-1]), len(masks) - 1


def _pair_ref(b, w, reverse):
    L, d = b.shape
    off = w if reverse else w - 1
    if 2 * w > 8:
        pieces = [jnp.broadcast_to(b[p * 2 * w + off:p * 2 * w + off + 1, :], (2 * w, d))
                  for p in range(L // (2 * w))]
        return pieces[0] if len(pieces) == 1 else jnp.concatenate(pieces, axis=0)
    b3 = b.reshape(L // 8, 8, d)
    sub = lax.broadcasted_iota(jnp.int32, b3.shape, 1)
    ref = None
    for p in range(8 // (2 * w)):
        row = jnp.broadcast_to(b3[:, p * 2 * w + off:p * 2 * w + off + 1, :], b3.shape)
        ref = row if ref is None else jnp.where(sub >= p * 2 * w, row, ref)
    return ref.reshape(L, d)


def _hgrn2_kernel(qp_ref, ffp_ref, fbp_ref, ip_ref, gp_ref, lb_ref, nw_ref,
                  trif_ref, trib_ref, mf_ref, mb_ref, o_ref,
                  of_scr, ob_scr, stf_scr, stb_scr, *, L, nl):
    S = qp_ref.shape[0]
    nc = S // L
    lb = lb_ref[...]
    lb_floor = jnp.maximum(lb, LB_FLOOR)
    one_m_lb = 1.0 - lb

    def chunk(c, fpre_ref, tri_ref, m_ref, st_scr, reverse):
        r0 = pl.multiple_of(c * L, L)
        q = _silu(qp_ref[pl.ds(r0, L), :])
        v = ip_ref[pl.ds(r0, L), :]
        f = lb_floor + one_m_lb * _sigmoid(fpre_ref[pl.ds(r0, L), :])
        k = 1.0 - f
        hi, mid, lo = _split3(jnp.log2(f))
        tri = tri_ref[...]
        b = _dot(tri, hi) + _dot(tri, mid) + _dot(tri, lo)
        qb = q.astype(BF16)
        kb = k.astype(BF16)
        a = m_ref[nl] * _dot_nt(qb, kb)
        w = L // 2
        for lvl in range(nl):
            e = jnp.exp2(-jnp.abs(b - _pair_ref(b, w, reverse))).astype(BF16)
            a = a + m_ref[lvl] * _dot_nt(qb * e, kb * e)
            w //= 2
        tot_row = b[0:1] if reverse else b[L - 1:L]
        q_in = (q * jnp.exp2(b)).astype(BF16)
        k_out = (k * jnp.exp2(tot_row - b)).astype(BF16)
        st = st_scr[...]
        o = _dot(a.astype(BF16), v.astype(BF16)) + _dot_nt(q_in, st.astype(BF16))
        st_scr[...] = st * jnp.exp2(tot_row) + _dot(v.T.astype(BF16), k_out)
        return r0, o

    stf_scr[...] = jnp.zeros_like(stf_scr)
    stb_scr[...] = jnp.zeros_like(stb_scr)

    def body(ci, carry):
        r0, o = chunk(ci, ffp_ref, trif_ref, mf_ref, stf_scr, False)
        of_scr[pl.ds(r0, L), :] = o
        r0, o = chunk(nc - 1 - ci, fbp_ref, trib_ref, mb_ref, stb_scr, True)
        ob_scr[pl.ds(r0, L), :] = o
        return carry

    lax.fori_loop(0, nc, body, 0, unroll=4)
    nw = nw_ref[...]
    R = 4 * L if S % (4 * L) == 0 else L

    def finish(i, carry):
        r0 = pl.multiple_of(i * R, R)
        o = of_scr[pl.ds(r0, R), :] + ob_scr[pl.ds(r0, R), :]
        o = o * lax.rsqrt(jnp.mean(o * o, axis=-1, keepdims=True) + RMS_EPS)
        o = o * nw * _sigmoid(gp_ref[pl.ds(r0, R), :])
        o_ref[pl.ds(r0, R), :] = o.astype(o_ref.dtype)
        return carry

    lax.fori_loop(0, S // R, finish, 0)


def hgrn2(proj, lb, norm_w, L=HG_CHUNK):
    B, S, _ = proj.shape
    m_f, m_b, nl = _hgrn2_masks(L)
    t = np.arange(L)
    tri_f = jnp.asarray((t[None, :] <= t[:, None]).astype(np.float32), BF16)
    tri_b = jnp.asarray((t[None, :] >= t[:, None]).astype(np.float32), BF16)
    d = HG_DIM
    nb = HG_WIDTH // d

    def col(off):
        return pl.BlockSpec((None, S, d), lambda b, h, off=off: (b, 0, off // d + h))

    vec = pl.BlockSpec((1, d), lambda b, h: (0, h))
    const2 = lambda a: pl.BlockSpec(a.shape, lambda b, h: (0, 0))
    const3 = lambda a: pl.BlockSpec(a.shape, lambda b, h: (0, 0, 0))
    return pl.pallas_call(
        functools.partial(_hgrn2_kernel, L=L, nl=nl),
        grid=(B, nb),
        in_specs=[col(OFF_Q), col(OFF_FF), col(OFF_FB), col(OFF_I), col(OFF_G), vec, vec,
                  const2(tri_f), const2(tri_b), const3(m_f), const3(m_b)],
        out_specs=pl.BlockSpec((None, S, d), lambda b, h: (b, 0, h)),
        out_shape=jax.ShapeDtypeStruct((B, S, HG_WIDTH), BF16),
        scratch_shapes=[pltpu.VMEM((S, d), F32), pltpu.VMEM((S, d), F32),
                        pltpu.VMEM((d, d), F32), pltpu.VMEM((d, d), F32)],
        compiler_params=_cparams("parallel", "parallel"),
        name="hgrn2",
    )(proj, proj, proj, proj, proj, lb.reshape(1, HG_WIDTH), norm_w.reshape(1, HG_WIDTH),
      tri_f, tri_b, jnp.asarray(m_f), jnp.asarray(m_b))


def _shift_rows(x, k):
    if k == 0:
        return x
    n = x.shape[0]
    rolled = pltpu.roll(x, (-k) % n, 0)
    t = lax.broadcasted_iota(jnp.int32, x.shape, 0)
    ok = (t + k >= 0) & (t + k < n)
    return jnp.where(ok, rolled, 0.0)


def _conv_silu(x, w, b):
    half = w.shape[0] // 2
    acc = b
    for j in range(w.shape[0]):
        acc = acc + w[j:j + 1, :] * _shift_rows(x, j - half)
    return _silu(acc)


def _ssd_kernel(x_ref, b_ref, c_ref, z_ref, dtc_ref, dtr_ref,
                cwx_ref, cwb_ref, cwc_ref, cbx_ref, cbb_ref, cbc_ref,
                dtbc_ref, dtbr_ref, alr_ref, alc_ref, dsk_ref, nw_ref,
                tril_ref, triu_ref,
                o_ref, xs_scr, bs_scr, cs_scr, yf_scr, yb_scr, stf_scr, stb_scr, *, L, hg, P):
    S = x_ref.shape[0]
    nc = S // L
    nh = 2 * hg
    gw = hg * P
    N = bs_scr.shape[1]

    a_row = jnp.exp(alr_ref[...]) * NEG_LOG2E
    a_col = jnp.exp(alc_ref[...]) * NEG_LOG2E
    tril = tril_ref[...]
    triu = triu_ref[...]
    ti = lax.broadcasted_iota(jnp.int32, (L, L), 0)
    si = lax.broadcasted_iota(jnp.int32, (L, L), 1)
    lane2 = lax.broadcasted_iota(jnp.int32, (L, 2 * P), 1)

    def softplus(v):
        return jnp.maximum(v, 0.0) + jnp.log(1.0 + jnp.exp(-jnp.abs(v)))

    def per_head(cols):
        tiles = []
        for j in range(0, hg, 2):
            lo = jnp.broadcast_to(cols[j], (L, 2 * P))
            hi = jnp.broadcast_to(cols[j + 1], (L, 2 * P))
            tiles.append(jnp.where(lane2 < P, lo, hi))
        return jnp.concatenate(tiles, axis=1)

    def chunk(c, d, g, st_scr):
        cum_c = tril if d == 0 else triu
        cum_r = triu if d == 0 else tril
        r0 = pl.multiple_of(c * L, L)
        x = xs_scr[pl.ds(r0, L), :]
        bm = bs_scr[pl.ds(r0, L), :]
        cm = cs_scr[pl.ds(r0, L), :]
        dt_c = softplus(dtc_ref[pl.ds(r0, L), :] + dtbc_ref[...])
        dt_r = softplus(dtr_ref[:, pl.ds(r0, L)] + dtbr_ref[...])
        h1, h2, h3 = _split3(dt_c * a_row)
        acum_c = _dot(cum_c, h1) + _dot(cum_c, h2) + _dot(cum_c, h3)
        g1, g2, g3 = _split3(dt_r * a_col)
        acum_r = _dot(g1, cum_r) + _dot(g2, cum_r) + _dot(g3, cum_r)
        ones_rows = jnp.ones((8, L), BF16)
        tot = (_dot(ones_rows, h1) + _dot(ones_rows, h2) + _dot(ones_rows, h3))[0:1]
        cb = _dot_nt(cm.astype(BF16), bm.astype(BF16))
        keep = (si <= ti) if d == 0 else (si >= ti)
        ys, in_cols, out_cols, dec_cols = [], [], [], []
        for hh in range(hg):
            j = d * nh + g * hg + hh
            ac = acum_c[:, j:j + 1]
            dc = dt_c[:, j:j + 1]
            te = tot[:, j:j + 1]
            ar = acum_r[j:j + 1, :]
            dr = dt_r[j:j + 1, :]
            seg = ac - ar
            decay = jnp.where(keep, jnp.exp2(jnp.where(keep, seg, 0.0)), 0.0)
            w = (cb * decay * dr).astype(BF16)
            ys.append(_dot(w, x[:, hh * P:(hh + 1) * P].astype(BF16)))
            in_cols.append(jnp.exp2(te - ac) * dc)
            out_cols.append(jnp.exp2(ac))
            dec_cols.append(jnp.broadcast_to(jnp.exp2(te), (1, P)))
        y_diag = jnp.concatenate(ys, axis=1)
        st = st_scr[...]
        y_off = _dot(cm.astype(BF16), st.astype(BF16)) * per_head(out_cols)
        x_in = (x * per_head(in_cols)).astype(BF16)
        st_scr[...] = st * jnp.concatenate(dec_cols, axis=1) + _dot(bm.T.astype(BF16), x_in)
        return r0, y_diag + y_off

    R = 2 * L
    for g in range(x_ref.shape[1] // gw):
        cols = slice(g * gw, (g + 1) * gw)
        ncols = slice(g * N, (g + 1) * N)
        xs_scr[...] = _conv_silu(x_ref[:, cols], cwx_ref[:, cols], cbx_ref[:, cols])
        bs_scr[...] = _conv_silu(b_ref[:, ncols], cwb_ref[:, ncols], cbb_ref[:, ncols])
        cs_scr[...] = _conv_silu(c_ref[:, ncols], cwc_ref[:, ncols], cbc_ref[:, ncols])
        stf_scr[...] = jnp.zeros_like(stf_scr)
        stb_scr[...] = jnp.zeros_like(stb_scr)

        def body(ci, carry, g=g):
            r0, y = chunk(ci, 0, g, stf_scr)
            yf_scr[pl.ds(r0, L), :] = y
            r0, y = chunk(nc - 1 - ci, 1, g, stb_scr)
            yb_scr[pl.ds(r0, L), :] = y
            return carry

        lax.fori_loop(0, nc, body, 0, unroll=2)

        def finish(i, carry, cols=cols):
            r0 = pl.multiple_of(i * R, R)
            y = (yf_scr[pl.ds(r0, R), :] + yb_scr[pl.ds(r0, R), :]
                 + dsk_ref[:, cols] * xs_scr[pl.ds(r0, R), :])
            y = y * _silu(z_ref[pl.ds(r0, R), cols])
            y = y * lax.rsqrt(jnp.mean(y * y, axis=-1, keepdims=True) + RMS_EPS)
            o_ref[pl.ds(r0, R), cols] = (y * nw_ref[:, cols]).astype(o_ref.dtype)
            return carry

        lax.fori_loop(0, S // R, finish, 0)


def ssd(proj, dt_rows, conv_w, conv_b, dt_bias, a_log, d_skip, norm_w, L=M2_CHUNK):
    B, S, _ = proj.shape
    hg = M2_HEADS // M2_GROUPS
    gw = hg * M2_P
    t = np.arange(L)
    tril = jnp.asarray((t[None, :] <= t[:, None]).astype(np.float32), BF16)
    triu = jnp.asarray((t[None, :] >= t[:, None]).astype(np.float32), BF16)
    nb_c = OFF_XBC + M2_INNER
    nc_c = nb_c + M2_GROUPS * M2_N
    xw = conv_w[:, :M2_INNER]
    bw = conv_w[:, M2_INNER:M2_INNER + M2_GROUPS * M2_N]
    cw = conv_w[:, M2_INNER + M2_GROUPS * M2_N:]
    cb2 = conv_b.reshape(1, -1)
    xb = cb2[:, :M2_INNER]
    bb = cb2[:, M2_INNER:M2_INNER + M2_GROUPS * M2_N]
    cbb = cb2[:, M2_INNER + M2_GROUPS * M2_N:]
    nh2 = 2 * M2_HEADS
    dtb_row = jnp.zeros((1, LANE), F32).at[0, :nh2].set(dt_bias.reshape(-1))
    al_row = jnp.zeros((1, LANE), F32).at[0, :nh2].set(a_log.reshape(-1))
    dtb_col = dt_bias.reshape(nh2, 1)
    al_col = a_log.reshape(nh2, 1)
    dsk = jnp.repeat(d_skip, M2_P).reshape(1, M2_INNER)
    nw = norm_w.reshape(1, M2_INNER)

    gn = M2_GROUPS * M2_N
    full2 = lambda a: pl.BlockSpec(a.shape, lambda b: (0, 0))
    return pl.pallas_call(
        functools.partial(_ssd_kernel, L=L, hg=hg, P=M2_P),
        grid=(B,),
        in_specs=[
            pl.BlockSpec((None, S, M2_INNER), lambda b: (b, 0, OFF_XBC // M2_INNER)),
            pl.BlockSpec((None, S, gn), lambda b: (b, 0, nb_c // gn)),
            pl.BlockSpec((None, S, gn), lambda b: (b, 0, nc_c // gn)),
            pl.BlockSpec((None, S, M2_INNER), lambda b: (b, 0, OFF_Z // M2_INNER)),
            pl.BlockSpec((None, S, LANE), lambda b: (b, 0, OFF_DT // LANE)),
            pl.BlockSpec((nh2, S), lambda b: (0, b)),
            full2(xw), full2(bw), full2(cw), full2(xb), full2(bb), full2(cbb),
            full2(dtb_row), full2(dtb_col), full2(al_row), full2(al_col),
            full2(dsk), full2(nw), full2(tril), full2(triu),
        ],
        out_specs=pl.BlockSpec((None, S, M2_INNER), lambda b: (b, 0, 0)),
        out_shape=jax.ShapeDtypeStruct((B, S, M2_INNER), BF16),
        scratch_shapes=[pltpu.VMEM((S, gw), F32), pltpu.VMEM((S, M2_N), F32),
                        pltpu.VMEM((S, M2_N), F32), pltpu.VMEM((S, gw), F32),
                        pltpu.VMEM((S, gw), F32), pltpu.VMEM((M2_N, gw), F32),
                        pltpu.VMEM((M2_N, gw), F32)],
        compiler_params=_cparams("parallel"),
        name="ssd",
    )(proj, proj, proj, proj, proj, dt_rows, xw, bw, cw, xb, bb, cbb,
      dtb_row, dtb_col, al_row, al_col, dsk, nw, tril, triu)


def _ffn_kernel(h_ref, oa_ref, ob_ref, wa_ref, wb_ref, nw_ref, wg_ref, wu_ref, wd_ref, o_ref,
                xn_scr, acc_scr):
    f = pl.program_id(1)

    @pl.when(f == 0)
    def _():
        x = h_ref[...] + _dot(oa_ref[...], wa_ref[...]) + _dot(ob_ref[...], wb_ref[...])
        xn_scr[...] = _rms(x, nw_ref[...]).astype(BF16)
        acc_scr[...] = x

    xn = xn_scr[...]
    act = _silu(_dot(xn, wg_ref[...].astype(BF16))) * _dot(xn, wu_ref[...].astype(BF16))
    acc_scr[...] += _dot(act.astype(BF16), wd_ref[...].astype(BF16))

    @pl.when(f == pl.num_programs(1) - 1)
    def _():
        o_ref[...] = acc_scr[...]


def ffn(h, oa, ob, w_out, nw, w_gu, w_down, layer, tm=1024, tf=512):
    m, d = h.shape
    dff = w_down.shape[1]
    nf = dff // tf
    ka, kb = oa.shape[1], ob.shape[1]
    assert ka == kb
    return pl.pallas_call(
        _ffn_kernel,
        grid=(m // tm, nf),
        in_specs=[pl.BlockSpec((tm, d), lambda i, f: (i, 0)),
                  pl.BlockSpec((tm, ka), lambda i, f: (i, 0)),
                  pl.BlockSpec((tm, kb), lambda i, f: (i, 0)),
                  pl.BlockSpec((None, ka, d), lambda i, f: (layer, 0, 0)),
                  pl.BlockSpec((None, kb, d), lambda i, f: (layer, 1, 0)),
                  pl.BlockSpec((1, d), lambda i, f: (0, 0)),
                  pl.BlockSpec((None, d, tf), lambda i, f: (layer, 0, f)),
                  pl.BlockSpec((None, d, tf), lambda i, f: (layer, 0, nf + f)),
                  pl.BlockSpec((None, tf, d), lambda i, f: (layer, f, 0))],
        out_specs=pl.BlockSpec((tm, d), lambda i, f: (i, 0)),
        out_shape=jax.ShapeDtypeStruct((m, d), F32),
        scratch_shapes=[pltpu.VMEM((tm, d), BF16), pltpu.VMEM((tm, d), F32)],
        compiler_params=_cparams("parallel", "arbitrary"),
        name="ffn",
    )(h, oa, ob, w_out, w_out, nw.reshape(1, d), w_gu, w_gu, w_down)


def _ple_kernel(*refs, moe, final):
    if moe:
        h_ref, y2_ref, meta_ref, p_ref, nw_ref, wg_ref, wp_ref, fw_ref, o_ref = refs
        meta = meta_ref[...]
        d = h_ref.shape[1]
        h = h_ref[...] + (meta[:, 2:3] * y2_ref[:, 0:d] + meta[:, 3:4] * y2_ref[:, d:2 * d])
    else:
        h_ref, p_ref, nw_ref, wg_ref, wp_ref, fw_ref, o_ref = refs
        h = h_ref[...]
    gate = _sigmoid(_dot(_rms(h, nw_ref[...]).astype(BF16), wg_ref[...]))
    h = h + gate * _dot(p_ref[...].astype(BF16), wp_ref[...])
    if final:
        h = _rms(h, fw_ref[...])
    o_ref[...] = h


def ple(h, p, nw, wg, wp, fw, layer, y2=None, meta=None, final=False, tm=512):
    m, d = h.shape
    moe = y2 is not None
    row = lambda w: pl.BlockSpec((tm, w), lambda i: (i, 0))
    slab = lambda a: pl.BlockSpec((None,) + a.shape[1:], lambda i: (layer, 0, 0))
    vec = pl.BlockSpec((1, d), lambda i: (0, 0))
    in_specs = [row(d)]
    args = [h]
    if moe:
        in_specs += [row(2 * d), row(meta.shape[1])]
        args += [y2, meta]
    in_specs += [pl.BlockSpec((None, tm, p.shape[2]), lambda i: (layer, i, 0)), vec, slab(wg), slab(wp), vec]
    args += [p, nw.reshape(1, d), wg, wp, fw.reshape(1, d)]
    return pl.pallas_call(
        functools.partial(_ple_kernel, moe=moe, final=final),
        grid=(m // tm,),
        in_specs=in_specs,
        out_specs=row(d),
        out_shape=jax.ShapeDtypeStruct((m, d), F32),
        compiler_params=_cparams("parallel"),
        name="ple",
    )(*args)


def _s5_strips(a_re, a_im, log_step, b_re, b_im, c_re, c_im):
    G, N = a_re.shape[1:]
    C = S5_GROUP_SIZE
    T = S5_STEPS
    gt = LANE // C
    Z = G // gt
    tau = jnp.arange(T + 1, dtype=F32)
    steps = jnp.arange(T)

    def cmul(xr, xi, yr, yi):
        return xr * yr - xi * yi, xr * yi + xi * yr

    st_parts, rd_parts, k_parts, lam_rows = [], [], [], []
    for d in range(2):
        delta = jnp.exp(log_step[d])[:, None]
        ar, ai = a_re[d], a_im[d]
        mag = jnp.exp(ar * delta)
        lam_re, lam_im = mag * jnp.cos(ai * delta), mag * jnp.sin(ai * delta)
        den = ar * ar + ai * ai
        num_re = lam_re - 1.0
        coef_re = (num_re * ar + lam_im * ai) / den
        coef_im = (lam_im * ar - num_re * ai) / den
        br = coef_re[..., None] * b_re - coef_im[..., None] * b_im
        bi = coef_re[..., None] * b_im + coef_im[..., None] * b_re
        cr, ci = c_re[d], c_im[d]
        pm = jnp.exp((ar * delta)[None] * tau[:, None, None])
        ang = (ai * delta)[None] * tau[:, None, None]
        pr, pi = pm * jnp.cos(ang), pm * jnp.sin(ang)
        e_in = (T - 1 - steps) if d == 0 else steps
        sr, si = cmul(pr[e_in][..., None], pi[e_in][..., None], br[None], bi[None])
        e_out = (steps + 1) if d == 0 else (T - steps)
        cpr, cpi = cmul(cr[None], ci[None], pr[e_out][:, :, None, :], pi[e_out][:, :, None, :])
        lbr, lbi = cmul(pr[:T][..., None], pi[:T][..., None], br[None], bi[None])
        ktau = jnp.einsum('gon,tgni->tgoi', cr, lbr) - jnp.einsum('gon,tgni->tgoi', ci, lbi)
        st_parts += [sr, si]
        rd_parts += [cpr, -cpi]
        k_parts.append(ktau)
        lam_rows += [pr[T], pi[T]]

    NS = gt * N
    st = jnp.stack(st_parts).reshape(4, T, Z, gt, N, C).transpose(2, 1, 5, 0, 3, 4)
    rd = jnp.stack(rd_parts).reshape(4, T, Z, gt, C, N).transpose(2, 1, 4, 0, 3, 5)
    lag = steps[None, :] - steps[:, None]
    sel_f = (lag[:, :, None] == steps[None, None, :]).astype(F32)
    sel_b = (-lag[:, :, None] == steps[None, None, :]).astype(F32)
    toe = (jnp.einsum('stk,kgoi->stgoi', sel_f, k_parts[0])
           + jnp.einsum('stk,kgoi->stgoi', sel_b, k_parts[1]))
    toe = toe.reshape(T, T, Z, gt, C, C).transpose(2, 0, 5, 1, 3, 4)
    lam = jnp.stack([r.reshape(Z, NS) for r in lam_rows], axis=1)
    lam = jnp.concatenate([lam, jnp.zeros_like(lam)], axis=1)
    return (st.reshape(Z, T * C, 4 * NS), toe.reshape(Z, T * C, T * LANE),
            rd.reshape(Z, T * C, 4 * NS), lam)


def s5_tables(a_re, a_im, log_step, b_re, b_im, c_re, c_im):
    st, toe, rd, lam = jax.vmap(_s5_strips)(a_re, a_im, log_step, b_re, b_im, c_re, c_im)
    flat = lambda a: a.reshape((-1,) + a.shape[2:])
    wst = block_diag_rows(flat(st), S5_GROUP_SIZE, S5_STATE)
    kin = block_diag_rows(flat(toe), S5_GROUP_SIZE, S5_GROUP_SIZE)
    wc = block_diag_rows(flat(rd), S5_GROUP_SIZE, S5_STATE)
    return wst, kin, wc, flat(lam)


def _block_diag_kernel(s_ref, o_ref, *, C, gcol):
    n = o_ref.shape[1]
    gt = LANE // C
    row_g = lax.broadcasted_iota(jnp.int32, (LANE, n), 0) // C
    col_g = (lax.broadcasted_iota(jnp.int32, (LANE, n), 1) // gcol) % gt
    keep = row_g == col_g
    for t in range(o_ref.shape[0] // LANE):
        strip = s_ref[t * C:(t + 1) * C, :]
        tiled = jnp.broadcast_to(strip[None], (gt, C, n)).reshape(LANE, n)
        o_ref[t * LANE:(t + 1) * LANE, :] = jnp.where(keep, tiled, 0.0).astype(o_ref.dtype)


def block_diag_rows(strips, C, gcol):
    Z, rows, n = strips.shape
    gt = LANE // C
    return pl.pallas_call(
        functools.partial(_block_diag_kernel, C=C, gcol=gcol),
        grid=(Z,),
        in_specs=[pl.BlockSpec((None, rows, n), lambda z: (z, 0, 0))],
        out_specs=pl.BlockSpec((None, rows * gt, n), lambda z: (z, 0, 0)),
        out_shape=jax.ShapeDtypeStruct((Z, rows * gt, n), BF16),
        compiler_params=_cparams("parallel"),
        name="block_diag_rows",
    )(strips)


def _s5_pre_kernel(x_ref, nw_ref, u_ref, xn_scr, tmp_scr):
    nb, R, D = x_ref.shape
    T = S5_STEPS
    rc = R // T
    xn = _rms(x_ref[...], nw_ref[...])
    for z in range(D // LANE):
        xn_scr[z] = xn[:, :, z * LANE:(z + 1) * LANE]
    for z in range(D // LANE):
        for t in range(T):
            for b in range(nb):
                tmp_scr[z, t, pl.ds(b, rc, stride=nb), :] = xn_scr[z, b, pl.ds(t, rc, stride=T), :]
            u_ref[z, :, t * LANE:(t + 1) * LANE] = tmp_scr[z, t].astype(u_ref.dtype)


def s5_pre(h3, nw, R=64):
    B, S, D = h3.shape
    T = S5_STEPS
    Z = D // LANE
    rc = R // T
    return pl.pallas_call(
        _s5_pre_kernel,
        grid=(S // R,),
        in_specs=[pl.BlockSpec((B, R, D), lambda i: (0, i, 0)),
                  pl.BlockSpec((1, D), lambda i: (0, 0))],
        out_specs=pl.BlockSpec((Z, rc * B, T * LANE), lambda i: (0, i, 0)),
        out_shape=jax.ShapeDtypeStruct((Z, S // T * B, T * LANE), BF16),
        scratch_shapes=[pltpu.VMEM((Z, B, R, LANE), F32), pltpu.VMEM((Z, T, rc * B, LANE), F32)],
        compiler_params=_cparams("parallel"),
        name="s5_pre",
    )(h3, nw.reshape(1, D))


def _s5_kernel(u_ref, wst_ref, kin_ref, wc_ref, lam_ref, y_ref, s_scr, *, nb, rblk):
    rows = u_ref.shape[0]
    nc = rows // nb
    ns = lam_ref.shape[1]
    for r0 in range(0, rows, rblk):
        s_scr[r0:r0 + rblk, :] = _dot(u_ref[r0:r0 + rblk, :], wst_ref[...])
    lam = lam_ref[...]
    lfr, lfi, lbr, lbi = lam[0:1], lam[1:2], lam[2:3], lam[3:4]

    def body(ci, carry):
        hfr, hfi, hbr, hbi = carry
        rf = pl.multiple_of(ci * nb, nb)
        rb = pl.multiple_of((nc - 1 - ci) * nb, nb)
        sfr = s_scr[pl.ds(rf, nb), 0:ns]
        sfi = s_scr[pl.ds(rf, nb), ns:2 * ns]
        sbr = s_scr[pl.ds(rb, nb), 2 * ns:3 * ns]
        sbi = s_scr[pl.ds(rb, nb), 3 * ns:4 * ns]
        s_scr[pl.ds(rf, nb), 0:ns] = hfr
        s_scr[pl.ds(rf, nb), ns:2 * ns] = hfi
        s_scr[pl.ds(rb, nb), 2 * ns:3 * ns] = hbr
        s_scr[pl.ds(rb, nb), 3 * ns:4 * ns] = hbi
        return (lfr * hfr - lfi * hfi + sfr, lfr * hfi + lfi * hfr + sfi,
                lbr * hbr - lbi * hbi + sbr, lbr * hbi + lbi * hbr + sbi)

    z = jnp.zeros((nb, ns), F32)
    lax.fori_loop(0, nc, body, (z, z, z, z))
    for r0 in range(0, rows, rblk):
        y = (_dot(u_ref[r0:r0 + rblk, :], kin_ref[...])
             + _dot_nt(s_scr[r0:r0 + rblk, :].astype(BF16), wc_ref[...]))
        y_ref[r0:r0 + rblk, :] = y.astype(y_ref.dtype)


def s5_scan(u, wst, kin, wc, lam, nb, layer, rblk=512):
    Z, rows, K = u.shape
    z0 = layer * Z
    per = lambda a: pl.BlockSpec((None,) + a.shape[1:], lambda s: (s, 0, 0))
    tab = lambda a: pl.BlockSpec((None,) + a.shape[1:], lambda s: (z0 + s, 0, 0))
    once = lambda a: pl.BlockSpec((None,) + a.shape[1:], lambda s: (z0 + s, 0, 0),
                                  pipeline_mode=pl.Buffered(1))
    return pl.pallas_call(
        functools.partial(_s5_kernel, nb=nb, rblk=rblk),
        grid=(Z,),
        in_specs=[per(u), once(wst), once(kin), once(wc), tab(lam)],
        out_specs=pl.BlockSpec((None, rows, K), lambda s: (s, 0, 0)),
        out_shape=jax.ShapeDtypeStruct((Z, rows, K), BF16),
        scratch_shapes=[pltpu.VMEM((rows, wst.shape[2]), F32)],
        compiler_params=_cparams("parallel"),
        name="s5_scan",
    )(u, wst, kin, wc, lam)


def _s5_post_kernel(h_ref, y_ref, nw_ref, d_ref, wo_ref, wg_ref, o_ref, yf_scr, yt_scr):
    nb, R, D = h_ref.shape
    T = S5_STEPS
    rc = R // T
    for z in range(D // LANE):
        for t in range(T):
            yf_scr[z, t] = y_ref[z, :, t * LANE:(t + 1) * LANE].astype(F32)
            for b in range(nb):
                yt_scr[z, b, pl.ds(t, rc, stride=T), :] = yf_scr[z, t, pl.ds(b, rc, stride=nb), :]
    yt = jnp.concatenate([yt_scr[z] for z in range(D // LANE)], axis=-1)
    h = h_ref[...].reshape(nb * R, D)
    y = yt.reshape(nb * R, D) + d_ref[...] * _rms(h, nw_ref[...])
    act = jax.nn.gelu(y).astype(BF16)
    out = h + _dot(act, wo_ref[...]) * _sigmoid(_dot(act, wg_ref[...]))
    o_ref[...] = out.reshape(nb, R, D)


def s5_post(h3, y, nw, d_skip, glu_w, layer, R=64):
    B, S, D = h3.shape
    T = S5_STEPS
    Z = D // LANE
    rc = R // T
    row = pl.BlockSpec((B, R, D), lambda i: (0, i, 0))
    vec = pl.BlockSpec((1, D), lambda i: (0, 0))
    return pl.pallas_call(
        _s5_post_kernel,
        grid=(S // R,),
        in_specs=[row, pl.BlockSpec((Z, rc * B, T * LANE), lambda i: (0, i, 0)), vec, vec,
                  pl.BlockSpec((None, D, D), lambda i: (layer, 0, 0)),
                  pl.BlockSpec((None, D, D), lambda i: (layer, 0, 1))],
        out_specs=row,
        out_shape=jax.ShapeDtypeStruct((B, S, D), F32),
        scratch_shapes=[pltpu.VMEM((Z, T, rc * B, LANE), F32), pltpu.VMEM((Z, B, R, LANE), F32)],
        compiler_params=_cparams("parallel"),
        name="s5_post",
    )(h3, y, nw.reshape(1, D), d_skip.reshape(1, D), glu_w, glu_w)


def _router_kernel(h_ref, nw_ref, whi_ref, wlo_ref, tri_ref, meta_ref, cnt_ref, carry_scr):
    @pl.when(pl.program_id(0) == 0)
    def _():
        carry_scr[...] = jnp.zeros_like(carry_scr)

    xn = _rms(h_ref[...], nw_ref[...])
    x_hi, x_lo, _ = _split3(xn)
    w_hi, w_lo = whi_ref[...], wlo_ref[...]
    logits = _dot(x_hi, w_hi) + (_dot(x_hi, w_lo) + _dot(x_lo, w_hi))
    lane = lax.broadcasted_iota(jnp.int32, logits.shape, 1)
    neg = jnp.float32(-jnp.inf)
    lg = jnp.where(lane < N_EXPERTS, logits, neg)
    t1 = jnp.max(lg, axis=1, keepdims=True)
    i1 = jnp.min(jnp.where(lg == t1, lane, LANE), axis=1, keepdims=True)
    lg2 = jnp.where(lane == i1, neg, lg)
    t2 = jnp.max(lg2, axis=1, keepdims=True)
    i2 = jnp.min(jnp.where(lg2 == t2, lane, LANE), axis=1, keepdims=True)
    ex = jnp.exp(t2 - t1)
    g1 = 1.0 / (1.0 + ex)
    g2 = ex / (1.0 + ex)
    oh1 = jnp.where(lane == i1, 1.0, 0.0)
    oh2 = jnp.where(lane == i2, 1.0, 0.0)
    tri = tri_ref[...]
    before1 = _dot(tri, oh1.astype(BF16))
    before2 = _dot(tri, oh2.astype(BF16))
    tot1 = jnp.sum(oh1, axis=0, keepdims=True)
    tot2 = jnp.sum(oh2, axis=0, keepdims=True)
    carry = carry_scr[0:1]
    rank1 = jnp.sum(oh1 * (carry + before1), axis=1, keepdims=True)
    rank2 = jnp.sum(oh2 * (carry + tot1 + before2), axis=1, keepdims=True)
    counts = jnp.broadcast_to(carry + tot1 + tot2, carry_scr.shape)
    carry_scr[...] = counts
    cnt_ref[...] = counts
    meta = jnp.where(lane == 0, i1.astype(F32), 0.0)
    meta = jnp.where(lane == 1, i2.astype(F32), meta)
    meta = jnp.where(lane == 2, g1, meta)
    meta = jnp.where(lane == 3, g2, meta)
    meta = jnp.where(lane == 4, rank1, meta)
    meta = jnp.where(lane == 5, rank2, meta)
    meta_ref[...] = meta


def router(h, nw, w_router, tm=512):
    m, d = h.shape
    wr = jnp.zeros((d, LANE), F32).at[:, :N_EXPERTS].set(w_router)
    w_hi, w_lo, _ = _split3(wr)
    t = np.arange(tm)
    tri = jnp.asarray((t[None, :] < t[:, None]).astype(np.float32), BF16)
    return pl.pallas_call(
        _router_kernel,
        grid=(m // tm,),
        in_specs=[pl.BlockSpec((tm, d), lambda i: (i, 0)),
                  pl.BlockSpec((1, d), lambda i: (0, 0)),
                  pl.BlockSpec((d, LANE), lambda i: (0, 0)),
                  pl.BlockSpec((d, LANE), lambda i: (0, 0)),
                  pl.BlockSpec((tm, tm), lambda i: (0, 0))],
        out_specs=[pl.BlockSpec((tm, LANE), lambda i: (i, 0)),
                   pl.BlockSpec((8, LANE), lambda i: (0, 0))],
        out_shape=[jax.ShapeDtypeStruct((m, LANE), F32), jax.ShapeDtypeStruct((8, LANE), F32)],
        scratch_shapes=[pltpu.VMEM((8, LANE), F32)],
        compiler_params=_cparams("arbitrary"),
        name="router",
    )(h, nw.reshape(1, d), w_hi, w_lo, tri)


def _moe_kernel(arow_ref, blk_e_ref, nvalid_ref, h_hbm, nw_ref, wg_ref, wu_ref, wd_ref,
                y2_hbm, xg_scr, xn_scr, acc_scr, sem_in, sem_out, *, bm, n_tok):
    g = pl.program_id(0)
    f = pl.program_id(1)
    nf = pl.num_programs(1)
    slot = g % 2
    other = 1 - slot
    d = acc_scr.shape[2]
    rows_per_step = bm // (nf - 1)
    prv, cur, nxt = g * bm, (g + 1) * bm, (g + 2) * bm
    nv = nvalid_ref[g]

    def in_copy(off, r, s):
        tok = jnp.minimum(arow_ref[off + r] >> 1, n_tok - 1)
        return pltpu.make_async_copy(h_hbm.at[pl.ds(tok, 1)], xg_scr.at[s, pl.ds(r, 1)], sem_in.at[s])

    def out_copy(off, r, s):
        a = arow_ref[off + r]
        col = pl.multiple_of((a & 1) * d, d)
        return pltpu.make_async_copy(acc_scr.at[s, pl.ds(r, 1)],
                                     y2_hbm.at[pl.ds(a >> 1, 1), pl.ds(col, d)], sem_out.at[s])

    def wait_in(s):
        pltpu.make_async_copy(h_hbm.at[pl.ds(0, bm)], xg_scr.at[s], sem_in.at[s]).wait()

    def wait_out(s):
        pltpu.make_async_copy(acc_scr.at[s], y2_hbm.at[pl.ds(0, bm), pl.ds(0, d)], sem_out.at[s]).wait()

    @pl.when((g == 0) & (f == 0))
    def _():
        acc_scr[...] = jnp.zeros_like(acc_scr)

        def start(r, c):
            in_copy(cur, r, slot).start()
            return c

        lax.fori_loop(0, bm, start, 0)

    @pl.when(f == 0)
    def _():
        wait_in(slot)
        xn_scr[...] = _rms(xg_scr[slot], nw_ref[...]).astype(BF16)
        acc_scr[slot] = jnp.zeros((bm, d), F32)

    def move_rows(part, parts):
        base = pl.multiple_of(f * rows_per_step, 8)
        per = rows_per_step // parts
        for j in range(part * per, (part + 1) * per):
            in_copy(nxt, base + j, other).start()
            out_copy(prv, base + j, other).start()

    def experts(with_moves):
        parts = 2
        tc = wg_ref.shape[2] // parts
        xn = xn_scr[...]
        upd = None
        for c in range(parts):
            cs = slice(c * tc, (c + 1) * tc)
            act = (_silu(_dot(xn, wg_ref[0, :, cs].astype(BF16)))
                   * _dot(xn, wu_ref[0, :, cs].astype(BF16)))
            part = _dot(act.astype(BF16), wd_ref[0, cs, :].astype(BF16))
            upd = part if upd is None else upd + part
            if with_moves:
                move_rows(c, parts)
        acc_scr[slot] += upd

    @pl.when((nv > 0) & (f < nf - 1))
    def _():
        experts(True)

    @pl.when((nv > 0) & (f == nf - 1))
    def _():
        experts(False)

    @pl.when((nv == 0) & (f < nf - 1))
    def _():
        move_rows(0, 1)

    @pl.when(f == nf - 1)
    def _():
        wait_out(other)

    @pl.when((f == nf - 1) & (g == pl.num_programs(0) - 1))
    def _():
        wait_in(other)


def moe_experts(h, nw, w_gu, w_down, arow, blk_e, nvalid, layer, bm=MOE_BM, tf=512):
    t, d = h.shape
    dff = w_down.shape[1]
    nf = dff // tf
    assert bm % (2 * (nf - 1)) == 0 and bm % 16 == 0
    n_steps = blk_e.shape[0]
    e0 = layer * N_EXPERTS

    def wmap(col0):
        def index(g, f, ar, be, nv):
            return (e0 + be[g], 0, col0 + jnp.where(nv[g] > 0, f, nf - 1))
        return index

    def dmap(g, f, ar, be, nv):
        return (e0 + be[g], jnp.where(nv[g] > 0, f, nf - 1), 0)

    grid_spec = pltpu.PrefetchScalarGridSpec(
        num_scalar_prefetch=3,
        grid=(n_steps, nf),
        in_specs=[pl.BlockSpec(memory_space=pl.ANY),
                  pl.BlockSpec((1, d), lambda g, f, ar, be, nv: (0, 0)),
                  pl.BlockSpec((1, d, tf), wmap(0)),
                  pl.BlockSpec((1, d, tf), wmap(nf)),
                  pl.BlockSpec((1, tf, d), dmap)],
        out_specs=pl.BlockSpec(memory_space=pl.ANY),
        scratch_shapes=[pltpu.VMEM((2, bm, d), F32), pltpu.VMEM((bm, d), BF16),
                        pltpu.VMEM((2, bm, d), F32),
                        pltpu.SemaphoreType.DMA((2,)), pltpu.SemaphoreType.DMA((2,))],
    )
    return pl.pallas_call(
        functools.partial(_moe_kernel, bm=bm, n_tok=t),
        grid_spec=grid_spec,
        out_shape=jax.ShapeDtypeStruct((t + bm // 2, 2 * d), F32),
        compiler_params=_cparams("arbitrary", "arbitrary"),
        name="moe_experts",
    )(arow, blk_e, nvalid, h, nw.reshape(1, d), w_gu, w_gu, w_down)


def _moe_plan(meta, counts, bm):
    t = meta.shape[0]
    n_assign = 2 * t
    experts = jnp.arange(N_EXPERTS, dtype=jnp.int32)
    cnt = counts[0, :N_EXPERTS].astype(jnp.int32)
    padded = (cnt + bm - 1) // bm * bm
    pend = jnp.cumsum(padded)
    pstart = pend - padded
    e = meta[:, 0:2].astype(jnp.int32)
    rank = meta[:, 4:6].astype(jnp.int32)
    dest = jnp.sum(jnp.where(e[..., None] == experts, pstart, 0), axis=-1) + rank
    n_blocks = -(-n_assign // bm) + N_EXPERTS
    spare = 2 * t + jnp.arange(bm, dtype=jnp.int32)
    arow = jnp.tile(spare, n_blocks).at[dest.reshape(-1)].set(
        jnp.arange(n_assign, dtype=jnp.int32), unique_indices=True, mode='promise_in_bounds')
    arow = jnp.concatenate([spare, arow, spare, spare])
    blk_start = jnp.arange(n_blocks, dtype=jnp.int32) * bm
    blk_e = jnp.minimum(jnp.sum((pend[None, :] <= blk_start[:, None]).astype(jnp.int32), axis=1),
                        N_EXPERTS - 1)
    nvalid = jnp.clip((pstart + cnt)[blk_e] - blk_start, 0, bm)
    nvalid = jnp.where(blk_start < pend[-1], nvalid, 0).astype(jnp.int32)
    blk_e = jnp.concatenate([blk_e, blk_e[-1:]]).astype(jnp.int32)
    nvalid = jnp.concatenate([nvalid, jnp.zeros((1,), jnp.int32)])
    return arow, blk_e, nvalid


def kernel(x, p, norm_mix, norm_ffn, norm_ple, final_norm, ple_gate, ple_proj, ev_w_in, ev_w_out,
           hg_lb_logits, hg_norm_w, m2_conv_w, m2_conv_b, m2_dt_bias, m2_a_log, m2_d, m2_norm_w,
           s5_a_re, s5_a_im, s5_log_step, s5_b_re, s5_b_im, s5_c_re, s5_c_im, s5_d, s5_glu_w,
           ffn_w_gu, ffn_w_down, moe_router, moe_w_gu, moe_w_down):
    B, S, D = x.shape
    T = B * S
    depth = norm_mix.shape[0]
    lb_soft = jax.nn.softmax(hg_lb_logits.astype(F32), axis=0)
    hg_lb = jnp.cumsum(lb_soft, axis=0) - lb_soft[0]
    w_in = jnp.pad(ev_w_in, ((0, 0), (0, 0), (0, EVEN_IN_PAD - ev_w_in.shape[2]))).astype(BF16)
    w_dt_t = jnp.swapaxes(ev_w_in[:, :, OFF_DT:], 1, 2).astype(BF16)
    w_out = ev_w_out.astype(BF16)
    w_moe_gu = moe_w_gu.reshape((-1,) + moe_w_gu.shape[2:])
    w_moe_down = moe_w_down.reshape((-1,) + moe_w_down.shape[2:])
    w_glu = s5_glu_w.astype(BF16)
    w_ple_gate, w_ple_proj = ple_gate.astype(BF16), ple_proj.astype(BF16)
    p_rows = p.reshape(depth, T, p.shape[-1])
    wst, kin, wc, lam = s5_tables(s5_a_re, s5_a_im, s5_log_step, s5_b_re, s5_b_im, s5_c_re, s5_c_im)
    h = x.reshape(T, D)
    for layer in range(depth):
        j = layer // 2
        if layer % 2 == 0:
            proj, dt_rows = rms_matmul(h, norm_mix[layer], w_in, w_dt_t, j)
            proj = proj.reshape(B, S, EVEN_IN_PAD)
            o_a = hgrn2(proj, hg_lb[j], hg_norm_w[j])
            o_b = ssd(proj, dt_rows, m2_conv_w[j], m2_conv_b[j], m2_dt_bias[j], m2_a_log[j],
                      m2_d[j], m2_norm_w[j])
            h = ffn(h, o_a.reshape(T, HG_WIDTH), o_b.reshape(T, M2_INNER), w_out,
                    norm_ffn[layer], ffn_w_gu, ffn_w_down, j)
            y2 = meta = None
        else:
            h3 = h.reshape(B, S, D)
            y = s5_scan(s5_pre(h3, norm_mix[layer]), wst, kin, wc, lam, B, j)
            h = s5_post(h3, y, norm_mix[layer], s5_d[j], w_glu, j).reshape(T, D)
            meta, counts = router(h, norm_ffn[layer], moe_router[j])
            arow, blk_e, nvalid = _moe_plan(meta, counts, MOE_BM)
            y2 = moe_experts(h, norm_ffn[layer], w_moe_gu, w_moe_down, arow, blk_e, nvalid, j)
        h = ple(h, p_rows, norm_ple[layer], w_ple_gate, w_ple_proj, final_norm, layer,
                y2=y2, meta=meta, final=(layer == depth - 1))
    return h.reshape(B, S, D)
```

```python
import functools

import numpy as np
import jax
import jax.numpy as jnp
from jax import lax
from jax.experimental import pallas as pl
from jax.experimental.pallas import tpu as pltpu

F32 = jnp.float32
BF16 = jnp.bfloat16

RMS_EPS = 1e-6
LB_FLOOR = 1e-30
NEG_LOG2E = -1.4426950408889634
LANE = 128
VMEM_LIMIT = 56 * 1024 * 1024

HG_HEADS = 4
HG_DIM = 128
HG_WIDTH = HG_HEADS * HG_DIM
HG_CHUNK = 128
M2_HEADS = 8
M2_P = 64
M2_INNER = M2_HEADS * M2_P
M2_GROUPS = 2
M2_N = 128
M2_CHUNK = 128
M2_XBC = M2_INNER + 2 * M2_GROUPS * M2_N
S5_GROUP_SIZE = 16
S5_STATE = 64
S5_STEPS = 8
N_EXPERTS = 8
MOE_BM = 576

OFF_Q, OFF_FF, OFF_FB, OFF_I, OFF_G = (k * HG_WIDTH for k in range(5))
OFF_Z = 5 * HG_WIDTH
OFF_XBC = OFF_Z + M2_INNER
OFF_DT = OFF_XBC + M2_XBC
EVEN_IN_PAD = OFF_DT + LANE


def _cparams(*sem):
    return pltpu.CompilerParams(dimension_semantics=sem, vmem_limit_bytes=VMEM_LIMIT)


def _rms(x, w):
    return x * lax.rsqrt(jnp.mean(x * x, axis=-1, keepdims=True) + RMS_EPS) * w


def _sigmoid(x):
    return 1.0 / (1.0 + jnp.exp2(x * NEG_LOG2E))


def _silu(x):
    return x * _sigmoid(x)


def _dot(a, b):
    return jnp.dot(a, b, preferred_element_type=F32)


def _dot_nt(a, b):
    return lax.dot_general(a, b, (((1,), (1,)), ((), ())), preferred_element_type=F32)


def _split3(x):
    hi = x.astype(BF16)
    r1 = x - hi.astype(F32)
    mid = r1.astype(BF16)
    lo = (r1 - mid.astype(F32)).astype(BF16)
    return hi, mid, lo


def _rms_matmul_kernel(x_ref, nw_ref, w_ref, wdt_ref, o_ref, dtr_ref):
    xn = _rms(x_ref[...], nw_ref[...]).astype(BF16)
    o_ref[...] = _dot(xn, w_ref[...])
    dtr_ref[...] = _dot_nt(wdt_ref[...], xn)


def rms_matmul(x, nw, w, w_dt_t, layer, tm=512):
    m, k = x.shape
    n = w.shape[2]
    nr = w_dt_t.shape[1]
    return pl.pallas_call(
        _rms_matmul_kernel,
        grid=(m // tm,),
        in_specs=[pl.BlockSpec((tm, k), lambda i: (i, 0)),
                  pl.BlockSpec((1, k), lambda i: (0, 0)),
                  pl.BlockSpec((None, k, n), lambda i: (layer, 0, 0)),
                  pl.BlockSpec((None, nr, k), lambda i: (layer, 0, 0))],
        out_specs=[pl.BlockSpec((tm, n), lambda i: (i, 0)),
                   pl.BlockSpec((nr, tm), lambda i: (0, i))],
        out_shape=[jax.ShapeDtypeStruct((m, n), F32), jax.ShapeDtypeStruct((nr, m), F32)],
        compiler_params=_cparams("parallel"),
        name="rms_matmul",
    )(x, nw.reshape(1, k), w, w_dt_t)


def _hgrn2_masks(L):
    t = np.arange(L)
    masks = []
    w = L // 2
    while w >= 1:
        blk = t // w
        masks.append((blk % 2 == 1)[:, None] & (blk[None, :] == blk[:, None] - 1))
        w //= 2
    masks.append(np.eye(L, dtype=bool))
    m_f = np.stack(masks).astype(np.float32)
    return m_f, np.ascontiguousarray(m_f[:, ::-1, ::-1]), len(masks) - 1


def _pair_ref(b, w, reverse):
    L, d = b.shape
    off = w if reverse else w - 1
    if 2 * w > 8:
        pieces = [jnp.broadcast_to(b[p * 2 * w + off:p * 2 * w + off + 1, :], (2 * w, d))
                  for p in range(L // (2 * w))]
        return pieces[0] if len(pieces) == 1 else jnp.concatenate(pieces, axis=0)
    b3 = b.reshape(L // 8, 8, d)
    sub = lax.broadcasted_iota(jnp.int32, b3.shape, 1)
    ref = None
    for p in range(8 // (2 * w)):
        row = jnp.broadcast_to(b3[:, p * 2 * w + off:p * 2 * w + off + 1, :], b3.shape)
        ref = row if ref is None else jnp.where(sub >= p * 2 * w, row, ref)
    return ref.reshape(L, d)


def _hgrn2_kernel(qp_ref, ffp_ref, fbp_ref, ip_ref, gp_ref, lb_ref, nw_ref,
                  trif_ref, trib_ref, mf_ref, mb_ref, o_ref,
                  of_scr, ob_scr, stf_scr, stb_scr, *, L, nl):
    S = qp_ref.shape[0]
    nc = S // L
    lb = lb_ref[...]
    lb_floor = jnp.maximum(lb, LB_FLOOR)
    one_m_lb = 1.0 - lb

    def chunk(c, fpre_ref, tri_ref, m_ref, st_scr, reverse):
        r0 = pl.multiple_of(c * L, L)
        q = _silu(qp_ref[pl.ds(r0, L), :])
        v = ip_ref[pl.ds(r0, L), :]
        f = lb_floor + one_m_lb * _sigmoid(fpre_ref[pl.ds(r0, L), :])
        k = 1.0 - f
        hi, mid, lo = _split3(jnp.log2(f))
        tri = tri_ref[...]
        b = _dot(tri, hi) + _dot(tri, mid) + _dot(tri, lo)
        qb = q.astype(BF16)
        kb = k.astype(BF16)
        a = m_ref[nl] * _dot_nt(qb, kb)
        w = L // 2
        for lvl in range(nl):
            e = jnp.exp2(-jnp.abs(b - _pair_ref(b, w, reverse))).astype(BF16)
            a = a + m_ref[lvl] * _dot_nt(qb * e, kb * e)
            w //= 2
        tot_row = b[0:1] if reverse else b[L - 1:L]
        q_in = (q * jnp.exp2(b)).astype(BF16)
        k_out = (k * jnp.exp2(tot_row - b)).astype(BF16)
        st = st_scr[...]
        o = _dot(a.astype(BF16), v.astype(BF16)) + _dot_nt(q_in, st.astype(BF16))
        st_scr[...] = st * jnp.exp2(tot_row) + _dot(v.T.astype(BF16), k_out)
        return r0, o

    stf_scr[...] = jnp.zeros_like(stf_scr)
    stb_scr[...] = jnp.zeros_like(stb_scr)

    def body(ci, carry):
        r0, o = chunk(ci, ffp_ref, trif_ref, mf_ref, stf_scr, False)
        of_scr[pl.ds(r0, L), :] = o
        r0, o = chunk(nc - 1 - ci, fbp_ref, trib_ref, mb_ref, stb_scr, True)
        ob_scr[pl.ds(r0, L), :] = o
        return carry

    lax.fori_loop(0, nc, body, 0, unroll=4)
    nw = nw_ref[...]
    R = 4 * L if S % (4 * L) == 0 else L

    def finish(i, carry):
        r0 = pl.multiple_of(i * R, R)
        o = of_scr[pl.ds(r0, R), :] + ob_scr[pl.ds(r0, R), :]
        o = o * lax.rsqrt(jnp.mean(o * o, axis=-1, keepdims=True) + RMS_EPS)
        o = o * nw * _sigmoid(gp_ref[pl.ds(r0, R), :])
        o_ref[pl.ds(r0, R), :] = o.astype(o_ref.dtype)
        return carry

    lax.fori_loop(0, S // R, finish, 0)


def hgrn2(proj, lb, norm_w, L=HG_CHUNK):
    B, S, _ = proj.shape
    m_f, m_b, nl = _hgrn2_masks(L)
    t = np.arange(L)
    tri_f = jnp.asarray((t[None, :] <= t[:, None]).astype(np.float32), BF16)
    tri_b = jnp.asarray((t[None, :] >= t[:, None]).astype(np.float32), BF16)
    d = HG_DIM
    nb = HG_WIDTH // d

    def col(off):
        return pl.BlockSpec((None, S, d), lambda b, h, off=off: (b, 0, off // d + h))

    vec = pl.BlockSpec((1, d), lambda b, h: (0, h))
    const2 = lambda a: pl.BlockSpec(a.shape, lambda b, h: (0, 0))
    const3 = lambda a: pl.BlockSpec(a.shape, lambda b, h: (0, 0, 0))
    return pl.pallas_call(
        functools.partial(_hgrn2_kernel, L=L, nl=nl),
        grid=(B, nb),
        in_specs=[col(OFF_Q), col(OFF_FF), col(OFF_FB), col(OFF_I), col(OFF_G), vec, vec,
                  const2(tri_f), const2(tri_b), const3(m_f), const3(m_b)],
        out_specs=pl.BlockSpec((None, S, d), lambda b, h: (b, 0, h)),
        out_shape=jax.ShapeDtypeStruct((B, S, HG_WIDTH), BF16),
        scratch_shapes=[pltpu.VMEM((S, d), F32), pltpu.VMEM((S, d), F32),
                        pltpu.VMEM((d, d), F32), pltpu.VMEM((d, d), F32)],
        compiler_params=_cparams("parallel", "parallel"),
        name="hgrn2",
    )(proj, proj, proj, proj, proj, lb.reshape(1, HG_WIDTH), norm_w.reshape(1, HG_WIDTH),
      tri_f, tri_b, jnp.asarray(m_f), jnp.asarray(m_b))


def _shift_rows(x, k):
    if k == 0:
        return x
    n = x.shape[0]
    rolled = pltpu.roll(x, (-k) % n, 0)
    t = lax.broadcasted_iota(jnp.int32, x.shape, 0)
    ok = (t + k >= 0) & (t + k < n)
    return jnp.where(ok, rolled, 0.0)


def _conv_silu(x, w, b):
    half = w.shape[0] // 2
    acc = b
    for j in range(w.shape[0]):
        acc = acc + w[j:j + 1, :] * _shift_rows(x, j - half)
    return _silu(acc)


def _ssd_kernel(x_ref, b_ref, c_ref, z_ref, dtc_ref, dtr_ref,
                cwx_ref, cwb_ref, cwc_ref, cbx_ref, cbb_ref, cbc_ref,
                dtbc_ref, dtbr_ref, alr_ref, alc_ref, dsk_ref, nw_ref,
                tril_ref, triu_ref,
                o_ref, xs_scr, bs_scr, cs_scr, yf_scr, yb_scr, stf_scr, stb_scr, *, L, hg, P):
    S = x_ref.shape[0]
    nc = S // L
    nh = 2 * hg
    gw = hg * P
    N = bs_scr.shape[1]

    a_row = jnp.exp(alr_ref[...]) * NEG_LOG2E
    a_col = jnp.exp(alc_ref[...]) * NEG_LOG2E
    tril = tril_ref[...]
    triu = triu_ref[...]
    ti = lax.broadcasted_iota(jnp.int32, (L, L), 0)
    si = lax.broadcasted_iota(jnp.int32, (L, L), 1)
    lane2 = lax.broadcasted_iota(jnp.int32, (L, 2 * P), 1)

    def softplus(v):
        return jnp.maximum(v, 0.0) + jnp.log(1.0 + jnp.exp(-jnp.abs(v)))

    def per_head(cols):
        tiles = []
        for j in range(0, hg, 2):
            lo = jnp.broadcast_to(cols[j], (L, 2 * P))
            hi = jnp.broadcast_to(cols[j + 1], (L, 2 * P))
            tiles.append(jnp.where(lane2 < P, lo, hi))
        return jnp.concatenate(tiles, axis=1)

    def chunk(c, d, g, st_scr):
        cum_c = tril if d == 0 else triu
        cum_r = triu if d == 0 else tril
        r0 = pl.multiple_of(c * L, L)
        x = xs_scr[pl.ds(r0, L), :]
        bm = bs_scr[pl.ds(r0, L), :]
        cm = cs_scr[pl.ds(r0, L), :]
        dt_c = softplus(dtc_ref[pl.ds(r0, L), :] + dtbc_ref[...])
        dt_r = softplus(dtr_ref[:, pl.ds(r0, L)] + dtbr_ref[...])
        h1, h2, h3 = _split3(dt_c * a_row)
        acum_c = _dot(cum_c, h1) + _dot(cum_c, h2) + _dot(cum_c, h3)
        g1, g2, g3 = _split3(dt_r * a_col)
        acum_r = _dot(g1, cum_r) + _dot(g2, cum_r) + _dot(g3, cum_r)
        ones_rows = jnp.ones((8, L), BF16)
        tot = (_dot(ones_rows, h1) + _dot(ones_rows, h2) + _dot(ones_rows, h3))[0:1]
        cb = _dot_nt(cm.astype(BF16), bm.astype(BF16))
        keep = (si <= ti) if d == 0 else (si >= ti)
        ys, in_cols, out_cols, dec_cols = [], [], [], []
        for hh in range(hg):
            j = d * nh + g * hg + hh
            ac = acum_c[:, j:j + 1]
            dc = dt_c[:, j:j + 1]
            te = tot[:, j:j + 1]
            ar = acum_r[j:j + 1, :]
            dr = dt_r[j:j + 1, :]
            seg = ac - ar
            decay = jnp.where(keep, jnp.exp2(jnp.where(keep, seg, 0.0)), 0.0)
            w = (cb * decay * dr).astype(BF16)
            ys.append(_dot(w, x[:, hh * P:(hh + 1) * P].astype(BF16)))
            in_cols.append(jnp.exp2(te - ac) * dc)
            out_cols.append(jnp.exp2(ac))
            dec_cols.append(jnp.broadcast_to(jnp.exp2(te), (1, P)))
        y_diag = jnp.concatenate(ys, axis=1)
        st = st_scr[...]
        y_off = _dot(cm.astype(BF16), st.astype(BF16)) * per_head(out_cols)
        x_in = (x * per_head(in_cols)).astype(BF16)
        st_scr[...] = st * jnp.concatenate(dec_cols, axis=1) + _dot(bm.T.astype(BF16), x_in)
        return r0, y_diag + y_off

    R = 2 * L
    for g in range(x_ref.shape[1] // gw):
        cols = slice(g * gw, (g + 1) * gw)
        ncols = slice(g * N, (g + 1) * N)
        xs_scr[...] = _conv_silu(x_ref[:, cols], cwx_ref[:, cols], cbx_ref[:, cols])
        bs_scr[...] = _conv_silu(b_ref[:, ncols], cwb_ref[:, ncols], cbb_ref[:, ncols])
        cs_scr[...] = _conv_silu(c_ref[:, ncols], cwc_ref[:, ncols], cbc_ref[:, ncols])
        stf_scr[...] = jnp.zeros_like(stf_scr)
        stb_scr[...] = jnp.zeros_like(stb_scr)

        def body(ci, carry, g=g):
            r0, y = chunk(ci, 0, g, stf_scr)
            yf_scr[pl.ds(r0, L), :] = y
            r0, y = chunk(nc - 1 - ci, 1, g, stb_scr)
            yb_scr[pl.ds(r0, L), :] = y
            return carry

        lax.fori_loop(0, nc, body, 0, unroll=2)

        def finish(i, carry, cols=cols):
            r0 = pl.multiple_of(i * R, R)
            y = (yf_scr[pl.ds(r0, R), :] + yb_scr[pl.ds(r0, R), :]
                 + dsk_ref[:, cols] * xs_scr[pl.ds(r0, R), :])
            y = y * _silu(z_ref[pl.ds(r0, R), cols])
            y = y * lax.rsqrt(jnp.mean(y * y, axis=-1, keepdims=True) + RMS_EPS)
            o_ref[pl.ds(r0, R), cols] = (y * nw_ref[:, cols]).astype(o_ref.dtype)
            return carry

        lax.fori_loop(0, S // R, finish, 0)


def ssd(proj, dt_rows, conv_w, conv_b, dt_bias, a_log, d_skip, norm_w, L=M2_CHUNK):
    B, S, _ = proj.shape
    hg = M2_HEADS // M2_GROUPS
    gw = hg * M2_P
    t = np.arange(L)
    tril = jnp.asarray((t[None, :] <= t[:, None]).astype(np.float32), BF16)
    triu = jnp.asarray((t[None, :] >= t[:, None]).astype(np.float32), BF16)
    nb_c = OFF_XBC + M2_INNER
    nc_c = nb_c + M2_GROUPS * M2_N
    xw = conv_w[:, :M2_INNER]
    bw = conv_w[:, M2_INNER:M2_INNER + M2_GROUPS * M2_N]
    cw = conv_w[:, M2_INNER + M2_GROUPS * M2_N:]
    cb2 = conv_b.reshape(1, -1)
    xb = cb2[:, :M2_INNER]
    bb = cb2[:, M2_INNER:M2_INNER + M2_GROUPS * M2_N]
    cbb = cb2[:, M2_INNER + M2_GROUPS * M2_N:]
    nh2 = 2 * M2_HEADS
    dtb_row = jnp.zeros((1, LANE), F32).at[0, :nh2].set(dt_bias.reshape(-1))
    al_row = jnp.zeros((1, LANE), F32).at[0, :nh2].set(a_log.reshape(-1))
    dtb_col = dt_bias.reshape(nh2, 1)
    al_col = a_log.reshape(nh2, 1)
    dsk = jnp.repeat(d_skip, M2_P).reshape(1, M2_INNER)
    nw = norm_w.reshape(1, M2_INNER)

    gn = M2_GROUPS * M2_N
    full2 = lambda a: pl.BlockSpec(a.shape, lambda b: (0, 0))
    return pl.pallas_call(
        functools.partial(_ssd_kernel, L=L, hg=hg, P=M2_P),
        grid=(B,),
        in_specs=[
            pl.BlockSpec((None, S, M2_INNER), lambda b: (b, 0, OFF_XBC // M2_INNER)),
            pl.BlockSpec((None, S, gn), lambda b: (b, 0, nb_c // gn)),
            pl.BlockSpec((None, S, gn), lambda b: (b, 0, nc_c // gn)),
            pl.BlockSpec((None, S, M2_INNER), lambda b: (b, 0, OFF_Z // M2_INNER)),
            pl.BlockSpec((None, S, LANE), lambda b: (b, 0, OFF_DT // LANE)),
            pl.BlockSpec((nh2, S), lambda b: (0, b)),
            full2(xw), full2(bw), full2(cw), full2(xb), full2(bb), full2(cbb),
            full2(dtb_row), full2(dtb_col), full2(al_row), full2(al_col),
            full2(dsk), full2(nw), full2(tril), full2(triu),
        ],
        out_specs=pl.BlockSpec((None, S, M2_INNER), lambda b: (b, 0, 0)),
        out_shape=jax.ShapeDtypeStruct((B, S, M2_INNER), BF16),
        scratch_shapes=[pltpu.VMEM((S, gw), F32), pltpu.VMEM((S, M2_N), F32),
                        pltpu.VMEM((S, M2_N), F32), pltpu.VMEM((S, gw), F32),
                        pltpu.VMEM((S, gw), F32), pltpu.VMEM((M2_N, gw), F32),
                        pltpu.VMEM((M2_N, gw), F32)],
        compiler_params=_cparams("parallel"),
        name="ssd",
    )(proj, proj, proj, proj, proj, dt_rows, xw, bw, cw, xb, bb, cbb,
      dtb_row, dtb_col, al_row, al_col, dsk, nw, tril, triu)


def _ffn_kernel(h_ref, oa_ref, ob_ref, wa_ref, wb_ref, nw_ref, wg_ref, wu_ref, wd_ref, o_ref,
                xn_scr, acc_scr):
    f = pl.program_id(1)

    @pl.when(f == 0)
    def _():
        x = h_ref[...] + _dot(oa_ref[...], wa_ref[...]) + _dot(ob_ref[...], wb_ref[...])
        xn_scr[...] = _rms(x, nw_ref[...]).astype(BF16)
        acc_scr[...] = x

    xn = xn_scr[...]
    act = _silu(_dot(xn, wg_ref[...].astype(BF16))) * _dot(xn, wu_ref[...].astype(BF16))
    acc_scr[...] += _dot(act.astype(BF16), wd_ref[...].astype(BF16))

    @pl.when(f == pl.num_programs(1) - 1)
    def _():
        o_ref[...] = acc_scr[...]


def ffn(h, oa, ob, w_out, nw, w_gu, w_down, layer, tm=1024, tf=512):
    m, d = h.shape
    dff = w_down.shape[1]
    nf = dff // tf
    ka, kb = oa.shape[1], ob.shape[1]
    assert ka == kb
    return pl.pallas_call(
        _ffn_kernel,
        grid=(m // tm, nf),
        in_specs=[pl.BlockSpec((tm, d), lambda i, f: (i, 0)),
                  pl.BlockSpec((tm, ka), lambda i, f: (i, 0)),
                  pl.BlockSpec((tm, kb), lambda i, f: (i, 0)),
                  pl.BlockSpec((None, ka, d), lambda i, f: (layer, 0, 0)),
                  pl.BlockSpec((None, kb, d), lambda i, f: (layer, 1, 0)),
                  pl.BlockSpec((1, d), lambda i, f: (0, 0)),
                  pl.BlockSpec((None, d, tf), lambda i, f: (layer, 0, f)),
                  pl.BlockSpec((None, d, tf), lambda i, f: (layer, 0, nf + f)),
                  pl.BlockSpec((None, tf, d), lambda i, f: (layer, f, 0))],
        out_specs=pl.BlockSpec((tm, d), lambda i, f: (i, 0)),
        out_shape=jax.ShapeDtypeStruct((m, d), F32),
        scratch_shapes=[pltpu.VMEM((tm, d), BF16), pltpu.VMEM((tm, d), F32)],
        compiler_params=_cparams("parallel", "arbitrary"),
        name="ffn",
    )(h, oa, ob, w_out, w_out, nw.reshape(1, d), w_gu, w_gu, w_down)


def _ple_kernel(*refs, moe, final):
    if moe:
        h_ref, y2_ref, meta_ref, p_ref, nw_ref, wg_ref, wp_ref, fw_ref, o_ref = refs
        meta = meta_ref[...]
        d = h_ref.shape[1]
        h = h_ref[...] + (meta[:, 2:3] * y2_ref[:, 0:d] + meta[:, 3:4] * y2_ref[:, d:2 * d])
    else:
        h_ref, p_ref, nw_ref, wg_ref, wp_ref, fw_ref, o_ref = refs
        h = h_ref[...]
    gate = _sigmoid(_dot(_rms(h, nw_ref[...]).astype(BF16), wg_ref[...]))
    h = h + gate * _dot(p_ref[...].astype(BF16), wp_ref[...])
    if final:
        h = _rms(h, fw_ref[...])
    o_ref[...] = h


def ple(h, p, nw, wg, wp, fw, layer, y2=None, meta=None, final=False, tm=512):
    m, d = h.shape
    moe = y2 is not None
    row = lambda w: pl.BlockSpec((tm, w), lambda i: (i, 0))
    slab = lambda a: pl.BlockSpec((None,) + a.shape[1:], lambda i: (layer, 0, 0))
    vec = pl.BlockSpec((1, d), lambda i: (0, 0))
    in_specs = [row(d)]
    args = [h]
    if moe:
        in_specs += [row(2 * d), row(meta.shape[1])]
        args += [y2, meta]
    in_specs += [pl.BlockSpec((None, tm, p.shape[2]), lambda i: (layer, i, 0)), vec, slab(wg), slab(wp), vec]
    args += [p, nw.reshape(1, d), wg, wp, fw.reshape(1, d)]
    return pl.pallas_call(
        functools.partial(_ple_kernel, moe=moe, final=final),
        grid=(m // tm,),
        in_specs=in_specs,
        out_specs=row(d),
        out_shape=jax.ShapeDtypeStruct((m, d), F32),
        compiler_params=_cparams("parallel"),
        name="ple",
    )(*args)


def _s5_strips(a_re, a_im, log_step, b_re, b_im, c_re, c_im):
    G, N = a_re.shape[1:]
    C = S5_GROUP_SIZE
    T = S5_STEPS
    gt = LANE // C
    Z = G // gt
    tau = jnp.arange(T + 1, dtype=F32)
    steps = jnp.arange(T)

    def cmul(xr, xi, yr, yi):
        return xr * yr - xi * yi, xr * yi + xi * yr

    st_parts, rd_parts, k_parts, lam_rows = [], [], [], []
    for d in range(2):
        delta = jnp.exp(log_step[d])[:, None]
        ar, ai = a_re[d], a_im[d]
        mag = jnp.exp(ar * delta)
        lam_re, lam_im = mag * jnp.cos(ai * delta), mag * jnp.sin(ai * delta)
        den = ar * ar + ai * ai
        num_re = lam_re - 1.0
        coef_re = (num_re * ar + lam_im * ai) / den
        coef_im = (lam_im * ar - num_re * ai) / den
        br = coef_re[..., None] * b_re - coef_im[..., None] * b_im
        bi = coef_re[..., None] * b_im + coef_im[..., None] * b_re
        cr, ci = c_re[d], c_im[d]
        pm = jnp.exp((ar * delta)[None] * tau[:, None, None])
        ang = (ai * delta)[None] * tau[:, None, None]
        pr, pi = pm * jnp.cos(ang), pm * jnp.sin(ang)
        e_in = (T - 1 - steps) if d == 0 else steps
        sr, si = cmul(pr[e_in][..., None], pi[e_in][..., None], br[None], bi[None])
        e_out = (steps + 1) if d == 0 else (T - steps)
        cpr, cpi = cmul(cr[None], ci[None], pr[e_out][:, :, None, :], pi[e_out][:, :, None, :])
        lbr, lbi = cmul(pr[:T][..., None], pi[:T][..., None], br[None], bi[None])
        ktau = jnp.einsum('gon,tgni->tgoi', cr, lbr) - jnp.einsum('gon,tgni->tgoi', ci, lbi)
        st_parts += [sr, si]
        rd_parts += [cpr, -cpi]
        k_parts.append(ktau)
        lam_rows += [pr[T], pi[T]]

    NS = gt * N
    st = jnp.stack(st_parts).reshape(4, T, Z, gt, N, C).transpose(2, 1, 5, 0, 3, 4)
    rd = jnp.stack(rd_parts).reshape(4, T, Z, gt, C, N).transpose(2, 1, 4, 0, 3, 5)
    lag = steps[None, :] - steps[:, None]
    sel_f = (lag[:, :, None] == steps[None, None, :]).astype(F32)
    sel_b = (-lag[:, :, None] == steps[None, None, :]).astype(F32)
    toe = (jnp.einsum('stk,kgoi->stgoi', sel_f, k_parts[0])
           + jnp.einsum('stk,kgoi->stgoi', sel_b, k_parts[1]))
    toe = toe.reshape(T, T, Z, gt, C, C).transpose(2, 0, 5, 1, 3, 4)
    lam = jnp.stack([r.reshape(Z, NS) for r in lam_rows], axis=1)
    lam = jnp.concatenate([lam, jnp.zeros_like(lam)], axis=1)
    return (st.reshape(Z, T * C, 4 * NS), toe.reshape(Z, T * C, T * LANE),
            rd.reshape(Z, T * C, 4 * NS), lam)


def s5_tables(a_re, a_im, log_step, b_re, b_im, c_re, c_im):
    st, toe, rd, lam = jax.vmap(_s5_strips)(a_re, a_im, log_step, b_re, b_im, c_re, c_im)
    flat = lambda a: a.reshape((-1,) + a.shape[2:])
    wst = block_diag_rows(flat(st), S5_GROUP_SIZE, S5_STATE)
    kin = block_diag_rows(flat(toe), S5_GROUP_SIZE, S5_GROUP_SIZE)
    wc = block_diag_rows(flat(rd), S5_GROUP_SIZE, S5_STATE)
    return wst, kin, wc, flat(lam)


def _block_diag_kernel(s_ref, o_ref, *, C, gcol):
    n = o_ref.shape[1]
    gt = LANE // C
    row_g = lax.broadcasted_iota(jnp.int32, (LANE, n), 0) // C
    col_g = (lax.broadcasted_iota(jnp.int32, (LANE, n), 1) // gcol) % gt
    keep = row_g == col_g
    for t in range(o_ref.shape[0] // LANE):
        strip = s_ref[t * C:(t + 1) * C, :]
        tiled = jnp.broadcast_to(strip[None], (gt, C, n)).reshape(LANE, n)
        o_ref[t * LANE:(t + 1) * LANE, :] = jnp.where(keep, tiled, 0.0).astype(o_ref.dtype)


def block_diag_rows(strips, C, gcol):
    Z, rows, n = strips.shape
    gt = LANE // C
    return pl.pallas_call(
        functools.partial(_block_diag_kernel, C=C, gcol=gcol),
        grid=(Z,),
        in_specs=[pl.BlockSpec((None, rows, n), lambda z: (z, 0, 0))],
        out_specs=pl.BlockSpec((None, rows * gt, n), lambda z: (z, 0, 0)),
        out_shape=jax.ShapeDtypeStruct((Z, rows * gt, n), BF16),
        compiler_params=_cparams("parallel"),
        name="block_diag_rows",
    )(strips)


def _s5_pre_kernel(x_ref, nw_ref, u_ref, xn_scr, tmp_scr):
    nb, R, D = x_ref.shape
    T = S5_STEPS
    rc = R // T
    xn = _rms(x_ref[...], nw_ref[...])
    for z in range(D // LANE):
        xn_scr[z] = xn[:, :, z * LANE:(z + 1) * LANE]
    for z in range(D // LANE):
        for t in range(T):
            for b in range(nb):
                tmp_scr[z, t, pl.ds(b, rc, stride=nb), :] = xn_scr[z, b, pl.ds(t, rc, stride=T), :]
            u_ref[z, :, t * LANE:(t + 1) * LANE] = tmp_scr[z, t].astype(u_ref.dtype)


def s5_pre(h3, nw, R=64):
    B, S, D = h3.shape
    T = S5_STEPS
    Z = D // LANE
    rc = R // T
    return pl.pallas_call(
        _s5_pre_kernel,
        grid=(S // R,),
        in_specs=[pl.BlockSpec((B, R, D), lambda i: (0, i, 0)),
                  pl.BlockSpec((1, D), lambda i: (0, 0))],
        out_specs=pl.BlockSpec((Z, rc * B, T * LANE), lambda i: (0, i, 0)),
        out_shape=jax.ShapeDtypeStruct((Z, S // T * B, T * LANE), BF16),
        scratch_shapes=[pltpu.VMEM((Z, B, R, LANE), F32), pltpu.VMEM((Z, T, rc * B, LANE), F32)],
        compiler_params=_cparams("parallel"),
        name="s5_pre",
    )(h3, nw.reshape(1, D))


def _s5_kernel(u_ref, wst_ref, kin_ref, wc_ref, lam_ref, y_ref, s_scr, *, nb, rblk):
    rows = u_ref.shape[0]
    nc = rows // nb
    ns = lam_ref.shape[1]
    for r0 in range(0, rows, rblk):
        s_scr[r0:r0 + rblk, :] = _dot(u_ref[r0:r0 + rblk, :], wst_ref[...])
    lam = lam_ref[...]
    lfr, lfi, lbr, lbi = lam[0:1], lam[1:2], lam[2:3], lam[3:4]

    def body(ci, carry):
        hfr, hfi, hbr, hbi = carry
        rf = pl.multiple_of(ci * nb, nb)
        rb = pl.multiple_of((nc - 1 - ci) * nb, nb)
        sfr = s_scr[pl.ds(rf, nb), 0:ns]
        sfi = s_scr[pl.ds(rf, nb), ns:2 * ns]
        sbr = s_scr[pl.ds(rb, nb), 2 * ns:3 * ns]
        sbi = s_scr[pl.ds(rb, nb), 3 * ns:4 * ns]
        s_scr[pl.ds(rf, nb), 0:ns] = hfr
        s_scr[pl.ds(rf, nb), ns:2 * ns] = hfi
        s_scr[pl.ds(rb, nb), 2 * ns:3 * ns] = hbr
        s_scr[pl.ds(rb, nb), 3 * ns:4 * ns] = hbi
        return (lfr * hfr - lfi * hfi + sfr, lfr * hfi + lfi * hfr + sfi,
                lbr * hbr - lbi * hbi + sbr, lbr * hbi + lbi * hbr + sbi)

    z = jnp.zeros((nb, ns), F32)
    lax.fori_loop(0, nc, body, (z, z, z, z))
    for r0 in range(0, rows, rblk):
        y = (_dot(u_ref[r0:r0 + rblk, :], kin_ref[...])
             + _dot_nt(s_scr[r0:r0 + rblk, :].astype(BF16), wc_ref[...]))
        y_ref[r0:r0 + rblk, :] = y.astype(y_ref.dtype)


def s5_scan(u, wst, kin, wc, lam, nb, layer, rblk=512):
    Z, rows, K = u.shape
    z0 = layer * Z
    per = lambda a: pl.BlockSpec((None,) + a.shape[1:], lambda s: (s, 0, 0))
    tab = lambda a: pl.BlockSpec((None,) + a.shape[1:], lambda s: (z0 + s, 0, 0))
    once = lambda a: pl.BlockSpec((None,) + a.shape[1:], lambda s: (z0 + s, 0, 0),
                                  pipeline_mode=pl.Buffered(1))
    return pl.pallas_call(
        functools.partial(_s5_kernel, nb=nb, rblk=rblk),
        grid=(Z,),
        in_specs=[per(u), once(wst), once(kin), once(wc), tab(lam)],
        out_specs=pl.BlockSpec((None, rows, K), lambda s: (s, 0, 0)),
        out_shape=jax.ShapeDtypeStruct((Z, rows, K), BF16),
        scratch_shapes=[pltpu.VMEM((rows, wst.shape[2]), F32)],
        compiler_params=_cparams("parallel"),
        name="s5_scan",
    )(u, wst, kin, wc, lam)


def _s5_post_kernel(h_ref, y_ref, nw_ref, d_ref, wo_ref, wg_ref, o_ref, yf_scr, yt_scr):
    nb, R, D = h_ref.shape
    T = S5_STEPS
    rc = R // T
    for z in range(D // LANE):
        for t in range(T):
            yf_scr[z, t] = y_ref[z, :, t * LANE:(t + 1) * LANE].astype(F32)
            for b in range(nb):
                yt_scr[z, b, pl.ds(t, rc, stride=T), :] = yf_scr[z, t, pl.ds(b, rc, stride=nb), :]
    yt = jnp.concatenate([yt_scr[z] for z in range(D // LANE)], axis=-1)
    h = h_ref[...].reshape(nb * R, D)
    y = yt.reshape(nb * R, D) + d_ref[...] * _rms(h, nw_ref[...])
    act = jax.nn.gelu(y).astype(BF16)
    out = h + _dot(act, wo_ref[...]) * _sigmoid(_dot(act, wg_ref[...]))
    o_ref[...] = out.reshape(nb, R, D)


def s5_post(h3, y, nw, d_skip, glu_w, layer, R=64):
    B, S, D = h3.shape
    T = S5_STEPS
    Z = D // LANE
    rc = R // T
    row = pl.BlockSpec((B, R, D), lambda i: (0, i, 0))
    vec = pl.BlockSpec((1, D), lambda i: (0, 0))
    return pl.pallas_call(
        _s5_post_kernel,
        grid=(S // R,),
        in_specs=[row, pl.BlockSpec((Z, rc * B, T * LANE), lambda i: (0, i, 0)), vec, vec,
                  pl.BlockSpec((None, D, D), lambda i: (layer, 0, 0)),
                  pl.BlockSpec((None, D, D), lambda i: (layer, 0, 1))],
        out_specs=row,
        out_shape=jax.ShapeDtypeStruct((B, S, D), F32),
        scratch_shapes=[pltpu.VMEM((Z, T, rc * B, LANE), F32), pltpu.VMEM((Z, B, R, LANE), F32)],
        compiler_params=_cparams("parallel"),
        name="s5_post",
    )(h3, y, nw.reshape(1, D), d_skip.reshape(1, D), glu_w, glu_w)


def _router_kernel(h_ref, nw_ref, whi_ref, wlo_ref, tri_ref, meta_ref, cnt_ref, carry_scr):
    @pl.when(pl.program_id(0) == 0)
    def _():
        carry_scr[...] = jnp.zeros_like(carry_scr)

    xn = _rms(h_ref[...], nw_ref[...])
    x_hi, x_lo, _ = _split3(xn)
    w_hi, w_lo = whi_ref[...], wlo_ref[...]
    logits = _dot(x_hi, w_hi) + (_dot(x_hi, w_lo) + _dot(x_lo, w_hi))
    lane = lax.broadcasted_iota(jnp.int32, logits.shape, 1)
    neg = jnp.float32(-jnp.inf)
    lg = jnp.where(lane < N_EXPERTS, logits, neg)
    t1 = jnp.max(lg, axis=1, keepdims=True)
    i1 = jnp.min(jnp.where(lg == t1, lane, LANE), axis=1, keepdims=True)
    lg2 = jnp.where(lane == i1, neg, lg)
    t2 = jnp.max(lg2, axis=1, keepdims=True)
    i2 = jnp.min(jnp.where(lg2 == t2, lane, LANE), axis=1, keepdims=True)
    ex = jnp.exp(t2 - t1)
    g1 = 1.0 / (1.0 + ex)
    g2 = ex / (1.0 + ex)
    oh1 = jnp.where(lane == i1, 1.0, 0.0)
    oh2 = jnp.where(lane == i2, 1.0, 0.0)
    tri = tri_ref[...]
    before1 = _dot(tri, oh1.astype(BF16))
    before2 = _dot(tri, oh2.astype(BF16))
    tot1 = jnp.sum(oh1, axis=0, keepdims=True)
    tot2 = jnp.sum(oh2, axis=0, keepdims=True)
    carry = carry_scr[0:1]
    rank1 = jnp.sum(oh1 * (carry + before1), axis=1, keepdims=True)
    rank2 = jnp.sum(oh2 * (carry + tot1 + before2), axis=1, keepdims=True)
    counts = jnp.broadcast_to(carry + tot1 + tot2, carry_scr.shape)
    carry_scr[...] = counts
    cnt_ref[...] = counts
    meta = jnp.where(lane == 0, i1.astype(F32), 0.0)
    meta = jnp.where(lane == 1, i2.astype(F32), meta)
    meta = jnp.where(lane == 2, g1, meta)
    meta = jnp.where(lane == 3, g2, meta)
    meta = jnp.where(lane == 4, rank1, meta)
    meta = jnp.where(lane == 5, rank2, meta)
    meta_ref[...] = meta


def router(h, nw, w_router, tm=512):
    m, d = h.shape
    wr = jnp.zeros((d, LANE), F32).at[:, :N_EXPERTS].set(w_router)
    w_hi, w_lo, _ = _split3(wr)
    t = np.arange(tm)
    tri = jnp.asarray((t[None, :] < t[:, None]).astype(np.float32), BF16)
    return pl.pallas_call(
        _router_kernel,
        grid=(m // tm,),
        in_specs=[pl.BlockSpec((tm, d), lambda i: (i, 0)),
                  pl.BlockSpec((1, d), lambda i: (0, 0)),
                  pl.BlockSpec((d, LANE), lambda i: (0, 0)),
                  pl.BlockSpec((d, LANE), lambda i: (0, 0)),
                  pl.BlockSpec((tm, tm), lambda i: (0, 0))],
        out_specs=[pl.BlockSpec((tm, LANE), lambda i: (i, 0)),
                   pl.BlockSpec((8, LANE), lambda i: (0, 0))],
        out_shape=[jax.ShapeDtypeStruct((m, LANE), F32), jax.ShapeDtypeStruct((8, LANE), F32)],
        scratch_shapes=[pltpu.VMEM((8, LANE), F32)],
        compiler_params=_cparams("arbitrary"),
        name="router",
    )(h, nw.reshape(1, d), w_hi, w_lo, tri)


def _moe_kernel(arow_ref, blk_e_ref, nvalid_ref, h_hbm, nw_ref, wg_ref, wu_ref, wd_ref,
                y2_hbm, xg_scr, xn_scr, acc_scr, sem_in, sem_out, *, bm, n_tok):
    g = pl.program_id(0)
    f = pl.program_id(1)
    nf = pl.num_programs(1)
    slot = g % 2
    other = 1 - slot
    d = acc_scr.shape[2]
    rows_per_step = bm // (nf - 1)
    prv, cur, nxt = g * bm, (g + 1) * bm, (g + 2) * bm
    nv = nvalid_ref[g]

    def in_copy(off, r, s):
        tok = jnp.minimum(arow_ref[off + r] >> 1, n_tok - 1)
        return pltpu.make_async_copy(h_hbm.at[pl.ds(tok, 1)], xg_scr.at[s, pl.ds(r, 1)], sem_in.at[s])

    def out_copy(off, r, s):
        a = arow_ref[off + r]
        col = pl.multiple_of((a & 1) * d, d)
        return pltpu.make_async_copy(acc_scr.at[s, pl.ds(r, 1)],
                                     y2_hbm.at[pl.ds(a >> 1, 1), pl.ds(col, d)], sem_out.at[s])

    def wait_in(s):
        pltpu.make_async_copy(h_hbm.at[pl.ds(0, bm)], xg_scr.at[s], sem_in.at[s]).wait()

    def wait_out(s):
        pltpu.make_async_copy(acc_scr.at[s], y2_hbm.at[pl.ds(0, bm), pl.ds(0, d)], sem_out.at[s]).wait()

    @pl.when((g == 0) & (f == 0))
    def _():
        acc_scr[...] = jnp.zeros_like(acc_scr)

        def start(r, c):
            in_copy(cur, r, slot).start()
            return c

        lax.fori_loop(0, bm, start, 0)

    @pl.when(f == 0)
    def _():
        wait_in(slot)
        xn_scr[...] = _rms(xg_scr[slot], nw_ref[...]).astype(BF16)
        acc_scr[slot] = jnp.zeros((bm, d), F32)

    def move_rows(part, parts):
        base = pl.multiple_of(f * rows_per_step, 8)
        per = rows_per_step // parts
        for j in range(part * per, (part + 1) * per):
            in_copy(nxt, base + j, other).start()
            out_copy(prv, base + j, other).start()

    def experts(with_moves):
        parts = 2
        tc = wg_ref.shape[2] // parts
        xn = xn_scr[...]
        upd = None
        for c in range(parts):
            cs = slice(c * tc, (c + 1) * tc)
            if with_moves:
                move_rows(c, parts)
            act = (_silu(_dot(xn, wg_ref[0, :, cs].astype(BF16)))
                   * _dot(xn, wu_ref[0, :, cs].astype(BF16)))
            part = _dot(act.astype(BF16), wd_ref[0, cs, :].astype(BF16))
            upd = part if upd is None else upd + part
        acc_scr[slot] += upd

    @pl.when((nv > 0) & (f < nf - 1))
    def _():
        experts(True)

    @pl.when((nv > 0) & (f == nf - 1))
    def _():
        experts(False)

    @pl.when((nv == 0) & (f < nf - 1))
    def _():
        move_rows(0, 1)

    @pl.when(f == nf - 1)
    def _():
        wait_out(other)

    @pl.when((f == nf - 1) & (g == pl.num_programs(0) - 1))
    def _():
        wait_in(other)


def moe_experts(h, nw, w_gu, w_down, arow, blk_e, nvalid, layer, bm=MOE_BM, tf=512):
    t, d = h.shape
    dff = w_down.shape[1]
    nf = dff // tf
    assert bm % (2 * (nf - 1)) == 0 and bm % 16 == 0
    n_steps = blk_e.shape[0]
    e0 = layer * N_EXPERTS

    def wmap(col0):
        def index(g, f, ar, be, nv):
            return (e0 + be[g], 0, col0 + jnp.where(nv[g] > 0, f, nf - 1))
        return index

    def dmap(g, f, ar, be, nv):
        return (e0 + be[g], jnp.where(nv[g] > 0, f, nf - 1), 0)

    grid_spec = pltpu.PrefetchScalarGridSpec(
        num_scalar_prefetch=3,
        grid=(n_steps, nf),
        in_specs=[pl.BlockSpec(memory_space=pl.ANY),
                  pl.BlockSpec((1, d), lambda g, f, ar, be, nv: (0, 0)),
                  pl.BlockSpec((1, d, tf), wmap(0)),
                  pl.BlockSpec((1, d, tf), wmap(nf)),
                  pl.BlockSpec((1, tf, d), dmap)],
        out_specs=pl.BlockSpec(memory_space=pl.ANY),
        scratch_shapes=[pltpu.VMEM((2, bm, d), F32), pltpu.VMEM((bm, d), BF16),
                        pltpu.VMEM((2, bm, d), F32),
                        pltpu.SemaphoreType.DMA((2,)), pltpu.SemaphoreType.DMA((2,))],
    )
    return pl.pallas_call(
        functools.partial(_moe_kernel, bm=bm, n_tok=t),
        grid_spec=grid_spec,
        out_shape=jax.ShapeDtypeStruct((t + bm // 2, 2 * d), F32),
        compiler_params=_cparams("arbitrary", "arbitrary"),
        name="moe_experts",
    )(arow, blk_e, nvalid, h, nw.reshape(1, d), w_gu, w_gu, w_down)


def _moe_plan(meta, counts, bm):
    t = meta.shape[0]
    n_assign = 2 * t
    experts = jnp.arange(N_EXPERTS, dtype=jnp.int32)
    cnt = counts[0, :N_EXPERTS].astype(jnp.int32)
    padded = (cnt + bm - 1) // bm * bm
    pend = jnp.cumsum(padded)
    pstart = pend - padded
    e = meta[:, 0:2].astype(jnp.int32)
    rank = meta[:, 4:6].astype(jnp.int32)
    dest = jnp.sum(jnp.where(e[..., None] == experts, pstart, 0), axis=-1) + rank
    n_blocks = -(-n_assign // bm) + N_EXPERTS
    spare = 2 * t + jnp.arange(bm, dtype=jnp.int32)
    arow = jnp.tile(spare, n_blocks).at[dest.reshape(-1)].set(
        jnp.arange(n_assign, dtype=jnp.int32), unique_indices=True, mode='promise_in_bounds')
    arow = jnp.concatenate([spare, arow, spare, spare])
    blk_start = jnp.arange(n_blocks, dtype=jnp.int32) * bm
    blk_e = jnp.minimum(jnp.sum((pend[None, :] <= blk_start[:, None]).astype(jnp.int32), axis=1),
                        N_EXPERTS - 1)
    nvalid = jnp.clip((pstart + cnt)[blk_e] - blk_start, 0, bm)
    nvalid = jnp.where(blk_start < pend[-1], nvalid, 0).astype(jnp.int32)
    blk_e = jnp.concatenate([blk_e, blk_e[-1:]]).astype(jnp.int32)
    nvalid = jnp.concatenate([nvalid, jnp.zeros((1,), jnp.int32)])
    return arow, blk_e, nvalid


def kernel(x, p, norm_mix, norm_ffn, norm_ple, final_norm, ple_gate, ple_proj, ev_w_in, ev_w_out,
           hg_lb_logits, hg_norm_w, m2_conv_w, m2_conv_b, m2_dt_bias, m2_a_log, m2_d, m2_norm_w,
           s5_a_re, s5_a_im, s5_log_step, s5_b_re, s5_b_im, s5_c_re, s5_c_im, s5_d, s5_glu_w,
           ffn_w_gu, ffn_w_down, moe_router, moe_w_gu, moe_w_down):
    B, S, D = x.shape
    T = B * S
    depth = norm_mix.shape[0]
    lb_soft = jax.nn.softmax(hg_lb_logits.astype(F32), axis=0)
    hg_lb = jnp.cumsum(lb_soft, axis=0) - lb_soft[0]
    w_in = jnp.pad(ev_w_in, ((0, 0), (0, 0), (0, EVEN_IN_PAD - ev_w_in.shape[2]))).astype(BF16)
    w_dt_t = jnp.swapaxes(ev_w_in[:, :, OFF_DT:], 1, 2).astype(BF16)
    w_out = ev_w_out.astype(BF16)
    w_moe_gu = moe_w_gu.reshape((-1,) + moe_w_gu.shape[2:])
    w_moe_down = moe_w_down.reshape((-1,) + moe_w_down.shape[2:])
    w_glu = s5_glu_w.astype(BF16)
    w_ple_gate, w_ple_proj = ple_gate.astype(BF16), ple_proj.astype(BF16)
    p_rows = p.reshape(depth, T, p.shape[-1])
    wst, kin, wc, lam = s5_tables(s5_a_re, s5_a_im, s5_log_step, s5_b_re, s5_b_im, s5_c_re, s5_c_im)
    h = x.reshape(T, D)
    for layer in range(depth):
        j = layer // 2
        if layer % 2 == 0:
            proj, dt_rows = rms_matmul(h, norm_mix[layer], w_in, w_dt_t, j)
            proj = proj.reshape(B, S, EVEN_IN_PAD)
            o_a = hgrn2(proj, hg_lb[j], hg_norm_w[j])
            o_b = ssd(proj, dt_rows, m2_conv_w[j], m2_conv_b[j], m2_dt_bias[j], m2_a_log[j],
                      m2_d[j], m2_norm_w[j])
            h = ffn(h, o_a.reshape(T, HG_WIDTH), o_b.reshape(T, M2_INNER), w_out,
                    norm_ffn[layer], ffn_w_gu, ffn_w_down, j)
            y2 = meta = None
        else:
            h3 = h.reshape(B, S, D)
            y = s5_scan(s5_pre(h3, norm_mix[layer]), wst, kin, wc, lam, B, j)
            h = s5_post(h3, y, norm_mix[layer], s5_d[j], w_glu, j).reshape(T, D)
            meta, counts = router(h, norm_ffn[layer], moe_router[j])
            arow, blk_e, nvalid = _moe_plan(meta, counts, MOE_BM)
            y2 = moe_experts(h, norm_ffn[layer], w_moe_gu, w_moe_down, arow, blk_e, nvalid, j)
        h = ple(h, p_rows, norm_ple[layer], w_ple_gate, w_ple_proj, final_norm, layer,
                y2=y2, meta=meta, final=(layer == depth - 1))
    return h.reshape(B, S, D)
```

```python
import functools

import numpy as np
import jax
import jax.numpy as jnp
from jax import lax
from jax.experimental import pallas as pl
from jax.experimental.pallas import tpu as pltpu

F32 = jnp.float32
BF16 = jnp.bfloat16

RMS_EPS = 1e-6
LB_FLOOR = 1e-30
NEG_LOG2E = -1.4426950408889634
LANE = 128
VMEM_LIMIT = 56 * 1024 * 1024

HG_HEADS = 4
HG_DIM = 128
HG_WIDTH = HG_HEADS * HG_DIM
HG_CHUNK = 128
M2_HEADS = 8
M2_P = 64
M2_INNER = M2_HEADS * M2_P
M2_GROUPS = 2
M2_N = 128
M2_CHUNK = 128
M2_XBC = M2_INNER + 2 * M2_GROUPS * M2_N
S5_GROUP_SIZE = 16
S5_STATE = 64
S5_STEPS = 8
N_EXPERTS = 8
MOE_BM = 576

OFF_Q, OFF_FF, OFF_FB, OFF_I, OFF_G = (k * HG_WIDTH for k in range(5))
OFF_Z = 5 * HG_WIDTH
OFF_XBC = OFF_Z + M2_INNER
OFF_DT = OFF_XBC + M2_XBC
EVEN_IN_PAD = OFF_DT + LANE


def _cparams(*sem):
    return pltpu.CompilerParams(dimension_semantics=sem, vmem_limit_bytes=VMEM_LIMIT)


def _rms(x, w):
    return x * lax.rsqrt(jnp.mean(x * x, axis=-1, keepdims=True) + RMS_EPS) * w


def _sigmoid(x):
    return 1.0 / (1.0 + jnp.exp2(x * NEG_LOG2E))


def _silu(x):
    return x * _sigmoid(x)


def _dot(a, b):
    return jnp.dot(a, b, preferred_element_type=F32)


def _dot_nt(a, b):
    return lax.dot_general(a, b, (((1,), (1,)), ((), ())), preferred_element_type=F32)


def _split3(x):
    hi = x.astype(BF16)
    r1 = x - hi.astype(F32)
    mid = r1.astype(BF16)
    lo = (r1 - mid.astype(F32)).astype(BF16)
    return hi, mid, lo


def _rms_matmul_kernel(x_ref, nw_ref, w_ref, wdt_ref, o_ref, dtr_ref):
    xn = _rms(x_ref[...], nw_ref[...]).astype(BF16)
    o_ref[...] = _dot(xn, w_ref[...])
    dtr_ref[...] = _dot_nt(wdt_ref[...], xn)


def rms_matmul(x, nw, w, w_dt_t, layer, tm=512):
    m, k = x.shape
    n = w.shape[2]
    nr = w_dt_t.shape[1]
    return pl.pallas_call(
        _rms_matmul_kernel,
        grid=(m // tm,),
        in_specs=[pl.BlockSpec((tm, k), lambda i: (i, 0)),
                  pl.BlockSpec((1, k), lambda i: (0, 0)),
                  pl.BlockSpec((None, k, n), lambda i: (layer, 0, 0)),
                  pl.BlockSpec((None, nr, k), lambda i: (layer, 0, 0))],
        out_specs=[pl.BlockSpec((tm, n), lambda i: (i, 0)),
                   pl.BlockSpec((nr, tm), lambda i: (0, i))],
        out_shape=[jax.ShapeDtypeStruct((m, n), F32), jax.ShapeDtypeStruct((nr, m), F32)],
        compiler_params=_cparams("parallel"),
        name="rms_matmul",
    )(x, nw.reshape(1, k), w, w_dt_t)


def _hgrn2_masks(L):
    t = np.arange(L)
    masks = []
    w = L // 2
    while w >= 1:
        blk = t // w
        masks.append((blk % 2 == 1)[:, None] & (blk[None, :] == blk[:, None] - 1))
        w //= 2
    masks.append(np.eye(L, dtype=bool))
    m_f = np.stack(masks).astype(np.float32)
    return m_f, np.ascontiguousarray(m_f[:, ::-1, ::-1]), len(masks) - 1


def _pair_ref(b, w, reverse):
    L, d = b.shape
    off = w if reverse else w - 1
    if 2 * w > 8:
        pieces = [jnp.broadcast_to(b[p * 2 * w + off:p * 2 * w + off + 1, :], (2 * w, d))
                  for p in range(L // (2 * w))]
        return pieces[0] if len(pieces) == 1 else jnp.concatenate(pieces, axis=0)
    b3 = b.reshape(L // 8, 8, d)
    sub = lax.broadcasted_iota(jnp.int32, b3.shape, 1)
    ref = None
    for p in range(8 // (2 * w)):
        row = jnp.broadcast_to(b3[:, p * 2 * w + off:p * 2 * w + off + 1, :], b3.shape)
        ref = row if ref is None else jnp.where(sub >= p * 2 * w, row, ref)
    return ref.reshape(L, d)


def _hgrn2_kernel(qp_ref, ffp_ref, fbp_ref, ip_ref, gp_ref, lb_ref, nw_ref,
                  trif_ref, trib_ref, mf_ref, mb_ref, o_ref,
                  of_scr, ob_scr, stf_scr, stb_scr, *, L, nl):
    S = qp_ref.shape[0]
    nc = S // L
    lb = lb_ref[...]
    lb_floor = jnp.maximum(lb, LB_FLOOR)
    one_m_lb = 1.0 - lb

    def chunk(c, fpre_ref, tri_ref, m_ref, st_scr, reverse):
        r0 = pl.multiple_of(c * L, L)
        q = _silu(qp_ref[pl.ds(r0, L), :])
        v = ip_ref[pl.ds(r0, L), :]
        f = lb_floor + one_m_lb * _sigmoid(fpre_ref[pl.ds(r0, L), :])
        k = 1.0 - f
        hi, mid, lo = _split3(jnp.log2(f))
        tri = tri_ref[...]
        b = _dot(tri, hi) + _dot(tri, mid) + _dot(tri, lo)
        qb = q.astype(BF16)
        kb = k.astype(BF16)
        a = m_ref[nl] * _dot_nt(qb, kb)
        w = L // 2
        for lvl in range(nl):
            e = jnp.exp2(-jnp.abs(b - _pair_ref(b, w, reverse))).astype(BF16)
            a = a + m_ref[lvl] * _dot_nt(qb * e, kb * e)
            w //= 2
        tot_row = b[0:1] if reverse else b[L - 1:L]
        q_in = (q * jnp.exp2(b)).astype(BF16)
        k_out = (k * jnp.exp2(tot_row - b)).astype(BF16)
        st = st_scr[...]
        o = _dot(a.astype(BF16), v.astype(BF16)) + _dot_nt(q_in, st.astype(BF16))
        st_scr[...] = st * jnp.exp2(tot_row) + _dot(v.T.astype(BF16), k_out)
        return r0, o

    stf_scr[...] = jnp.zeros_like(stf_scr)
    stb_scr[...] = jnp.zeros_like(stb_scr)

    def body(ci, carry):
        r0, o = chunk(ci, ffp_ref, trif_ref, mf_ref, stf_scr, False)
        of_scr[pl.ds(r0, L), :] = o
        r0, o = chunk(nc - 1 - ci, fbp_ref, trib_ref, mb_ref, stb_scr, True)
        ob_scr[pl.ds(r0, L), :] = o
        return carry

    lax.fori_loop(0, nc, body, 0, unroll=4)
    nw = nw_ref[...]
    R = 4 * L if S % (4 * L) == 0 else L

    def finish(i, carry):
        r0 = pl.multiple_of(i * R, R)
        o = of_scr[pl.ds(r0, R), :] + ob_scr[pl.ds(r0, R), :]
        o = o * lax.rsqrt(jnp.mean(o * o, axis=-1, keepdims=True) + RMS_EPS)
        o = o * nw * _sigmoid(gp_ref[pl.ds(r0, R), :])
        o_ref[pl.ds(r0, R), :] = o.astype(o_ref.dtype)
        return carry

    lax.fori_loop(0, S // R, finish, 0)


def hgrn2(proj, lb, norm_w, L=HG_CHUNK):
    B, S, _ = proj.shape
    m_f, m_b, nl = _hgrn2_masks(L)
    t = np.arange(L)
    tri_f = jnp.asarray((t[None, :] <= t[:, None]).astype(np.float32), BF16)
    tri_b = jnp.asarray((t[None, :] >= t[:, None]).astype(np.float32), BF16)
    d = HG_DIM
    nb = HG_WIDTH // d

    def col(off):
        return pl.BlockSpec((None, S, d), lambda b, h, off=off: (b, 0, off // d + h))

    vec = pl.BlockSpec((1, d), lambda b, h: (0, h))
    const2 = lambda a: pl.BlockSpec(a.shape, lambda b, h: (0, 0))
    const3 = lambda a: pl.BlockSpec(a.shape, lambda b, h: (0, 0, 0))
    return pl.pallas_call(
        functools.partial(_hgrn2_kernel, L=L, nl=nl),
        grid=(B, nb),
        in_specs=[col(OFF_Q), col(OFF_FF), col(OFF_FB), col(OFF_I), col(OFF_G), vec, vec,
                  const2(tri_f), const2(tri_b), const3(m_f), const3(m_b)],
        out_specs=pl.BlockSpec((None, S, d), lambda b, h: (b, 0, h)),
        out_shape=jax.ShapeDtypeStruct((B, S, HG_WIDTH), BF16),
        scratch_shapes=[pltpu.VMEM((S, d), F32), pltpu.VMEM((S, d), F32),
                        pltpu.VMEM((d, d), F32), pltpu.VMEM((d, d), F32)],
        compiler_params=_cparams("parallel", "parallel"),
        name="hgrn2",
    )(proj, proj, proj, proj, proj, lb.reshape(1, HG_WIDTH), norm_w.reshape(1, HG_WIDTH),
      tri_f, tri_b, jnp.asarray(m_f), jnp.asarray(m_b))


def _shift_rows(x, k):
    if k == 0:
        return x
    n = x.shape[0]
    rolled = pltpu.roll(x, (-k) % n, 0)
    t = lax.broadcasted_iota(jnp.int32, x.shape, 0)
    ok = (t + k >= 0) & (t + k < n)
    return jnp.where(ok, rolled, 0.0)


def _conv_silu(x, w, b):
    half = w.shape[0] // 2
    acc = b
    for j in range(w.shape[0]):
        acc = acc + w[j:j + 1, :] * _shift_rows(x, j - half)
    return _silu(acc)


def _ssd_kernel(x_ref, b_ref, c_ref, z_ref, dtc_ref, dtr_ref,
                cwx_ref, cwb_ref, cwc_ref, cbx_ref, cbb_ref, cbc_ref,
                dtbc_ref, dtbr_ref, alr_ref, alc_ref, dsk_ref, nw_ref,
                tril_ref, triu_ref,
                o_ref, xs_scr, bs_scr, cs_scr, yf_scr, yb_scr, stf_scr, stb_scr, *, L, hg, P):
    S = x_ref.shape[0]
    nc = S // L
    nh = 2 * hg
    gw = hg * P
    N = bs_scr.shape[1]

    a_row = jnp.exp(alr_ref[...]) * NEG_LOG2E
    a_col = jnp.exp(alc_ref[...]) * NEG_LOG2E
    tril = tril_ref[...]
    triu = triu_ref[...]
    ti = lax.broadcasted_iota(jnp.int32, (L, L), 0)
    si = lax.broadcasted_iota(jnp.int32, (L, L), 1)
    lane2 = lax.broadcasted_iota(jnp.int32, (L, 2 * P), 1)

    def softplus(v):
        return jnp.maximum(v, 0.0) + jnp.log(1.0 + jnp.exp(-jnp.abs(v)))

    def per_head(cols):
        tiles = []
        for j in range(0, hg, 2):
            lo = jnp.broadcast_to(cols[j], (L, 2 * P))
            hi = jnp.broadcast_to(cols[j + 1], (L, 2 * P))
            tiles.append(jnp.where(lane2 < P, lo, hi))
        return jnp.concatenate(tiles, axis=1)

    def chunk(c, d, g, st_scr):
        cum_c = tril if d == 0 else triu
        cum_r = triu if d == 0 else tril
        r0 = pl.multiple_of(c * L, L)
        x = xs_scr[pl.ds(r0, L), :]
        bm = bs_scr[pl.ds(r0, L), :]
        cm = cs_scr[pl.ds(r0, L), :]
        dt_c = softplus(dtc_ref[pl.ds(r0, L), :] + dtbc_ref[...])
        dt_r = softplus(dtr_ref[:, pl.ds(r0, L)] + dtbr_ref[...])
        h1, h2, h3 = _split3(dt_c * a_row)
        acum_c = _dot(cum_c, h1) + _dot(cum_c, h2) + _dot(cum_c, h3)
        g1, g2, g3 = _split3(dt_r * a_col)
        acum_r = _dot(g1, cum_r) + _dot(g2, cum_r) + _dot(g3, cum_r)
        ones_rows = jnp.ones((8, L), BF16)
        tot = (_dot(ones_rows, h1) + _dot(ones_rows, h2) + _dot(ones_rows, h3))[0:1]
        cb = _dot_nt(cm.astype(BF16), bm.astype(BF16))
        keep = (si <= ti) if d == 0 else (si >= ti)
        ys, in_cols, out_cols, dec_cols = [], [], [], []
        for hh in range(hg):
            j = d * nh + g * hg + hh
            ac = acum_c[:, j:j + 1]
            dc = dt_c[:, j:j + 1]
            te = tot[:, j:j + 1]
            ar = acum_r[j:j + 1, :]
            dr = dt_r[j:j + 1, :]
            seg = ac - ar
            decay = jnp.where(keep, jnp.exp2(jnp.where(keep, seg, 0.0)), 0.0)
            w = (cb * decay * dr).astype(BF16)
            ys.append(_dot(w, x[:, hh * P:(hh + 1) * P].astype(BF16)))
            in_cols.append(jnp.exp2(te - ac) * dc)
            out_cols.append(jnp.exp2(ac))
            dec_cols.append(jnp.broadcast_to(jnp.exp2(te), (1, P)))
        y_diag = jnp.concatenate(ys, axis=1)
        st = st_scr[...]
        y_off = _dot(cm.astype(BF16), st.astype(BF16)) * per_head(out_cols)
        x_in = (x * per_head(in_cols)).astype(BF16)
        st_scr[...] = st * jnp.concatenate(dec_cols, axis=1) + _dot(bm.T.astype(BF16), x_in)
        return r0, y_diag + y_off

    R = 2 * L
    for g in range(x_ref.shape[1] // gw):
        cols = slice(g * gw, (g + 1) * gw)
        ncols = slice(g * N, (g + 1) * N)
        xs_scr[...] = _conv_silu(x_ref[:, cols], cwx_ref[:, cols], cbx_ref[:, cols])
        bs_scr[...] = _conv_silu(b_ref[:, ncols], cwb_ref[:, ncols], cbb_ref[:, ncols])
        cs_scr[...] = _conv_silu(c_ref[:, ncols], cwc_ref[:, ncols], cbc_ref[:, ncols])
        stf_scr[...] = jnp.zeros_like(stf_scr)
        stb_scr[...] = jnp.zeros_like(stb_scr)

        def body(ci, carry, g=g):
            r0, y = chunk(ci, 0, g, stf_scr)
            yf_scr[pl.ds(r0, L), :] = y
            r0, y = chunk(nc - 1 - ci, 1, g, stb_scr)
            yb_scr[pl.ds(r0, L), :] = y
            return carry

        lax.fori_loop(0, nc, body, 0, unroll=2)

        def finish(i, carry, cols=cols):
            r0 = pl.multiple_of(i * R, R)
            y = (yf_scr[pl.ds(r0, R), :] + yb_scr[pl.ds(r0, R), :]
                 + dsk_ref[:, cols] * xs_scr[pl.ds(r0, R), :])
            y = y * _silu(z_ref[pl.ds(r0, R), cols])
            y = y * lax.rsqrt(jnp.mean(y * y, axis=-1, keepdims=True) + RMS_EPS)
            o_ref[pl.ds(r0, R), cols] = (y * nw_ref[:, cols]).astype(o_ref.dtype)
            return carry

        lax.fori_loop(0, S // R, finish, 0)


def ssd(proj, dt_rows, conv_w, conv_b, dt_bias, a_log, d_skip, norm_w, L=M2_CHUNK):
    B, S, _ = proj.shape
    hg = M2_HEADS // M2_GROUPS
    gw = hg * M2_P
    t = np.arange(L)
    tril = jnp.asarray((t[None, :] <= t[:, None]).astype(np.float32), BF16)
    triu = jnp.asarray((t[None, :] >= t[:, None]).astype(np.float32), BF16)
    nb_c = OFF_XBC + M2_INNER
    nc_c = nb_c + M2_GROUPS * M2_N
    xw = conv_w[:, :M2_INNER]
    bw = conv_w[:, M2_INNER:M2_INNER + M2_GROUPS * M2_N]
    cw = conv_w[:, M2_INNER + M2_GROUPS * M2_N:]
    cb2 = conv_b.reshape(1, -1)
    xb = cb2[:, :M2_INNER]
    bb = cb2[:, M2_INNER:M2_INNER + M2_GROUPS * M2_N]
    cbb = cb2[:, M2_INNER + M2_GROUPS * M2_N:]
    nh2 = 2 * M2_HEADS
    dtb_row = jnp.zeros((1, LANE), F32).at[0, :nh2].set(dt_bias.reshape(-1))
    al_row = jnp.zeros((1, LANE), F32).at[0, :nh2].set(a_log.reshape(-1))
    dtb_col = dt_bias.reshape(nh2, 1)
    al_col = a_log.reshape(nh2, 1)
    dsk = jnp.repeat(d_skip, M2_P).reshape(1, M2_INNER)
    nw = norm_w.reshape(1, M2_INNER)

    gn = M2_GROUPS * M2_N
    full2 = lambda a: pl.BlockSpec(a.shape, lambda b: (0, 0))
    return pl.pallas_call(
        functools.partial(_ssd_kernel, L=L, hg=hg, P=M2_P),
        grid=(B,),
        in_specs=[
            pl.BlockSpec((None, S, M2_INNER), lambda b: (b, 0, OFF_XBC // M2_INNER)),
            pl.BlockSpec((None, S, gn), lambda b: (b, 0, nb_c // gn)),
            pl.BlockSpec((None, S, gn), lambda b: (b, 0, nc_c // gn)),
            pl.BlockSpec((None, S, M2_INNER), lambda b: (b, 0, OFF_Z // M2_INNER)),
            pl.BlockSpec((None, S, LANE), lambda b: (b, 0, OFF_DT // LANE)),
            pl.BlockSpec((nh2, S), lambda b: (0, b)),
            full2(xw), full2(bw), full2(cw), full2(xb), full2(bb), full2(cbb),
            full2(dtb_row), full2(dtb_col), full2(al_row), full2(al_col),
            full2(dsk), full2(nw), full2(tril), full2(triu),
        ],
        out_specs=pl.BlockSpec((None, S, M2_INNER), lambda b: (b, 0, 0)),
        out_shape=jax.ShapeDtypeStruct((B, S, M2_INNER), BF16),
        scratch_shapes=[pltpu.VMEM((S, gw), F32), pltpu.VMEM((S, M2_N), F32),
                        pltpu.VMEM((S, M2_N), F32), pltpu.VMEM((S, gw), F32),
                        pltpu.VMEM((S, gw), F32), pltpu.VMEM((M2_N, gw), F32),
                        pltpu.VMEM((M2_N, gw), F32)],
        compiler_params=_cparams("parallel"),
        name="ssd",
    )(proj, proj, proj, proj, proj, dt_rows, xw, bw, cw, xb, bb, cbb,
      dtb_row, dtb_col, al_row, al_col, dsk, nw, tril, triu)


def _ffn_kernel(h_ref, oa_ref, ob_ref, wa_ref, wb_ref, nw_ref, wg_ref, wu_ref, wd_ref, o_ref,
                xn_scr, acc_scr):
    f = pl.program_id(1)

    @pl.when(f == 0)
    def _():
        x = h_ref[...] + _dot(oa_ref[...], wa_ref[...]) + _dot(ob_ref[...], wb_ref[...])
        xn_scr[...] = _rms(x, nw_ref[...]).astype(BF16)
        acc_scr[...] = x

    xn = xn_scr[...]
    act = _silu(_dot(xn, wg_ref[...].astype(BF16))) * _dot(xn, wu_ref[...].astype(BF16))
    acc_scr[...] += _dot(act.astype(BF16), wd_ref[...].astype(BF16))

    @pl.when(f == pl.num_programs(1) - 1)
    def _():
        o_ref[...] = acc_scr[...]


def ffn(h, oa, ob, w_out, nw, w_gu, w_down, layer, tm=1024, tf=512):
    m, d = h.shape
    dff = w_down.shape[1]
    nf = dff // tf
    ka, kb = oa.shape[1], ob.shape[1]
    assert ka == kb
    return pl.pallas_call(
        _ffn_kernel,
        grid=(m // tm, nf),
        in_specs=[pl.BlockSpec((tm, d), lambda i, f: (i, 0)),
                  pl.BlockSpec((tm, ka), lambda i, f: (i, 0)),
                  pl.BlockSpec((tm, kb), lambda i, f: (i, 0)),
                  pl.BlockSpec((None, ka, d), lambda i, f: (layer, 0, 0)),
                  pl.BlockSpec((None, kb, d), lambda i, f: (layer, 1, 0)),
                  pl.BlockSpec((1, d), lambda i, f: (0, 0)),
                  pl.BlockSpec((None, d, tf), lambda i, f: (layer, 0, f)),
                  pl.BlockSpec((None, d, tf), lambda i, f: (layer, 0, nf + f)),
                  pl.BlockSpec((None, tf, d), lambda i, f: (layer, f, 0))],
        out_specs=pl.BlockSpec((tm, d), lambda i, f: (i, 0)),
        out_shape=jax.ShapeDtypeStruct((m, d), F32),
        scratch_shapes=[pltpu.VMEM((tm, d), BF16), pltpu.VMEM((tm, d), F32)],
        compiler_params=_cparams("parallel", "arbitrary"),
        name="ffn",
    )(h, oa, ob, w_out, w_out, nw.reshape(1, d), w_gu, w_gu, w_down)


def _ple_kernel(*refs, moe, final):
    if moe:
        h_ref, y2_ref, meta_ref, p_ref, nw_ref, wg_ref, wp_ref, fw_ref, o_ref = refs
        meta = meta_ref[...]
        d = h_ref.shape[1]
        h = h_ref[...] + (meta[:, 2:3] * y2_ref[:, 0:d] + meta[:, 3:4] * y2_ref[:, d:2 * d])
    else:
        h_ref, p_ref, nw_ref, wg_ref, wp_ref, fw_ref, o_ref = refs
        h = h_ref[...]
    gate = _sigmoid(_dot(_rms(h, nw_ref[...]).astype(BF16), wg_ref[...]))
    h = h + gate * _dot(p_ref[...].astype(BF16), wp_ref[...])
    if final:
        h = _rms(h, fw_ref[...])
    o_ref[...] = h


def ple(h, p, nw, wg, wp, fw, layer, y2=None, meta=None, final=False, tm=512):
    m, d = h.shape
    moe = y2 is not None
    row = lambda w: pl.BlockSpec((tm, w), lambda i: (i, 0))
    slab = lambda a: pl.BlockSpec((None,) + a.shape[1:], lambda i: (layer, 0, 0))
    vec = pl.BlockSpec((1, d), lambda i: (0, 0))
    in_specs = [row(d)]
    args = [h]
    if moe:
        in_specs += [row(2 * d), row(meta.shape[1])]
        args += [y2, meta]
    in_specs += [pl.BlockSpec((None, tm, p.shape[2]), lambda i: (layer, i, 0)), vec, slab(wg), slab(wp), vec]
    args += [p, nw.reshape(1, d), wg, wp, fw.reshape(1, d)]
    return pl.pallas_call(
        functools.partial(_ple_kernel, moe=moe, final=final),
        grid=(m // tm,),
        in_specs=in_specs,
        out_specs=row(d),
        out_shape=jax.ShapeDtypeStruct((m, d), F32),
        compiler_params=_cparams("parallel"),
        name="ple",
    )(*args)


def _s5_strips(a_re, a_im, log_step, b_re, b_im, c_re, c_im):
    G, N = a_re.shape[1:]
    C = S5_GROUP_SIZE
    T = S5_STEPS
    gt = LANE // C
    Z = G // gt
    tau = jnp.arange(T + 1, dtype=F32)
    steps = jnp.arange(T)

    def cmul(xr, xi, yr, yi):
        return xr * yr - xi * yi, xr * yi + xi * yr

    st_parts, rd_parts, k_parts, lam_rows = [], [], [], []
    for d in range(2):
        delta = jnp.exp(log_step[d])[:, None]
        ar, ai = a_re[d], a_im[d]
        mag = jnp.exp(ar * delta)
        lam_re, lam_im = mag * jnp.cos(ai * delta), mag * jnp.sin(ai * delta)
        den = ar * ar + ai * ai
        num_re = lam_re - 1.0
        coef_re = (num_re * ar + lam_im * ai) / den
        coef_im = (lam_im * ar - num_re * ai) / den
        br = coef_re[..., None] * b_re - coef_im[..., None] * b_im
        bi = coef_re[..., None] * b_im + coef_im[..., None] * b_re
        cr, ci = c_re[d], c_im[d]
        pm = jnp.exp((ar * delta)[None] * tau[:, None, None])
        ang = (ai * delta)[None] * tau[:, None, None]
        pr, pi = pm * jnp.cos(ang), pm * jnp.sin(ang)
        e_in = (T - 1 - steps) if d == 0 else steps
        sr, si = cmul(pr[e_in][..., None], pi[e_in][..., None], br[None], bi[None])
        e_out = (steps + 1) if d == 0 else (T - steps)
        cpr, cpi = cmul(cr[None], ci[None], pr[e_out][:, :, None, :], pi[e_out][:, :, None, :])
        lbr, lbi = cmul(pr[:T][..., None], pi[:T][..., None], br[None], bi[None])
        ktau = jnp.einsum('gon,tgni->tgoi', cr, lbr) - jnp.einsum('gon,tgni->tgoi', ci, lbi)
        st_parts += [sr, si]
        rd_parts += [cpr, -cpi]
        k_parts.append(ktau)
        lam_rows += [pr[T], pi[T]]

    NS = gt * N
    st = jnp.stack(st_parts).reshape(4, T, Z, gt, N, C).transpose(2, 1, 5, 0, 3, 4)
    rd = jnp.stack(rd_parts).reshape(4, T, Z, gt, C, N).transpose(2, 1, 4, 0, 3, 5)
    lag = steps[None, :] - steps[:, None]
    sel_f = (lag[:, :, None] == steps[None, None, :]).astype(F32)
    sel_b = (-lag[:, :, None] == steps[None, None, :]).astype(F32)
    toe = (jnp.einsum('stk,kgoi->stgoi', sel_f, k_parts[0])
           + jnp.einsum('stk,kgoi->stgoi', sel_b, k_parts[1]))
    toe = toe.reshape(T, T, Z, gt, C, C).transpose(2, 0, 5, 1, 3, 4)
    lam = jnp.stack([r.reshape(Z, NS) for r in lam_rows], axis=1)
    lam = jnp.concatenate([lam, jnp.zeros_like(lam)], axis=1)
    return (st.reshape(Z, T * C, 4 * NS), toe.reshape(Z, T * C, T * LANE),
            rd.reshape(Z, T * C, 4 * NS), lam)


def s5_tables(a_re, a_im, log_step, b_re, b_im, c_re, c_im):
    st, toe, rd, lam = jax.vmap(_s5_strips)(a_re, a_im, log_step, b_re, b_im, c_re, c_im)
    flat = lambda a: a.reshape((-1,) + a.shape[2:])
    wst = block_diag_rows(flat(st), S5_GROUP_SIZE, S5_STATE)
    kin = block_diag_rows(flat(toe), S5_GROUP_SIZE, S5_GROUP_SIZE)
    wc = block_diag_rows(flat(rd), S5_GROUP_SIZE, S5_STATE)
    return wst, kin, wc, flat(lam)


def _block_diag_kernel(s_ref, o_ref, *, C, gcol):
    n = o_ref.shape[1]
    gt = LANE // C
    row_g = lax.broadcasted_iota(jnp.int32, (LANE, n), 0) // C
    col_g = (lax.broadcasted_iota(jnp.int32, (LANE, n), 1) // gcol) % gt
    keep = row_g == col_g
    for t in range(o_ref.shape[0] // LANE):
        strip = s_ref[t * C:(t + 1) * C, :]
        tiled = jnp.broadcast_to(strip[None], (gt, C, n)).reshape(LANE, n)
        o_ref[t * LANE:(t + 1) * LANE, :] = jnp.where(keep, tiled, 0.0).astype(o_ref.dtype)


def block_diag_rows(strips, C, gcol):
    Z, rows, n = strips.shape
    gt = LANE // C
    return pl.pallas_call(
        functools.partial(_block_diag_kernel, C=C, gcol=gcol),
        grid=(Z,),
        in_specs=[pl.BlockSpec((None, rows, n), lambda z: (z, 0, 0))],
        out_specs=pl.BlockSpec((None, rows * gt, n), lambda z: (z, 0, 0)),
        out_shape=jax.ShapeDtypeStruct((Z, rows * gt, n), BF16),
        compiler_params=_cparams("parallel"),
        name="block_diag_rows",
    )(strips)


def _s5_pre_kernel(x_ref, nw_ref, u_ref, xn_scr, tmp_scr):
    nb, R, D = x_ref.shape
    T = S5_STEPS
    rc = R // T
    xn = _rms(x_ref[...], nw_ref[...])
    for z in range(D // LANE):
        xn_scr[z] = xn[:, :, z * LANE:(z + 1) * LANE]
    for z in range(D // LANE):
        for t in range(T):
            for b in range(nb):
                tmp_scr[z, t, pl.ds(b, rc, stride=nb), :] = xn_scr[z, b, pl.ds(t, rc, stride=T), :]
            u_ref[z, :, t * LANE:(t + 1) * LANE] = tmp_scr[z, t].astype(u_ref.dtype)


def s5_pre(h3, nw, R=64):
    B, S, D = h3.shape
    T = S5_STEPS
    Z = D // LANE
    rc = R // T
    return pl.pallas_call(
        _s5_pre_kernel,
        grid=(S // R,),
        in_specs=[pl.BlockSpec((B, R, D), lambda i: (0, i, 0)),
                  pl.BlockSpec((1, D), lambda i: (0, 0))],
        out_specs=pl.BlockSpec((Z, rc * B, T * LANE), lambda i: (0, i, 0)),
        out_shape=jax.ShapeDtypeStruct((Z, S // T * B, T * LANE), BF16),
        scratch_shapes=[pltpu.VMEM((Z, B, R, LANE), F32), pltpu.VMEM((Z, T, rc * B, LANE), F32)],
        compiler_params=_cparams("parallel"),
        name="s5_pre",
    )(h3, nw.reshape(1, D))


def _s5_kernel(u_ref, wst_ref, kin_ref, wc_ref, lam_ref, y_ref, s_scr, *, nb, rblk):
    rows = u_ref.shape[0]
    nc = rows // nb
    ns = lam_ref.shape[1]
    for r0 in range(0, rows, rblk):
        s_scr[r0:r0 + rblk, :] = _dot(u_ref[r0:r0 + rblk, :], wst_ref[...])
    lam = lam_ref[...]
    lfr, lfi, lbr, lbi = lam[0:1], lam[1:2], lam[2:3], lam[3:4]

    def body(ci, carry):
        hfr, hfi, hbr, hbi = carry
        rf = pl.multiple_of(ci * nb, nb)
        rb = pl.multiple_of((nc - 1 - ci) * nb, nb)
        sfr = s_scr[pl.ds(rf, nb), 0:ns]
        sfi = s_scr[pl.ds(rf, nb), ns:2 * ns]
        sbr = s_scr[pl.ds(rb, nb), 2 * ns:3 * ns]
        sbi = s_scr[pl.ds(rb, nb), 3 * ns:4 * ns]
        s_scr[pl.ds(rf, nb), 0:ns] = hfr
        s_scr[pl.ds(rf, nb), ns:2 * ns] = hfi
        s_scr[pl.ds(rb, nb), 2 * ns:3 * ns] = hbr
        s_scr[pl.ds(rb, nb), 3 * ns:4 * ns] = hbi
        return (lfr * hfr - lfi * hfi + sfr, lfr * hfi + lfi * hfr + sfi,
                lbr * hbr - lbi * hbi + sbr, lbr * hbi + lbi * hbr + sbi)

    z = jnp.zeros((nb, ns), F32)
    lax.fori_loop(0, nc, body, (z, z, z, z))
    for r0 in range(0, rows, rblk):
        y = (_dot(u_ref[r0:r0 + rblk, :], kin_ref[...])
             + _dot_nt(s_scr[r0:r0 + rblk, :].astype(BF16), wc_ref[...]))
        y_ref[r0:r0 + rblk, :] = y.astype(y_ref.dtype)


def s5_scan(u, wst, kin, wc, lam, nb, layer, rblk=512):
    Z, rows, K = u.shape
    z0 = layer * Z
    per = lambda a: pl.BlockSpec((None,) + a.shape[1:], lambda s: (s, 0, 0))
    tab = lambda a: pl.BlockSpec((None,) + a.shape[1:], lambda s: (z0 + s, 0, 0))
    once = lambda a: pl.BlockSpec((None,) + a.shape[1:], lambda s: (z0 + s, 0, 0))
    return pl.pallas_call(
        functools.partial(_s5_kernel, nb=nb, rblk=rblk),
        grid=(Z,),
        in_specs=[per(u), once(wst), once(kin), once(wc), tab(lam)],
        out_specs=pl.BlockSpec((None, rows, K), lambda s: (s, 0, 0)),
        out_shape=jax.ShapeDtypeStruct((Z, rows, K), BF16),
        scratch_shapes=[pltpu.VMEM((rows, wst.shape[2]), F32)],
        compiler_params=_cparams("parallel"),
        name="s5_scan",
    )(u, wst, kin, wc, lam)


def _s5_post_kernel(h_ref, y_ref, nw_ref, d_ref, wo_ref, wg_ref, o_ref, yf_scr, yt_scr):
    nb, R, D = h_ref.shape
    T = S5_STEPS
    rc = R // T
    for z in range(D // LANE):
        for t in range(T):
            yf_scr[z, t] = y_ref[z, :, t * LANE:(t + 1) * LANE].astype(F32)
            for b in range(nb):
                yt_scr[z, b, pl.ds(t, rc, stride=T), :] = yf_scr[z, t, pl.ds(b, rc, stride=nb), :]
    yt = jnp.concatenate([yt_scr[z] for z in range(D // LANE)], axis=-1)
    h = h_ref[...].reshape(nb * R, D)
    y = yt.reshape(nb * R, D) + d_ref[...] * _rms(h, nw_ref[...])
    act = jax.nn.gelu(y).astype(BF16)
    out = h + _dot(act, wo_ref[...]) * _sigmoid(_dot(act, wg_ref[...]))
    o_ref[...] = out.reshape(nb, R, D)


def s5_post(h3, y, nw, d_skip, glu_w, layer, R=64):
    B, S, D = h3.shape
    T = S5_STEPS
    Z = D // LANE
    rc = R // T
    row = pl.BlockSpec((B, R, D), lambda i: (0, i, 0))
    vec = pl.BlockSpec((1, D), lambda i: (0, 0))
    return pl.pallas_call(
        _s5_post_kernel,
        grid=(S // R,),
        in_specs=[row, pl.BlockSpec((Z, rc * B, T * LANE), lambda i: (0, i, 0)), vec, vec,
                  pl.BlockSpec((None, D, D), lambda i: (layer, 0, 0)),
                  pl.BlockSpec((None, D, D), lambda i: (layer, 0, 1))],
        out_specs=row,
        out_shape=jax.ShapeDtypeStruct((B, S, D), F32),
        scratch_shapes=[pltpu.VMEM((Z, T, rc * B, LANE), F32), pltpu.VMEM((Z, B, R, LANE), F32)],
        compiler_params=_cparams("parallel"),
        name="s5_post",
    )(h3, y, nw.reshape(1, D), d_skip.reshape(1, D), glu_w, glu_w)


def _router_kernel(h_ref, nw_ref, whi_ref, wlo_ref, tri_ref, meta_ref, cnt_ref, carry_scr):
    @pl.when(pl.program_id(0) == 0)
    def _():
        carry_scr[...] = jnp.zeros_like(carry_scr)

    xn = _rms(h_ref[...], nw_ref[...])
    x_hi, x_lo, _ = _split3(xn)
    w_hi, w_lo = whi_ref[...], wlo_ref[...]
    logits = _dot(x_hi, w_hi) + (_dot(x_hi, w_lo) + _dot(x_lo, w_hi))
    lane = lax.broadcasted_iota(jnp.int32, logits.shape, 1)
    neg = jnp.float32(-jnp.inf)
    lg = jnp.where(lane < N_EXPERTS, logits, neg)
    t1 = jnp.max(lg, axis=1, keepdims=True)
    i1 = jnp.min(jnp.where(lg == t1, lane, LANE), axis=1, keepdims=True)
    lg2 = jnp.where(lane == i1, neg, lg)
    t2 = jnp.max(lg2, axis=1, keepdims=True)
    i2 = jnp.min(jnp.where(lg2 == t2, lane, LANE), axis=1, keepdims=True)
    ex = jnp.exp(t2 - t1)
    g1 = 1.0 / (1.0 + ex)
    g2 = ex / (1.0 + ex)
    oh1 = jnp.where(lane == i1, 1.0, 0.0)
    oh2 = jnp.where(lane == i2, 1.0, 0.0)
    tri = tri_ref[...]
    before1 = _dot(tri, oh1.astype(BF16))
    before2 = _dot(tri, oh2.astype(BF16))
    tot1 = jnp.sum(oh1, axis=0, keepdims=True)
    tot2 = jnp.sum(oh2, axis=0, keepdims=True)
    carry = carry_scr[0:1]
    rank1 = jnp.sum(oh1 * (carry + before1), axis=1, keepdims=True)
    rank2 = jnp.sum(oh2 * (carry + tot1 + before2), axis=1, keepdims=True)
    counts = jnp.broadcast_to(carry + tot1 + tot2, carry_scr.shape)
    carry_scr[...] = counts
    cnt_ref[...] = counts
    meta = jnp.where(lane == 0, i1.astype(F32), 0.0)
    meta = jnp.where(lane == 1, i2.astype(F32), meta)
    meta = jnp.where(lane == 2, g1, meta)
    meta = jnp.where(lane == 3, g2, meta)
    meta = jnp.where(lane == 4, rank1, meta)
    meta = jnp.where(lane == 5, rank2, meta)
    meta_ref[...] = meta


def router(h, nw, w_router, tm=512):
    m, d = h.shape
    wr = jnp.zeros((d, LANE), F32).at[:, :N_EXPERTS].set(w_router)
    w_hi, w_lo, _ = _split3(wr)
    t = np.arange(tm)
    tri = jnp.asarray((t[None, :] < t[:, None]).astype(np.float32), BF16)
    return pl.pallas_call(
        _router_kernel,
        grid=(m // tm,),
        in_specs=[pl.BlockSpec((tm, d), lambda i: (i, 0)),
                  pl.BlockSpec((1, d), lambda i: (0, 0)),
                  pl.BlockSpec((d, LANE), lambda i: (0, 0)),
                  pl.BlockSpec((d, LANE), lambda i: (0, 0)),
                  pl.BlockSpec((tm, tm), lambda i: (0, 0))],
        out_specs=[pl.BlockSpec((tm, LANE), lambda i: (i, 0)),
                   pl.BlockSpec((8, LANE), lambda i: (0, 0))],
        out_shape=[jax.ShapeDtypeStruct((m, LANE), F32), jax.ShapeDtypeStruct((8, LANE), F32)],
        scratch_shapes=[pltpu.VMEM((8, LANE), F32)],
        compiler_params=_cparams("arbitrary"),
        name="router",
    )(h, nw.reshape(1, d), w_hi, w_lo, tri)


def _moe_kernel(arow_ref, blk_e_ref, nvalid_ref, h_hbm, nw_ref, wg_ref, wu_ref, wd_ref,
                y2_hbm, xg_scr, xn_scr, acc_scr, sem_in, sem_out, *, bm, n_tok):
    g = pl.program_id(0)
    f = pl.program_id(1)
    nf = pl.num_programs(1)
    slot = g % 2
    other = 1 - slot
    d = acc_scr.shape[2]
    rows_per_step = bm // (nf - 1)
    prv, cur, nxt = g * bm, (g + 1) * bm, (g + 2) * bm
    nv = nvalid_ref[g]

    def in_copy(off, r, s):
        tok = jnp.minimum(arow_ref[off + r] >> 1, n_tok - 1)
        return pltpu.make_async_copy(h_hbm.at[pl.ds(tok, 1)], xg_scr.at[s, pl.ds(r, 1)], sem_in.at[s])

    def out_copy(off, r, s):
        a = arow_ref[off + r]
        col = pl.multiple_of((a & 1) * d, d)
        return pltpu.make_async_copy(acc_scr.at[s, pl.ds(r, 1)],
                                     y2_hbm.at[pl.ds(a >> 1, 1), pl.ds(col, d)], sem_out.at[s])

    def wait_in(s):
        pltpu.make_async_copy(h_hbm.at[pl.ds(0, bm)], xg_scr.at[s], sem_in.at[s]).wait()

    def wait_out(s):
        pltpu.make_async_copy(acc_scr.at[s], y2_hbm.at[pl.ds(0, bm), pl.ds(0, d)], sem_out.at[s]).wait()

    @pl.when((g == 0) & (f == 0))
    def _():
        acc_scr[...] = jnp.zeros_like(acc_scr)

        def start(r, c):
            in_copy(cur, r, slot).start()
            return c

        lax.fori_loop(0, bm, start, 0)

    @pl.when(f == 0)
    def _():
        wait_in(slot)
        xn_scr[...] = _rms(xg_scr[slot], nw_ref[...]).astype(BF16)
        acc_scr[slot] = jnp.zeros((bm, d), F32)

    def move_rows(part, parts):
        base = pl.multiple_of(f * rows_per_step, 8)
        per = rows_per_step // parts
        for j in range(part * per, (part + 1) * per):
            in_copy(nxt, base + j, other).start()
            out_copy(prv, base + j, other).start()

    def experts(with_moves):
        parts = 2
        tc = wg_ref.shape[2] // parts
        xn = xn_scr[...]
        upd = None
        for c in range(parts):
            cs = slice(c * tc, (c + 1) * tc)
            if with_moves:
                move_rows(c, parts)
            act = (_silu(_dot(xn, wg_ref[0, :, cs].astype(BF16)))
                   * _dot(xn, wu_ref[0, :, cs].astype(BF16)))
            part = _dot(act.astype(BF16), wd_ref[0, cs, :].astype(BF16))
            upd = part if upd is None else upd + part
        acc_scr[slot] += upd

    @pl.when((nv > 0) & (f < nf - 1))
    def _():
        experts(True)

    @pl.when((nv > 0) & (f == nf - 1))
    def _():
        experts(False)

    @pl.when((nv == 0) & (f < nf - 1))
    def _():
        move_rows(0, 1)

    @pl.when(f == nf - 1)
    def _():
        wait_out(other)

    @pl.when((f == nf - 1) & (g == pl.num_programs(0) - 1))
    def _():
        wait_in(other)


def moe_experts(h, nw, w_gu, w_down, arow, blk_e, nvalid, layer, bm=MOE_BM, tf=512):
    t, d = h.shape
    dff = w_down.shape[1]
    nf = dff // tf
    assert bm % (2 * (nf - 1)) == 0 and bm % 16 == 0
    n_steps = blk_e.shape[0]
    e0 = layer * N_EXPERTS

    def wmap(col0):
        def index(g, f, ar, be, nv):
            return (e0 + be[g], 0, col0 + jnp.where(nv[g] > 0, f, nf - 1))
        return index

    def dmap(g, f, ar, be, nv):
        return (e0 + be[g], jnp.where(nv[g] > 0, f, nf - 1), 0)

    grid_spec = pltpu.PrefetchScalarGridSpec(
        num_scalar_prefetch=3,
        grid=(n_steps, nf),
        in_specs=[pl.BlockSpec(memory_space=pl.ANY),
                  pl.BlockSpec((1, d), lambda g, f, ar, be, nv: (0, 0)),
                  pl.BlockSpec((1, d, tf), wmap(0)),
                  pl.BlockSpec((1, d, tf), wmap(nf)),
                  pl.BlockSpec((1, tf, d), dmap)],
        out_specs=pl.BlockSpec(memory_space=pl.ANY),
        scratch_shapes=[pltpu.VMEM((2, bm, d), F32), pltpu.VMEM((bm, d), BF16),
                        pltpu.VMEM((2, bm, d), F32),
                        pltpu.SemaphoreType.DMA((2,)), pltpu.SemaphoreType.DMA((2,))],
    )
    return pl.pallas_call(
        functools.partial(_moe_kernel, bm=bm, n_tok=t),
        grid_spec=grid_spec,
        out_shape=jax.ShapeDtypeStruct((t + bm // 2, 2 * d), F32),
        compiler_params=_cparams("arbitrary", "arbitrary"),
        name="moe_experts",
    )(arow, blk_e, nvalid, h, nw.reshape(1, d), w_gu, w_gu, w_down)


def _moe_plan(meta, counts, bm):
    t = meta.shape[0]
    n_assign = 2 * t
    experts = jnp.arange(N_EXPERTS, dtype=jnp.int32)
    cnt = counts[0, :N_EXPERTS].astype(jnp.int32)
    padded = (cnt + bm - 1) // bm * bm
    pend = jnp.cumsum(padded)
    pstart = pend - padded
    e = meta[:, 0:2].astype(jnp.int32)
    rank = meta[:, 4:6].astype(jnp.int32)
    dest = jnp.sum(jnp.where(e[..., None] == experts, pstart, 0), axis=-1) + rank
    n_blocks = -(-n_assign // bm) + N_EXPERTS
    spare = 2 * t + jnp.arange(bm, dtype=jnp.int32)
    arow = jnp.tile(spare, n_blocks).at[dest.reshape(-1)].set(
        jnp.arange(n_assign, dtype=jnp.int32), unique_indices=True, mode='promise_in_bounds')
    arow = jnp.concatenate([spare, arow, spare, spare])
    blk_start = jnp.arange(n_blocks, dtype=jnp.int32) * bm
    blk_e = jnp.minimum(jnp.sum((pend[None, :] <= blk_start[:, None]).astype(jnp.int32), axis=1),
                        N_EXPERTS - 1)
    nvalid = jnp.clip((pstart + cnt)[blk_e] - blk_start, 0, bm)
    nvalid = jnp.where(blk_start < pend[-1], nvalid, 0).astype(jnp.int32)
    blk_e = jnp.concatenate([blk_e, blk_e[-1:]]).astype(jnp.int32)
    nvalid = jnp.concatenate([nvalid, jnp.zeros((1,), jnp.int32)])
    return arow, blk_e, nvalid


def kernel(x, p, norm_mix, norm_ffn, norm_ple, final_norm, ple_gate, ple_proj, ev_w_in, ev_w_out,
           hg_lb_logits, hg_norm_w, m2_conv_w, m2_conv_b, m2_dt_bias, m2_a_log, m2_d, m2_norm_w,
           s5_a_re, s5_a_im, s5_log_step, s5_b_re, s5_b_im, s5_c_re, s5_c_im, s5_d, s5_glu_w,
           ffn_w_gu, ffn_w_down, moe_router, moe_w_gu, moe_w_down):
    B, S, D = x.shape
    T = B * S
    depth = norm_mix.shape[0]
    lb_soft = jax.nn.softmax(hg_lb_logits.astype(F32), axis=0)
    hg_lb = jnp.cumsum(lb_soft, axis=0) - lb_soft[0]
    w_in = jnp.pad(ev_w_in, ((0, 0), (0, 0), (0, EVEN_IN_PAD - ev_w_in.shape[2]))).astype(BF16)
    w_dt_t = jnp.swapaxes(ev_w_in[:, :, OFF_DT:], 1, 2).astype(BF16)
    w_out = ev_w_out.astype(BF16)
    w_moe_gu = moe_w_gu.reshape((-1,) + moe_w_gu.shape[2:])
    w_moe_down = moe_w_down.reshape((-1,) + moe_w_down.shape[2:])
    w_glu = s5_glu_w.astype(BF16)
    w_ple_gate, w_ple_proj = ple_gate.astype(BF16), ple_proj.astype(BF16)
    p_rows = p.reshape(depth, T, p.shape[-1])
    wst, kin, wc, lam = s5_tables(s5_a_re, s5_a_im, s5_log_step, s5_b_re, s5_b_im, s5_c_re, s5_c_im)
    h = x.reshape(T, D)
    for layer in range(depth):
        j = layer // 2
        if layer % 2 == 0:
            proj, dt_rows = rms_matmul(h, norm_mix[layer], w_in, w_dt_t, j)
            proj = proj.reshape(B, S, EVEN_IN_PAD)
            o_a = hgrn2(proj, hg_lb[j], hg_norm_w[j])
            o_b = ssd(proj, dt_rows, m2_conv_w[j], m2_conv_b[j], m2_dt_bias[j], m2_a_log[j],
                      m2_d[j], m2_norm_w[j])
            h = ffn(h, o_a.reshape(T, HG_WIDTH), o_b.reshape(T, M2_INNER), w_out,
                    norm_ffn[layer], ffn_w_gu, ffn_w_down, j)
            y2 = meta = None
        else:
            h3 = h.reshape(B, S, D)
            y = s5_scan(s5_pre(h3, norm_mix[layer]), wst, kin, wc, lam, B, j)
            h = s5_post(h3, y, norm_mix[layer], s5_d[j], w_glu, j).reshape(T, D)
            meta, counts = router(h, norm_ffn[layer], moe_router[j])
            arow, blk_e, nvalid = _moe_plan(meta, counts, MOE_BM)
            y2 = moe_experts(h, norm_ffn[layer], w_moe_gu, w_moe_down, arow, blk_e, nvalid, j)
        h = ple(h, p_rows, norm_ple[layer], w_ple_gate, w_ple_proj, final_norm, layer,
                y2=y2, meta=meta, final=(layer == depth - 1))
    return h.reshape(B, S, D)
```

```python
import functools

import numpy as np
import jax
import jax.numpy as jnp
from jax import lax
from jax.experimental import pallas as pl
from jax.experimental.pallas import tpu as pltpu

F32 = jnp.float32
BF16 = jnp.bfloat16

RMS_EPS = 1e-6
LB_FLOOR = 1e-30
NEG_LOG2E = -1.4426950408889634
LANE = 128
VMEM_LIMIT = 56 * 1024 * 1024

HG_HEADS = 4
HG_DIM = 128
HG_WIDTH = HG_HEADS * HG_DIM
HG_CHUNK = 128
M2_HEADS = 8
M2_P = 64
M2_INNER = M2_HEADS * M2_P
M2_GROUPS = 2
M2_N = 128
M2_CHUNK = 128
M2_XBC = M2_INNER + 2 * M2_GROUPS * M2_N
S5_GROUP_SIZE = 16
S5_STATE = 64
S5_STEPS = 8
N_EXPERTS = 8
MOE_BM = 576

OFF_Q, OFF_FF, OFF_FB, OFF_I, OFF_G = (k * HG_WIDTH for k in range(5))
OFF_Z = 5 * HG_WIDTH
OFF_XBC = OFF_Z + M2_INNER
OFF_DT = OFF_XBC + M2_XBC
EVEN_IN_PAD = OFF_DT + LANE


def _cparams(*sem):
    return pltpu.CompilerParams(dimension_semantics=sem, vmem_limit_bytes=VMEM_LIMIT)


def _rms(x, w):
    return x * lax.rsqrt(jnp.mean(x * x, axis=-1, keepdims=True) + RMS_EPS) * w


def _sigmoid(x):
    return 1.0 / (1.0 + jnp.exp2(x * NEG_LOG2E))


def _silu(x):
    return x * _sigmoid(x)


def _dot(a, b):
    return jnp.dot(a, b, preferred_element_type=F32)


def _dot_nt(a, b):
    return lax.dot_general(a, b, (((1,), (1,)), ((), ())), preferred_element_type=F32)


def _split3(x):
    hi = x.astype(BF16)
    r1 = x - hi.astype(F32)
    mid = r1.astype(BF16)
    lo = (r1 - mid.astype(F32)).astype(BF16)
    return hi, mid, lo


def _rms_matmul_kernel(x_ref, nw_ref, w_ref, wdt_ref, o_ref, dtr_ref):
    xn = _rms(x_ref[...], nw_ref[...]).astype(BF16)
    o_ref[...] = _dot(xn, w_ref[...])
    dtr_ref[...] = _dot_nt(wdt_ref[...], xn)


def rms_matmul(x, nw, w, w_dt_t, layer, tm=512):
    m, k = x.shape
    n = w.shape[2]
    nr = w_dt_t.shape[1]
    return pl.pallas_call(
        _rms_matmul_kernel,
        grid=(m // tm,),
        in_specs=[pl.BlockSpec((tm, k), lambda i: (i, 0)),
                  pl.BlockSpec((1, k), lambda i: (0, 0)),
                  pl.BlockSpec((None, k, n), lambda i: (layer, 0, 0)),
                  pl.BlockSpec((None, nr, k), lambda i: (layer, 0, 0))],
        out_specs=[pl.BlockSpec((tm, n), lambda i: (i, 0)),
                   pl.BlockSpec((nr, tm), lambda i: (0, i))],
        out_shape=[jax.ShapeDtypeStruct((m, n), F32), jax.ShapeDtypeStruct((nr, m), F32)],
        compiler_params=_cparams("parallel"),
        name="rms_matmul",
    )(x, nw.reshape(1, k), w, w_dt_t)


def _hgrn2_masks(L):
    t = np.arange(L)
    masks = []
    w = L // 2
    while w >= 1:
        blk = t // w
        masks.append((blk % 2 == 1)[:, None] & (blk[None, :] == blk[:, None] - 1))
        w //= 2
    masks.append(np.eye(L, dtype=bool))
    m_f = np.stack(masks).astype(np.float32)
    return m_f, np.ascontiguousarray(m_f[:, ::-1, ::-1]), len(masks) - 1


def _pair_ref(b, w, reverse):
    L, d = b.shape
    off = w if reverse else w - 1
    if 2 * w > 8:
        pieces = [jnp.broadcast_to(b[p * 2 * w + off:p * 2 * w + off + 1, :], (2 * w, d))
                  for p in range(L // (2 * w))]
        return pieces[0] if len(pieces) == 1 else jnp.concatenate(pieces, axis=0)
    b3 = b.reshape(L // 8, 8, d)
    sub = lax.broadcasted_iota(jnp.int32, b3.shape, 1)
    ref = None
    for p in range(8 // (2 * w)):
        row = jnp.broadcast_to(b3[:, p * 2 * w + off:p * 2 * w + off + 1, :], b3.shape)
        ref = row if ref is None else jnp.where(sub >= p * 2 * w, row, ref)
    return ref.reshape(L, d)


def _hgrn2_kernel(qp_ref, ffp_ref, fbp_ref, ip_ref, gp_ref, lb_ref, nw_ref,
                  trif_ref, trib_ref, mf_ref, mb_ref, o_ref,
                  of_scr, ob_scr, stf_scr, stb_scr, *, L, nl):
    S = qp_ref.shape[0]
    nc = S // L
    lb = lb_ref[...]
    lb_floor = jnp.maximum(lb, LB_FLOOR)
    one_m_lb = 1.0 - lb

    def chunk(c, fpre_ref, tri_ref, m_ref, st_scr, reverse):
        r0 = pl.multiple_of(c * L, L)
        q = _silu(qp_ref[pl.ds(r0, L), :])
        v = ip_ref[pl.ds(r0, L), :]
        f = lb_floor + one_m_lb * _sigmoid(fpre_ref[pl.ds(r0, L), :])
        k = 1.0 - f
        hi, mid, lo = _split3(jnp.log2(f))
        tri = tri_ref[...]
        b = _dot(tri, hi) + _dot(tri, mid) + _dot(tri, lo)
        qb = q.astype(BF16)
        kb = k.astype(BF16)
        a = m_ref[nl] * _dot_nt(qb, kb)
        w = L // 2
        for lvl in range(nl):
            e = jnp.exp2(-jnp.abs(b - _pair_ref(b, w, reverse))).astype(BF16)
            a = a + m_ref[lvl] * _dot_nt(qb * e, kb * e)
            w //= 2
        tot_row = b[0:1] if reverse else b[L - 1:L]
        q_in = (q * jnp.exp2(b)).astype(BF16)
        k_out = (k * jnp.exp2(tot_row - b)).astype(BF16)
        st = st_scr[...]
        o = _dot(a.astype(BF16), v.astype(BF16)) + _dot_nt(q_in, st.astype(BF16))
        st_scr[...] = st * jnp.exp2(tot_row) + _dot(v.T.astype(BF16), k_out)
        return r0, o

    stf_scr[...] = jnp.zeros_like(stf_scr)
    stb_scr[...] = jnp.zeros_like(stb_scr)

    def body(ci, carry):
        r0, o = chunk(ci, ffp_ref, trif_ref, mf_ref, stf_scr, False)
        of_scr[pl.ds(r0, L), :] = o
        r0, o = chunk(nc - 1 - ci, fbp_ref, trib_ref, mb_ref, stb_scr, True)
        ob_scr[pl.ds(r0, L), :] = o
        return carry

    lax.fori_loop(0, nc, body, 0, unroll=4)
    nw = nw_ref[...]
    R = 4 * L if S % (4 * L) == 0 else L

    def finish(i, carry):
        r0 = pl.multiple_of(i * R, R)
        o = of_scr[pl.ds(r0, R), :] + ob_scr[pl.ds(r0, R), :]
        o = o * lax.rsqrt(jnp.mean(o * o, axis=-1, keepdims=True) + RMS_EPS)
        o = o * nw * _sigmoid(gp_ref[pl.ds(r0, R), :])
        o_ref[pl.ds(r0, R), :] = o.astype(o_ref.dtype)
        return carry

    lax.fori_loop(0, S // R, finish, 0)


def hgrn2(proj, lb, norm_w, L=HG_CHUNK):
    B, S, _ = proj.shape
    m_f, m_b, nl = _hgrn2_masks(L)
    t = np.arange(L)
    tri_f = jnp.asarray((t[None, :] <= t[:, None]).astype(np.float32), BF16)
    tri_b = jnp.asarray((t[None, :] >= t[:, None]).astype(np.float32), BF16)
    d = HG_DIM
    nb = HG_WIDTH // d

    def col(off):
        return pl.BlockSpec((None, S, d), lambda b, h, off=off: (b, 0, off // d + h))

    vec = pl.BlockSpec((1, d), lambda b, h: (0, h))
    const2 = lambda a: pl.BlockSpec(a.shape, lambda b, h: (0, 0))
    const3 = lambda a: pl.BlockSpec(a.shape, lambda b, h: (0, 0, 0))
    return pl.pallas_call(
        functools.partial(_hgrn2_kernel, L=L, nl=nl),
        grid=(B, nb),
        in_specs=[col(OFF_Q), col(OFF_FF), col(OFF_FB), col(OFF_I), col(OFF_G), vec, vec,
                  const2(tri_f), const2(tri_b), const3(m_f), const3(m_b)],
        out_specs=pl.BlockSpec((None, S, d), lambda b, h: (b, 0, h)),
        out_shape=jax.ShapeDtypeStruct((B, S, HG_WIDTH), BF16),
        scratch_shapes=[pltpu.VMEM((S, d), F32), pltpu.VMEM((S, d), F32),
                        pltpu.VMEM((d, d), F32), pltpu.VMEM((d, d), F32)],
        compiler_params=_cparams("parallel", "parallel"),
        name="hgrn2",
    )(proj, proj, proj, proj, proj, lb.reshape(1, HG_WIDTH), norm_w.reshape(1, HG_WIDTH),
      tri_f, tri_b, jnp.asarray(m_f), jnp.asarray(m_b))


def _shift_rows(x, k):
    if k == 0:
        return x
    n = x.shape[0]
    rolled = pltpu.roll(x, (-k) % n, 0)
    t = lax.broadcasted_iota(jnp.int32, x.shape, 0)
    ok = (t + k >= 0) & (t + k < n)
    return jnp.where(ok, rolled, 0.0)


def _conv_silu(x, w, b):
    half = w.shape[0] // 2
    acc = b
    for j in range(w.shape[0]):
        acc = acc + w[j:j + 1, :] * _shift_rows(x, j - half)
    return _silu(acc)


def _ssd_kernel(x_ref, b_ref, c_ref, z_ref, dtc_ref, dtr_ref,
                cwx_ref, cwb_ref, cwc_ref, cbx_ref, cbb_ref, cbc_ref,
                dtbc_ref, dtbr_ref, alr_ref, alc_ref, dsk_ref, nw_ref,
                tril_ref, triu_ref,
                o_ref, xs_scr, bs_scr, cs_scr, yf_scr, yb_scr, stf_scr, stb_scr, *, L, hg, P):
    S = x_ref.shape[0]
    nc = S // L
    nh = 2 * hg
    gw = hg * P
    N = bs_scr.shape[1]

    a_row = jnp.exp(alr_ref[...]) * NEG_LOG2E
    a_col = jnp.exp(alc_ref[...]) * NEG_LOG2E
    tril = tril_ref[...]
    triu = triu_ref[...]
    ti = lax.broadcasted_iota(jnp.int32, (L, L), 0)
    si = lax.broadcasted_iota(jnp.int32, (L, L), 1)
    lane2 = lax.broadcasted_iota(jnp.int32, (L, 2 * P), 1)

    def softplus(v):
        return jnp.maximum(v, 0.0) + jnp.log(1.0 + jnp.exp(-jnp.abs(v)))

    def per_head(cols):
        tiles = []
        for j in range(0, hg, 2):
            lo = jnp.broadcast_to(cols[j], (L, 2 * P))
            hi = jnp.broadcast_to(cols[j + 1], (L, 2 * P))
            tiles.append(jnp.where(lane2 < P, lo, hi))
        return jnp.concatenate(tiles, axis=1)

    def chunk(c, d, g, st_scr):
        cum_c = tril if d == 0 else triu
        cum_r = triu if d == 0 else tril
        r0 = pl.multiple_of(c * L, L)
        x = xs_scr[pl.ds(r0, L), :]
        bm = bs_scr[pl.ds(r0, L), :]
        cm = cs_scr[pl.ds(r0, L), :]
        dt_c = softplus(dtc_ref[pl.ds(r0, L), :] + dtbc_ref[...])
        dt_r = softplus(dtr_ref[:, pl.ds(r0, L)] + dtbr_ref[...])
        h1, h2, h3 = _split3(dt_c * a_row)
        acum_c = _dot(cum_c, h1) + _dot(cum_c, h2) + _dot(cum_c, h3)
        g1, g2, g3 = _split3(dt_r * a_col)
        acum_r = _dot(g1, cum_r) + _dot(g2, cum_r) + _dot(g3, cum_r)
        ones_rows = jnp.ones((8, L), BF16)
        tot = (_dot(ones_rows, h1) + _dot(ones_rows, h2) + _dot(ones_rows, h3))[0:1]
        cb = _dot_nt(cm.astype(BF16), bm.astype(BF16))
        keep = (si <= ti) if d == 0 else (si >= ti)
        ys, in_cols, out_cols, dec_cols = [], [], [], []
        for hh in range(hg):
            j = d * nh + g * hg + hh
            ac = acum_c[:, j:j + 1]
            dc = dt_c[:, j:j + 1]
            te = tot[:, j:j + 1]
            ar = acum_r[j:j + 1, :]
            dr = dt_r[j:j + 1, :]
            seg = ac - ar
            decay = jnp.where(keep, jnp.exp2(jnp.where(keep, seg, 0.0)), 0.0)
            w = (cb * decay * dr).astype(BF16)
            ys.append(_dot(w, x[:, hh * P:(hh + 1) * P].astype(BF16)))
            in_cols.append(jnp.exp2(te - ac) * dc)
            out_cols.append(jnp.exp2(ac))
            dec_cols.append(jnp.broadcast_to(jnp.exp2(te), (1, P)))
        y_diag = jnp.concatenate(ys, axis=1)
        st = st_scr[...]
        y_off = _dot(cm.astype(BF16), st.astype(BF16)) * per_head(out_cols)
        x_in = (x * per_head(in_cols)).astype(BF16)
        st_scr[...] = st * jnp.concatenate(dec_cols, axis=1) + _dot(bm.T.astype(BF16), x_in)
        return r0, y_diag + y_off

    R = 2 * L
    for g in range(x_ref.shape[1] // gw):
        cols = slice(g * gw, (g + 1) * gw)
        ncols = slice(g * N, (g + 1) * N)
        xs_scr[...] = _conv_silu(x_ref[:, cols], cwx_ref[:, cols], cbx_ref[:, cols])
        bs_scr[...] = _conv_silu(b_ref[:, ncols], cwb_ref[:, ncols], cbb_ref[:, ncols])
        cs_scr[...] = _conv_silu(c_ref[:, ncols], cwc_ref[:, ncols], cbc_ref[:, ncols])
        stf_scr[...] = jnp.zeros_like(stf_scr)
        stb_scr[...] = jnp.zeros_like(stb_scr)

        def body(ci, carry, g=g):
            r0, y = chunk(ci, 0, g, stf_scr)
            yf_scr[pl.ds(r0, L), :] = y
            r0, y = chunk(nc - 1 - ci, 1, g, stb_scr)
            yb_scr[pl.ds(r0, L), :] = y
            return carry

        lax.fori_loop(0, nc, body, 0, unroll=2)

        def finish(i, carry, cols=cols):
            r0 = pl.multiple_of(i * R, R)
            y = (yf_scr[pl.ds(r0, R), :] + yb_scr[pl.ds(r0, R), :]
                 + dsk_ref[:, cols] * xs_scr[pl.ds(r0, R), :])
            y = y * _silu(z_ref[pl.ds(r0, R), cols])
            y = y * lax.rsqrt(jnp.mean(y * y, axis=-1, keepdims=True) + RMS_EPS)
            o_ref[pl.ds(r0, R), cols] = (y * nw_ref[:, cols]).astype(o_ref.dtype)
            return carry

        lax.fori_loop(0, S // R, finish, 0)


def ssd(proj, dt_rows, conv_w, conv_b, dt_bias, a_log, d_skip, norm_w, L=M2_CHUNK):
    B, S, _ = proj.shape
    hg = M2_HEADS // M2_GROUPS
    gw = hg * M2_P
    t = np.arange(L)
    tril = jnp.asarray((t[None, :] <= t[:, None]).astype(np.float32), BF16)
    triu = jnp.asarray((t[None, :] >= t[:, None]).astype(np.float32), BF16)
    nb_c = OFF_XBC + M2_INNER
    nc_c = nb_c + M2_GROUPS * M2_N
    xw = conv_w[:, :M2_INNER]
    bw = conv_w[:, M2_INNER:M2_INNER + M2_GROUPS * M2_N]
    cw = conv_w[:, M2_INNER + M2_GROUPS * M2_N:]
    cb2 = conv_b.reshape(1, -1)
    xb = cb2[:, :M2_INNER]
    bb = cb2[:, M2_INNER:M2_INNER + M2_GROUPS * M2_N]
    cbb = cb2[:, M2_INNER + M2_GROUPS * M2_N:]
    nh2 = 2 * M2_HEADS
    dtb_row = jnp.zeros((1, LANE), F32).at[0, :nh2].set(dt_bias.reshape(-1))
    al_row = jnp.zeros((1, LANE), F32).at[0, :nh2].set(a_log.reshape(-1))
    dtb_col = dt_bias.reshape(nh2, 1)
    al_col = a_log.reshape(nh2, 1)
    dsk = jnp.repeat(d_skip, M2_P).reshape(1, M2_INNER)
    nw = norm_w.reshape(1, M2_INNER)

    gn = M2_GROUPS * M2_N
    full2 = lambda a: pl.BlockSpec(a.shape, lambda b: (0, 0))
    return pl.pallas_call(
        functools.partial(_ssd_kernel, L=L, hg=hg, P=M2_P),
        grid=(B,),
        in_specs=[
            pl.BlockSpec((None, S, M2_INNER), lambda b: (b, 0, OFF_XBC // M2_INNER)),
            pl.BlockSpec((None, S, gn), lambda b: (b, 0, nb_c // gn)),
            pl.BlockSpec((None, S, gn), lambda b: (b, 0, nc_c // gn)),
            pl.BlockSpec((None, S, M2_INNER), lambda b: (b, 0, OFF_Z // M2_INNER)),
            pl.BlockSpec((None, S, LANE), lambda b: (b, 0, OFF_DT // LANE)),
            pl.BlockSpec((nh2, S), lambda b: (0, b)),
            full2(xw), full2(bw), full2(cw), full2(xb), full2(bb), full2(cbb),
            full2(dtb_row), full2(dtb_col), full2(al_row), full2(al_col),
            full2(dsk), full2(nw), full2(tril), full2(triu),
        ],
        out_specs=pl.BlockSpec((None, S, M2_INNER), lambda b: (b, 0, 0)),
        out_shape=jax.ShapeDtypeStruct((B, S, M2_INNER), BF16),
        scratch_shapes=[pltpu.VMEM((S, gw), F32), pltpu.VMEM((S, M2_N), F32),
                        pltpu.VMEM((S, M2_N), F32), pltpu.VMEM((S, gw), F32),
                        pltpu.VMEM((S, gw), F32), pltpu.VMEM((M2_N, gw), F32),
                        pltpu.VMEM((M2_N, gw), F32)],
        compiler_params=_cparams("parallel"),
        name="ssd",
    )(proj, proj, proj, proj, proj, dt_rows, xw, bw, cw, xb, bb, cbb,
      dtb_row, dtb_col, al_row, al_col, dsk, nw, tril, triu)


def _ffn_kernel(h_ref, oa_ref, ob_ref, wa_ref, wb_ref, nw_ref, wg_ref, wu_ref, wd_ref, o_ref,
                xn_scr, acc_scr):
    f = pl.program_id(1)

    @pl.when(f == 0)
    def _():
        x = h_ref[...] + _dot(oa_ref[...], wa_ref[...]) + _dot(ob_ref[...], wb_ref[...])
        xn_scr[...] = _rms(x, nw_ref[...]).astype(BF16)
        acc_scr[...] = x

    xn = xn_scr[...]
    act = _silu(_dot(xn, wg_ref[...].astype(BF16))) * _dot(xn, wu_ref[...].astype(BF16))
    acc_scr[...] += _dot(act.astype(BF16), wd_ref[...].astype(BF16))

    @pl.when(f == pl.num_programs(1) - 1)
    def _():
        o_ref[...] = acc_scr[...]


def ffn(h, oa, ob, w_out, nw, w_gu, w_down, layer, tm=1024, tf=512):
    m, d = h.shape
    dff = w_down.shape[1]
    nf = dff // tf
    ka, kb = oa.shape[1], ob.shape[1]
    assert ka == kb
    return pl.pallas_call(
        _ffn_kernel,
        grid=(m // tm, nf),
        in_specs=[pl.BlockSpec((tm, d), lambda i, f: (i, 0)),
                  pl.BlockSpec((tm, ka), lambda i, f: (i, 0)),
                  pl.BlockSpec((tm, kb), lambda i, f: (i, 0)),
                  pl.BlockSpec((None, ka, d), lambda i, f: (layer, 0, 0)),
                  pl.BlockSpec((None, kb, d), lambda i, f: (layer, 1, 0)),
                  pl.BlockSpec((1, d), lambda i, f: (0, 0)),
                  pl.BlockSpec((None, d, tf), lambda i, f: (layer, 0, f)),
                  pl.BlockSpec((None, d, tf), lambda i, f: (layer, 0, nf + f)),
                  pl.BlockSpec((None, tf, d), lambda i, f: (layer, f, 0))],
        out_specs=pl.BlockSpec((tm, d), lambda i, f: (i, 0)),
        out_shape=jax.ShapeDtypeStruct((m, d), F32),
        scratch_shapes=[pltpu.VMEM((tm, d), BF16), pltpu.VMEM((tm, d), F32)],
        compiler_params=_cparams("parallel", "arbitrary"),
        name="ffn",
    )(h, oa, ob, w_out, w_out, nw.reshape(1, d), w_gu, w_gu, w_down)


def _ple_kernel(*refs, moe, final):
    if moe:
        h_ref, y2_ref, meta_ref, p_ref, nw_ref, wg_ref, wp_ref, fw_ref, o_ref = refs
        meta = meta_ref[...]
        d = h_ref.shape[1]
        h = h_ref[...] + (meta[:, 2:3] * y2_ref[:, 0:d] + meta[:, 3:4] * y2_ref[:, d:2 * d])
    else:
        h_ref, p_ref, nw_ref, wg_ref, wp_ref, fw_ref, o_ref = refs
        h = h_ref[...]
    gate = _sigmoid(_dot(_rms(h, nw_ref[...]).astype(BF16), wg_ref[...]))
    h = h + gate * _dot(p_ref[...].astype(BF16), wp_ref[...])
    if final:
        h = _rms(h, fw_ref[...])
    o_ref[...] = h


def ple(h, p, nw, wg, wp, fw, layer, y2=None, meta=None, final=False, tm=1024):
    m, d = h.shape
    moe = y2 is not None
    row = lambda w: pl.BlockSpec((tm, w), lambda i: (i, 0))
    slab = lambda a: pl.BlockSpec((None,) + a.shape[1:], lambda i: (layer, 0, 0))
    vec = pl.BlockSpec((1, d), lambda i: (0, 0))
    in_specs = [row(d)]
    args = [h]
    if moe:
        in_specs += [row(2 * d), row(meta.shape[1])]
        args += [y2, meta]
    in_specs += [pl.BlockSpec((None, tm, p.shape[2]), lambda i: (layer, i, 0)), vec, slab(wg), slab(wp), vec]
    args += [p, nw.reshape(1, d), wg, wp, fw.reshape(1, d)]
    return pl.pallas_call(
        functools.partial(_ple_kernel, moe=moe, final=final),
        grid=(m // tm,),
        in_specs=in_specs,
        out_specs=row(d),
        out_shape=jax.ShapeDtypeStruct((m, d), F32),
        compiler_params=_cparams("parallel"),
        name="ple",
    )(*args)


def _s5_strips(a_re, a_im, log_step, b_re, b_im, c_re, c_im):
    G, N = a_re.shape[1:]
    C = S5_GROUP_SIZE
    T = S5_STEPS
    gt = LANE // C
    Z = G // gt
    tau = jnp.arange(T + 1, dtype=F32)
    steps = jnp.arange(T)

    def cmul(xr, xi, yr, yi):
        return xr * yr - xi * yi, xr * yi + xi * yr

    st_parts, rd_parts, k_parts, lam_rows = [], [], [], []
    for d in range(2):
        delta = jnp.exp(log_step[d])[:, None]
        ar, ai = a_re[d], a_im[d]
        mag = jnp.exp(ar * delta)
        lam_re, lam_im = mag * jnp.cos(ai * delta), mag * jnp.sin(ai * delta)
        den = ar * ar + ai * ai
        num_re = lam_re - 1.0
        coef_re = (num_re * ar + lam_im * ai) / den
        coef_im = (lam_im * ar - num_re * ai) / den
        br = coef_re[..., None] * b_re - coef_im[..., None] * b_im
        bi = coef_re[..., None] * b_im + coef_im[..., None] * b_re
        cr, ci = c_re[d], c_im[d]
        pm = jnp.exp((ar * delta)[None] * tau[:, None, None])
        ang = (ai * delta)[None] * tau[:, None, None]
        pr, pi = pm * jnp.cos(ang), pm * jnp.sin(ang)
        e_in = (T - 1 - steps) if d == 0 else steps
        sr, si = cmul(pr[e_in][..., None], pi[e_in][..., None], br[None], bi[None])
        e_out = (steps + 1) if d == 0 else (T - steps)
        cpr, cpi = cmul(cr[None], ci[None], pr[e_out][:, :, None, :], pi[e_out][:, :, None, :])
        lbr, lbi = cmul(pr[:T][..., None], pi[:T][..., None], br[None], bi[None])
        ktau = jnp.einsum('gon,tgni->tgoi', cr, lbr) - jnp.einsum('gon,tgni->tgoi', ci, lbi)
        st_parts += [sr, si]
        rd_parts += [cpr, -cpi]
        k_parts.append(ktau)
        lam_rows += [pr[T], pi[T]]

    NS = gt * N
    st = jnp.stack(st_parts).reshape(4, T, Z, gt, N, C).transpose(2, 1, 5, 0, 3, 4)
    rd = jnp.stack(rd_parts).reshape(4, T, Z, gt, C, N).transpose(2, 1, 4, 0, 3, 5)
    lag = steps[None, :] - steps[:, None]
    sel_f = (lag[:, :, None] == steps[None, None, :]).astype(F32)
    sel_b = (-lag[:, :, None] == steps[None, None, :]).astype(F32)
    toe = (jnp.einsum('stk,kgoi->stgoi', sel_f, k_parts[0])
           + jnp.einsum('stk,kgoi->stgoi', sel_b, k_parts[1]))
    toe = toe.reshape(T, T, Z, gt, C, C).transpose(2, 0, 5, 1, 3, 4)
    lam = jnp.stack([r.reshape(Z, NS) for r in lam_rows], axis=1)
    lam = jnp.concatenate([lam, jnp.zeros_like(lam)], axis=1)
    return (st.reshape(Z, T * C, 4 * NS), toe.reshape(Z, T * C, T * LANE),
            rd.reshape(Z, T * C, 4 * NS), lam)


def s5_tables(a_re, a_im, log_step, b_re, b_im, c_re, c_im):
    st, toe, rd, lam = jax.vmap(_s5_strips)(a_re, a_im, log_step, b_re, b_im, c_re, c_im)
    flat = lambda a: a.reshape((-1,) + a.shape[2:])
    wst = block_diag_rows(flat(st), S5_GROUP_SIZE, S5_STATE)
    kin = block_diag_rows(flat(toe), S5_GROUP_SIZE, S5_GROUP_SIZE)
    wc = block_diag_rows(flat(rd), S5_GROUP_SIZE, S5_STATE)
    return wst, kin, wc, flat(lam)


def _block_diag_kernel(s_ref, o_ref, *, C, gcol):
    n = o_ref.shape[1]
    gt = LANE // C
    row_g = lax.broadcasted_iota(jnp.int32, (LANE, n), 0) // C
    col_g = (lax.broadcasted_iota(jnp.int32, (LANE, n), 1) // gcol) % gt
    keep = row_g == col_g
    for t in range(o_ref.shape[0] // LANE):
        strip = s_ref[t * C:(t + 1) * C, :]
        tiled = jnp.broadcast_to(strip[None], (gt, C, n)).reshape(LANE, n)
        o_ref[t * LANE:(t + 1) * LANE, :] = jnp.where(keep, tiled, 0.0).astype(o_ref.dtype)


def block_diag_rows(strips, C, gcol):
    Z, rows, n = strips.shape
    gt = LANE // C
    return pl.pallas_call(
        functools.partial(_block_diag_kernel, C=C, gcol=gcol),
        grid=(Z,),
        in_specs=[pl.BlockSpec((None, rows, n), lambda z: (z, 0, 0))],
        out_specs=pl.BlockSpec((None, rows * gt, n), lambda z: (z, 0, 0)),
        out_shape=jax.ShapeDtypeStruct((Z, rows * gt, n), BF16),
        compiler_params=_cparams("parallel"),
        name="block_diag_rows",
    )(strips)


def _s5_pre_kernel(x_ref, nw_ref, u_ref, xn_scr, tmp_scr):
    nb, R, D = x_ref.shape
    T = S5_STEPS
    rc = R // T
    xn = _rms(x_ref[...], nw_ref[...])
    for z in range(D // LANE):
        xn_scr[z] = xn[:, :, z * LANE:(z + 1) * LANE]
    for z in range(D // LANE):
        for t in range(T):
            for b in range(nb):
                tmp_scr[z, t, pl.ds(b, rc, stride=nb), :] = xn_scr[z, b, pl.ds(t, rc, stride=T), :]
            u_ref[z, :, t * LANE:(t + 1) * LANE] = tmp_scr[z, t].astype(u_ref.dtype)


def s5_pre(h3, nw, R=64):
    B, S, D = h3.shape
    T = S5_STEPS
    Z = D // LANE
    rc = R // T
    return pl.pallas_call(
        _s5_pre_kernel,
        grid=(S // R,),
        in_specs=[pl.BlockSpec((B, R, D), lambda i: (0, i, 0)),
                  pl.BlockSpec((1, D), lambda i: (0, 0))],
        out_specs=pl.BlockSpec((Z, rc * B, T * LANE), lambda i: (0, i, 0)),
        out_shape=jax.ShapeDtypeStruct((Z, S // T * B, T * LANE), BF16),
        scratch_shapes=[pltpu.VMEM((Z, B, R, LANE), F32), pltpu.VMEM((Z, T, rc * B, LANE), F32)],
        compiler_params=_cparams("parallel"),
        name="s5_pre",
    )(h3, nw.reshape(1, D))


def _s5_kernel(u_ref, wst_ref, kin_ref, wc_ref, lam_ref, y_ref, s_scr, *, nb, rblk):
    rows = u_ref.shape[0]
    nc = rows // nb
    ns = lam_ref.shape[1]
    for r0 in range(0, rows, rblk):
        s_scr[r0:r0 + rblk, :] = _dot(u_ref[r0:r0 + rblk, :], wst_ref[...])
    lam = lam_ref[...]
    lfr, lfi, lbr, lbi = lam[0:1], lam[1:2], lam[2:3], lam[3:4]

    def body(ci, carry):
        hfr, hfi, hbr, hbi = carry
        rf = pl.multiple_of(ci * nb, nb)
        rb = pl.multiple_of((nc - 1 - ci) * nb, nb)
        sfr = s_scr[pl.ds(rf, nb), 0:ns]
        sfi = s_scr[pl.ds(rf, nb), ns:2 * ns]
        sbr = s_scr[pl.ds(rb, nb), 2 * ns:3 * ns]
        sbi = s_scr[pl.ds(rb, nb), 3 * ns:4 * ns]
        s_scr[pl.ds(rf, nb), 0:ns] = hfr
        s_scr[pl.ds(rf, nb), ns:2 * ns] = hfi
        s_scr[pl.ds(rb, nb), 2 * ns:3 * ns] = hbr
        s_scr[pl.ds(rb, nb), 3 * ns:4 * ns] = hbi
        return (lfr * hfr - lfi * hfi + sfr, lfr * hfi + lfi * hfr + sfi,
                lbr * hbr - lbi * hbi + sbr, lbr * hbi + lbi * hbr + sbi)

    z = jnp.zeros((nb, ns), F32)
    lax.fori_loop(0, nc, body, (z, z, z, z))
    for r0 in range(0, rows, rblk):
        y = (_dot(u_ref[r0:r0 + rblk, :], kin_ref[...])
             + _dot_nt(s_scr[r0:r0 + rblk, :].astype(BF16), wc_ref[...]))
        y_ref[r0:r0 + rblk, :] = y.astype(y_ref.dtype)


def s5_scan(u, wst, kin, wc, lam, nb, layer, rblk=512):
    Z, rows, K = u.shape
    z0 = layer * Z
    per = lambda a: pl.BlockSpec((None,) + a.shape[1:], lambda s: (s, 0, 0))
    tab = lambda a: pl.BlockSpec((None,) + a.shape[1:], lambda s: (z0 + s, 0, 0))
    once = lambda a: pl.BlockSpec((None,) + a.shape[1:], lambda s: (z0 + s, 0, 0))
    return pl.pallas_call(
        functools.partial(_s5_kernel, nb=nb, rblk=rblk),
        grid=(Z,),
        in_specs=[per(u), once(wst), once(kin), once(wc), tab(lam)],
        out_specs=pl.BlockSpec((None, rows, K), lambda s: (s, 0, 0)),
        out_shape=jax.ShapeDtypeStruct((Z, rows, K), BF16),
        scratch_shapes=[pltpu.VMEM((rows, wst.shape[2]), F32)],
        compiler_params=_cparams("parallel"),
        name="s5_scan",
    )(u, wst, kin, wc, lam)


def _s5_post_kernel(h_ref, y_ref, nw_ref, d_ref, wo_ref, wg_ref, o_ref, yf_scr, yt_scr):
    nb, R, D = h_ref.shape
    T = S5_STEPS
    rc = R // T
    for z in range(D // LANE):
        for t in range(T):
            yf_scr[z, t] = y_ref[z, :, t * LANE:(t + 1) * LANE].astype(F32)
            for b in range(nb):
                yt_scr[z, b, pl.ds(t, rc, stride=T), :] = yf_scr[z, t, pl.ds(b, rc, stride=nb), :]
    yt = jnp.concatenate([yt_scr[z] for z in range(D // LANE)], axis=-1)
    h = h_ref[...].reshape(nb * R, D)
    y = yt.reshape(nb * R, D) + d_ref[...] * _rms(h, nw_ref[...])
    act = jax.nn.gelu(y).astype(BF16)
    out = h + _dot(act, wo_ref[...]) * _sigmoid(_dot(act, wg_ref[...]))
    o_ref[...] = out.reshape(nb, R, D)


def s5_post(h3, y, nw, d_skip, glu_w, layer, R=64):
    B, S, D = h3.shape
    T = S5_STEPS
    Z = D // LANE
    rc = R // T
    row = pl.BlockSpec((B, R, D), lambda i: (0, i, 0))
    vec = pl.BlockSpec((1, D), lambda i: (0, 0))
    return pl.pallas_call(
        _s5_post_kernel,
        grid=(S // R,),
        in_specs=[row, pl.BlockSpec((Z, rc * B, T * LANE), lambda i: (0, i, 0)), vec, vec,
                  pl.BlockSpec((None, D, D), lambda i: (layer, 0, 0)),
                  pl.BlockSpec((None, D, D), lambda i: (layer, 0, 1))],
        out_specs=row,
        out_shape=jax.ShapeDtypeStruct((B, S, D), F32),
        scratch_shapes=[pltpu.VMEM((Z, T, rc * B, LANE), F32), pltpu.VMEM((Z, B, R, LANE), F32)],
        compiler_params=_cparams("parallel"),
        name="s5_post",
    )(h3, y, nw.reshape(1, D), d_skip.reshape(1, D), glu_w, glu_w)


def _router_kernel(h_ref, nw_ref, whi_ref, wlo_ref, tri_ref, meta_ref, cnt_ref, carry_scr):
    @pl.when(pl.program_id(0) == 0)
    def _():
        carry_scr[...] = jnp.zeros_like(carry_scr)

    xn = _rms(h_ref[...], nw_ref[...])
    x_hi, x_lo, _ = _split3(xn)
    w_hi, w_lo = whi_ref[...], wlo_ref[...]
    logits = _dot(x_hi, w_hi) + (_dot(x_hi, w_lo) + _dot(x_lo, w_hi))
    lane = lax.broadcasted_iota(jnp.int32, logits.shape, 1)
    neg = jnp.float32(-jnp.inf)
    lg = jnp.where(lane < N_EXPERTS, logits, neg)
    t1 = jnp.max(lg, axis=1, keepdims=True)
    i1 = jnp.min(jnp.where(lg == t1, lane, LANE), axis=1, keepdims=True)
    lg2 = jnp.where(lane == i1, neg, lg)
    t2 = jnp.max(lg2, axis=1, keepdims=True)
    i2 = jnp.min(jnp.where(lg2 == t2, lane, LANE), axis=1, keepdims=True)
    ex = jnp.exp(t2 - t1)
    g1 = 1.0 / (1.0 + ex)
    g2 = ex / (1.0 + ex)
    oh1 = jnp.where(lane == i1, 1.0, 0.0)
    oh2 = jnp.where(lane == i2, 1.0, 0.0)
    tri = tri_ref[...]
    before1 = _dot(tri, oh1.astype(BF16))
    before2 = _dot(tri, oh2.astype(BF16))
    tot1 = jnp.sum(oh1, axis=0, keepdims=True)
    tot2 = jnp.sum(oh2, axis=0, keepdims=True)
    carry = carry_scr[0:1]
    rank1 = jnp.sum(oh1 * (carry + before1), axis=1, keepdims=True)
    rank2 = jnp.sum(oh2 * (carry + tot1 + before2), axis=1, keepdims=True)
    counts = jnp.broadcast_to(carry + tot1 + tot2, carry_scr.shape)
    carry_scr[...] = counts
    cnt_ref[...] = counts
    meta = jnp.where(lane == 0, i1.astype(F32), 0.0)
    meta = jnp.where(lane == 1, i2.astype(F32), meta)
    meta = jnp.where(lane == 2, g1, meta)
    meta = jnp.where(lane == 3, g2, meta)
    meta = jnp.where(lane == 4, rank1, meta)
    meta = jnp.where(lane == 5, rank2, meta)
    meta_ref[...] = meta


def router(h, nw, w_router, tm=512):
    m, d = h.shape
    wr = jnp.zeros((d, LANE), F32).at[:, :N_EXPERTS].set(w_router)
    w_hi, w_lo, _ = _split3(wr)
    t = np.arange(tm)
    tri = jnp.asarray((t[None, :] < t[:, None]).astype(np.float32), BF16)
    return pl.pallas_call(
        _router_kernel,
        grid=(m // tm,),
        in_specs=[pl.BlockSpec((tm, d), lambda i: (i, 0)),
                  pl.BlockSpec((1, d), lambda i: (0, 0)),
                  pl.BlockSpec((d, LANE), lambda i: (0, 0)),
                  pl.BlockSpec((d, LANE), lambda i: (0, 0)),
                  pl.BlockSpec((tm, tm), lambda i: (0, 0))],
        out_specs=[pl.BlockSpec((tm, LANE), lambda i: (i, 0)),
                   pl.BlockSpec((8, LANE), lambda i: (0, 0))],
        out_shape=[jax.ShapeDtypeStruct((m, LANE), F32), jax.ShapeDtypeStruct((8, LANE), F32)],
        scratch_shapes=[pltpu.VMEM((8, LANE), F32)],
        compiler_params=_cparams("arbitrary"),
        name="router",
    )(h, nw.reshape(1, d), w_hi, w_lo, tri)


def _moe_kernel(arow_ref, blk_e_ref, nvalid_ref, h_hbm, nw_ref, wg_ref, wu_ref, wd_ref,
                y2_hbm, xg_scr, xn_scr, acc_scr, sem_in, sem_out, *, bm, n_tok):
    g = pl.program_id(0)
    f = pl.program_id(1)
    nf = pl.num_programs(1)
    slot = g % 2
    other = 1 - slot
    d = acc_scr.shape[2]
    rows_per_step = bm // (nf - 1)
    prv, cur, nxt = g * bm, (g + 1) * bm, (g + 2) * bm
    nv = nvalid_ref[g]

    def in_copy(off, r, s):
        tok = jnp.minimum(arow_ref[off + r] >> 1, n_tok - 1)
        return pltpu.make_async_copy(h_hbm.at[pl.ds(tok, 1)], xg_scr.at[s, pl.ds(r, 1)], sem_in.at[s])

    def out_copy(off, r, s):
        a = arow_ref[off + r]
        col = pl.multiple_of((a & 1) * d, d)
        return pltpu.make_async_copy(acc_scr.at[s, pl.ds(r, 1)],
                                     y2_hbm.at[pl.ds(a >> 1, 1), pl.ds(col, d)], sem_out.at[s])

    def wait_in(s):
        pltpu.make_async_copy(h_hbm.at[pl.ds(0, bm)], xg_scr.at[s], sem_in.at[s]).wait()

    def wait_out(s):
        pltpu.make_async_copy(acc_scr.at[s], y2_hbm.at[pl.ds(0, bm), pl.ds(0, d)], sem_out.at[s]).wait()

    @pl.when((g == 0) & (f == 0))
    def _():
        acc_scr[...] = jnp.zeros_like(acc_scr)

        def start(r, c):
            in_copy(cur, r, slot).start()
            return c

        lax.fori_loop(0, bm, start, 0)

    @pl.when(f == 0)
    def _():
        wait_in(slot)
        xn_scr[...] = _rms(xg_scr[slot], nw_ref[...]).astype(BF16)
        acc_scr[slot] = jnp.zeros((bm, d), F32)

    def move_rows(part, parts):
        base = pl.multiple_of(f * rows_per_step, 8)
        per = rows_per_step // parts
        for j in range(part * per, (part + 1) * per):
            in_copy(nxt, base + j, other).start()
            out_copy(prv, base + j, other).start()

    def experts(with_moves):
        parts = 2
        tc = wg_ref.shape[2] // parts
        xn = xn_scr[...]
        upd = None
        for c in range(parts):
            cs = slice(c * tc, (c + 1) * tc)
            if with_moves:
                move_rows(c, parts)
            act = (_silu(_dot(xn, wg_ref[0, :, cs].astype(BF16)))
                   * _dot(xn, wu_ref[0, :, cs].astype(BF16)))
            part = _dot(act.astype(BF16), wd_ref[0, cs, :].astype(BF16))
            upd = part if upd is None else upd + part
        acc_scr[slot] += upd

    @pl.when((nv > 0) & (f < nf - 1))
    def _():
        experts(True)

    @pl.when((nv > 0) & (f == nf - 1))
    def _():
        experts(False)

    @pl.when((nv == 0) & (f < nf - 1))
    def _():
        move_rows(0, 1)

    @pl.when(f == nf - 1)
    def _():
        wait_out(other)

    @pl.when((f == nf - 1) & (g == pl.num_programs(0) - 1))
    def _():
        wait_in(other)


def moe_experts(h, nw, w_gu, w_down, arow, blk_e, nvalid, layer, bm=MOE_BM, tf=512):
    t, d = h.shape
    dff = w_down.shape[1]
    nf = dff // tf
    assert bm % (2 * (nf - 1)) == 0 and bm % 16 == 0
    n_steps = blk_e.shape[0]
    e0 = layer * N_EXPERTS

    def wmap(col0):
        def index(g, f, ar, be, nv):
            return (e0 + be[g], 0, col0 + jnp.where(nv[g] > 0, f, nf - 1))
        return index

    def dmap(g, f, ar, be, nv):
        return (e0 + be[g], jnp.where(nv[g] > 0, f, nf - 1), 0)

    grid_spec = pltpu.PrefetchScalarGridSpec(
        num_scalar_prefetch=3,
        grid=(n_steps, nf),
        in_specs=[pl.BlockSpec(memory_space=pl.ANY),
                  pl.BlockSpec((1, d), lambda g, f, ar, be, nv: (0, 0)),
                  pl.BlockSpec((1, d, tf), wmap(0)),
                  pl.BlockSpec((1, d, tf), wmap(nf)),
                  pl.BlockSpec((1, tf, d), dmap)],
        out_specs=pl.BlockSpec(memory_space=pl.ANY),
        scratch_shapes=[pltpu.VMEM((2, bm, d), F32), pltpu.VMEM((bm, d), BF16),
                        pltpu.VMEM((2, bm, d), F32),
                        pltpu.SemaphoreType.DMA((2,)), pltpu.SemaphoreType.DMA((2,))],
    )
    return pl.pallas_call(
        functools.partial(_moe_kernel, bm=bm, n_tok=t),
        grid_spec=grid_spec,
        out_shape=jax.ShapeDtypeStruct((t + bm // 2, 2 * d), F32),
        compiler_params=_cparams("arbitrary", "arbitrary"),
        name="moe_experts",
    )(arow, blk_e, nvalid, h, nw.reshape(1, d), w_gu, w_gu, w_down)


def _moe_plan(meta, counts, bm):
    t = meta.shape[0]
    n_assign = 2 * t
    experts = jnp.arange(N_EXPERTS, dtype=jnp.int32)
    cnt = counts[0, :N_EXPERTS].astype(jnp.int32)
    padded = (cnt + bm - 1) // bm * bm
    pend = jnp.cumsum(padded)
    pstart = pend - padded
    e = meta[:, 0:2].astype(jnp.int32)
    rank = meta[:, 4:6].astype(jnp.int32)
    dest = jnp.sum(jnp.where(e[..., None] == experts, pstart, 0), axis=-1) + rank
    n_blocks = -(-n_assign // bm) + N_EXPERTS
    spare = 2 * t + jnp.arange(bm, dtype=jnp.int32)
    arow = jnp.tile(spare, n_blocks).at[dest.reshape(-1)].set(
        jnp.arange(n_assign, dtype=jnp.int32), unique_indices=True, mode='promise_in_bounds')
    arow = jnp.concatenate([spare, arow, spare, spare])
    blk_start = jnp.arange(n_blocks, dtype=jnp.int32) * bm
    blk_e = jnp.minimum(jnp.sum((pend[None, :] <= blk_start[:, None]).astype(jnp.int32), axis=1),
                        N_EXPERTS - 1)
    nvalid = jnp.clip((pstart + cnt)[blk_e] - blk_start, 0, bm)
    nvalid = jnp.where(blk_start < pend[-1], nvalid, 0).astype(jnp.int32)
    blk_e = jnp.concatenate([blk_e, blk_e[-1:]]).astype(jnp.int32)
    nvalid = jnp.concatenate([nvalid, jnp.zeros((1,), jnp.int32)])
    return arow, blk_e, nvalid


def kernel(x, p, norm_mix, norm_ffn, norm_ple, final_norm, ple_gate, ple_proj, ev_w_in, ev_w_out,
           hg_lb_logits, hg_norm_w, m2_conv_w, m2_conv_b, m2_dt_bias, m2_a_log, m2_d, m2_norm_w,
           s5_a_re, s5_a_im, s5_log_step, s5_b_re, s5_b_im, s5_c_re, s5_c_im, s5_d, s5_glu_w,
           ffn_w_gu, ffn_w_down, moe_router, moe_w_gu, moe_w_down):
    B, S, D = x.shape
    T = B * S
    depth = norm_mix.shape[0]
    lb_soft = jax.nn.softmax(hg_lb_logits.astype(F32), axis=0)
    hg_lb = jnp.cumsum(lb_soft, axis=0) - lb_soft[0]
    w_in = jnp.pad(ev_w_in, ((0, 0), (0, 0), (0, EVEN_IN_PAD - ev_w_in.shape[2]))).astype(BF16)
    w_dt_t = jnp.swapaxes(ev_w_in[:, :, OFF_DT:], 1, 2).astype(BF16)
    w_out = ev_w_out.astype(BF16)
    w_moe_gu = moe_w_gu.reshape((-1,) + moe_w_gu.shape[2:])
    w_moe_down = moe_w_down.reshape((-1,) + moe_w_down.shape[2:])
    w_glu = s5_glu_w.astype(BF16)
    w_ple_gate, w_ple_proj = ple_gate.astype(BF16), ple_proj.astype(BF16)
    p_rows = p.reshape(depth, T, p.shape[-1])
    wst, kin, wc, lam = s5_tables(s5_a_re, s5_a_im, s5_log_step, s5_b_re, s5_b_im, s5_c_re, s5_c_im)
    h = x.reshape(T, D)
    for layer in range(depth):
        j = layer // 2
        if layer % 2 == 0:
            proj, dt_rows = rms_matmul(h, norm_mix[layer], w_in, w_dt_t, j)
            proj = proj.reshape(B, S, EVEN_IN_PAD)
            o_a = hgrn2(proj, hg_lb[j], hg_norm_w[j])
            o_b = ssd(proj, dt_rows, m2_conv_w[j], m2_conv_b[j], m2_dt_bias[j], m2_a_log[j],
                      m2_d[j], m2_norm_w[j])
            h = ffn(h, o_a.reshape(T, HG_WIDTH), o_b.reshape(T, M2_INNER), w_out,
                    norm_ffn[layer], ffn_w_gu, ffn_w_down, j)
            y2 = meta = None
        else:
            h3 = h.reshape(B, S, D)
            y = s5_scan(s5_pre(h3, norm_mix[layer]), wst, kin, wc, lam, B, j)
            h = s5_post(h3, y, norm_mix[layer], s5_d[j], w_glu, j).reshape(T, D)
            meta, counts = router(h, norm_ffn[layer], moe_router[j])
            arow, blk_e, nvalid = _moe_plan(meta, counts, MOE_BM)
            y2 = moe_experts(h, norm_ffn[layer], w_moe_gu, w_moe_down, arow, blk_e, nvalid, j)
        h = ple(h, p_rows, norm_ple[layer], w_ple_gate, w_ple_proj, final_norm, layer,
                y2=y2, meta=meta, final=(layer == depth - 1))
    return h.reshape(B, S, D)
```
